```python
import math
import jax, jax.numpy as jnp
from jax import lax
import numpy as np

D_MODEL = 4096
BATCH = 4
SEQ = 2048
DEPTH = 4
DEC_BATCH = 8
DEC_SEQ = 8
PAST_LEN = 8192
PAGE_SIZE = 128

N_EVEN = (DEPTH + 1) // 2
N_ODD = DEPTH // 2
EPS = 1e-6
HEAD_DIM = 128
SCALE = HEAD_DIM ** -0.5

MOBA_HEADS = 16
MOBA_W = MOBA_HEADS * HEAD_DIM
MOBA_BLOCK = 256
MOBA_TOPK = 3
MOBA_QB = 8

D_INNER = D_MODEL
SSD_HEAD_DIM = 64
SSD_HEADS = D_INNER // SSD_HEAD_DIM
SSD_GROUPS = 8
SSD_STATE = 128
SSD_CONV = 4
SSD_CHUNK = 128
CONV_DIM = D_INNER + 2 * SSD_GROUPS * SSD_STATE

NSA_HEADS = 32
NSA_KV = 4
NSA_HPG = NSA_HEADS // NSA_KV
NSA_Q = NSA_HEADS * HEAD_DIM
KV_W = NSA_KV * HEAD_DIM
CMP_LEN = 32
CMP_STRIDE = 16
CMP_HIDDEN = HEAD_DIM
SEL_BLOCK = 64
SEL_TOPN = 16
NSA_QB = 32
WINDOW = 512
WIN_QB = 128

D_FF = 2 * D_MODEL
FFN_CONV = 3

EVEN_SPLIT = (MOBA_W, MOBA_W, MOBA_W, D_INNER, CONV_DIM, SSD_HEADS)
EVEN_IN = sum(EVEN_SPLIT)
EVEN_OUT = MOBA_W + D_INNER
ODD_SPLIT = (NSA_Q,) + (KV_W,) * 6 + (3 * NSA_HEADS,)
ODD_IN = sum(ODD_SPLIT)

kernel_name = 'moba_ssd_nsa_convffn_step'


def rmsnorm(x, g):
    xf = x.astype(jnp.float32)
    y = xf * lax.rsqrt(jnp.mean(xf * xf, axis=-1, keepdims=True) + EPS)
    return (y * g.astype(jnp.float32)).astype(x.dtype)


def masked_softmax(s, mask):
    s = jnp.where(mask, s.astype(jnp.float32), -jnp.inf)
    m = jnp.max(s, axis=-1, keepdims=True)
    p = jnp.exp(s - jnp.where(jnp.isfinite(m), m, 0.0))
    d = jnp.sum(p, axis=-1, keepdims=True)
    return p / jnp.where(d > 0, d, 1.0)


def _split(a, sizes):
    return jnp.split(a, np.cumsum(sizes)[:-1].tolist(), axis=-1)


def causal_dwconv(x, prev, w, b):
    width, t = w.shape[0], x.shape[1]
    xp = jnp.concatenate([prev.astype(x.dtype), x], axis=1)
    y = b + sum(xp[:, i:i + t] * w[i] for i in range(width))
    return y, xp[:, xp.shape[1] - (width - 1):]


def gqa_attend(q, k, v, mask):
    s = jnp.einsum('bqgrd,bkgd->bgrqk', q, k).astype(jnp.float32) * SCALE
    p = masked_softmax(s, mask)
    return jnp.einsum('bgrqk,bkgd->bqgrd', p.astype(v.dtype), v)


def moba_attention(q, k, v, q_pos):
    bsz, tk, nh, dh = k.shape
    tq = q.shape[1]
    nb = -(-tk // MOBA_BLOCK)
    pad = ((0, 0), (0, nb * MOBA_BLOCK - tk), (0, 0), (0, 0))
    kb = jnp.pad(k, pad).reshape(bsz, nb, MOBA_BLOCK, nh, dh).transpose(0, 3, 1, 2, 4)
    vb = jnp.pad(v, pad).reshape(bsz, nb, MOBA_BLOCK, nh, dh).transpose(0, 3, 1, 2, 4)
    kmean = jnp.mean(kb.astype(jnp.float32), axis=3)
    topk = min(MOBA_TOPK, nb)
    nsel = topk + 1
    qb = MOBA_QB if tq % MOBA_QB == 0 else tq
    nq = tq // qb
    qc = q.reshape(bsz, nq, qb, nh, dh).transpose(1, 0, 3, 2, 4)
    pc = q_pos.reshape(nq, qb)
    blocks = jnp.arange(nb)
    offs = jnp.arange(MOBA_BLOCK)
    gather = jax.vmap(jax.vmap(lambda t, i: t[i]))

    def step(args):
        qi, pi = args
        own = pi // MOBA_BLOCK
        gate = jnp.einsum('bhqd,bhnd->bhqn', qi.astype(jnp.float32), kmean)
        gate = jnp.where(blocks[None, :] < own[:, None], gate, -jnp.inf)
        top_val, top_idx = lax.top_k(gate, topk)
        idx = jnp.concatenate([top_idx, jnp.broadcast_to(own[:, None], (bsz, nh, qb, 1)).astype(top_idx.dtype)], axis=-1)
        ok = jnp.concatenate([top_val > -jnp.inf, jnp.ones((bsz, nh, qb, 1), bool)], axis=-1)
        kg = gather(kb, idx)
        vg = gather(vb, idx)
        s = jnp.einsum('bhqd,bhqnld->bhqnl', qi, kg).astype(jnp.float32) * SCALE
        kpos = idx[..., None] * MOBA_BLOCK + offs
        mask = ok[..., None] & (kpos <= pi[:, None, None])
        p = masked_softmax(s.reshape(bsz, nh, qb, nsel * MOBA_BLOCK), mask.reshape(bsz, nh, qb, nsel * MOBA_BLOCK))
        return jnp.einsum('bhqnl,bhqnld->bhqd', p.reshape(s.shape).astype(vg.dtype), vg)

    o = lax.map(step, (qc, pc))
    return o.transpose(1, 0, 3, 2, 4).reshape(bsz, tq, nh * dh)


def ssd_scan(x, dt, a, bm, cm, s0):
    bsz, t, nh, hp = x.shape
    ng, ns = bm.shape[2], bm.shape[3]
    nr = nh // ng
    cq_len = SSD_CHUNK if t % SSD_CHUNK == 0 else t
    nc = t // cq_len
    f32 = jnp.float32

    def chunks(arr, *tail):
        return arr.astype(f32).reshape((bsz, nc, cq_len) + tail).swapaxes(0, 1)

    xs, dts = chunks(x, ng, nr, hp), chunks(dt, ng, nr)
    bs, cs = chunks(bm, ng, ns), chunks(cm, ng, ns)
    causal = jnp.tril(jnp.ones((cq_len, cq_len), bool))[None, :, :, None, None]
    a_gr = a.reshape(ng, nr)

    def step(state, inp):
        xq, dq, bq, cq = inp
        cum = jnp.cumsum(dq * a_gr, axis=1)
        seg = cum[:, :, None] - cum[:, None, :]
        decay = jnp.where(causal, jnp.exp(jnp.where(causal, seg, 0.0)), 0.0)
        cb = jnp.einsum('bign,bjgn->bijg', cq, bq)
        w = cb[..., None] * decay * dq[:, None]
        y = jnp.einsum('bijgr,bjgrp->bigrp', w, xq)
        y = y + jnp.einsum('bign,bgrpn->bigrp', cq, state) * jnp.exp(cum)[..., None]
        tail = jnp.exp(cum[:, -1:] - cum) * dq
        state = state * jnp.exp(cum[:, -1])[..., None, None] + jnp.einsum('bjgr,bjgrp,bjgn->bgrpn', tail, xq, bq)
        return state, y

    s_fin, ys = lax.scan(step, s0.astype(f32).reshape(bsz, ng, nr, hp, ns), (xs, dts, bs, cs))
    return ys.swapaxes(0, 1).reshape(bsz, t, nh, hp), s_fin.reshape(bsz, nh, hp, ns)


def compress_rows(rows, pe, w1, b1, w2, b2):
    bsz, tk, ng, dh = rows.shape
    n_cmp = (tk - CMP_LEN) // CMP_STRIDE + 1
    idx = jnp.arange(n_cmp)[:, None] * CMP_STRIDE + jnp.arange(CMP_LEN)[None, :]
    blk = rows[:, idx] + pe[:, None, :]
    flat = blk.transpose(0, 1, 3, 2, 4).reshape(bsz, n_cmp, ng, CMP_LEN * dh)
    return jax.nn.gelu(flat @ w1 + b1) @ w2 + b2


def nsa_compressed(q, kc, vc, q_pos):
    n_cmp = kc.shape[1]
    s = jnp.einsum('bqgrd,bngd->bgrqn', q, kc).astype(jnp.float32) * SCALE
    end = jnp.arange(n_cmp) * CMP_STRIDE + CMP_LEN - 1
    p = masked_softmax(s, end[None, :] <= q_pos[:, None])
    o = jnp.einsum('bgrqn,bngd->bqgrd', p.astype(vc.dtype), vc)
    return o, jnp.sum(p, axis=2)


def nsa_selected(q, sk, sv, q_pos, imp):
    bsz, tk, ng, dh = sk.shape
    tq, nr = q.shape[1], q.shape[3]
    n_sel = -(-tk // SEL_BLOCK)
    n_cmp = imp.shape[-1]
    c_start = jnp.arange(n_cmp)[:, None] * CMP_STRIDE
    s_start = jnp.arange(n_sel)[None, :] * SEL_BLOCK
    overlap = ((c_start < s_start + SEL_BLOCK) & (c_start + CMP_LEN > s_start)).astype(jnp.float32)
    score = jnp.einsum('bgqn,ns->bgqs', imp, overlap)
    qblk = q_pos[:, None] // SEL_BLOCK
    j = jnp.arange(n_sel)[None, :]
    valid = j <= qblk
    forced = (j == 0) | (j == qblk) | (j == qblk - 1)
    score = jnp.where(valid, jnp.where(forced, jnp.inf, score), -jnp.inf)
    topn = min(SEL_TOPN, n_sel)
    val, idx = lax.top_k(score, topn)
    ok = val > -jnp.inf
    pad = ((0, 0), (0, n_sel * SEL_BLOCK - tk), (0, 0), (0, 0))
    kb = jnp.pad(sk, pad).reshape(bsz, n_sel, SEL_BLOCK, ng, dh).transpose(0, 3, 1, 2, 4)
    vb = jnp.pad(sv, pad).reshape(bsz, n_sel, SEL_BLOCK, ng, dh).transpose(0, 3, 1, 2, 4)
    qb = NSA_QB if tq % NSA_QB == 0 else tq
    nq = tq // qb
    qc = q.reshape(bsz, nq, qb, ng, nr, dh).transpose(1, 0, 3, 4, 2, 5)
    idc = idx.reshape(bsz, ng, nq, qb, topn).transpose(2, 0, 1, 3, 4)
    okc = ok.reshape(bsz, ng, nq, qb, topn).transpose(2, 0, 1, 3, 4)
    pc = q_pos.reshape(nq, qb)
    offs = jnp.arange(SEL_BLOCK)
    gather = jax.vmap(jax.vmap(lambda t, i: t[i]))

    def step(args):
        qi, ii, oi, pi = args
        kg = gather(kb, ii)
        vg = gather(vb, ii)
        s = jnp.einsum('bgrqd,bgqnld->bgrqnl', qi, kg).astype(jnp.float32) * SCALE
        kpos = ii[..., None] * SEL_BLOCK + offs
        mask = (oi[..., None] & (kpos <= pi[:, None, None]))[:, :, None]
        p = masked_softmax(s.reshape(bsz, ng, nr, qb, topn * SEL_BLOCK), mask.reshape(bsz, ng, 1, qb, topn * SEL_BLOCK))
        return jnp.einsum('bgrqnl,bgqnld->bgrqd', p.reshape(s.shape).astype(vg.dtype), vg)

    o = lax.map(step, (qc, idc, okc, pc))
    return o.transpose(1, 0, 4, 2, 3, 5).reshape(bsz, tq, ng, nr, dh)


def nsa_window(q, wk, wv, q_start, k_start):
    bsz, tq, ng, nr, dh = q.shape
    qb = WIN_QB if tq % WIN_QB == 0 else tq
    nq = tq // qb
    span = WINDOW + qb - 1
    kp = jnp.pad(wk, ((0, 0), (WINDOW, 0), (0, 0), (0, 0)))
    vp = jnp.pad(wv, ((0, 0), (WINDOW, 0), (0, 0), (0, 0)))
    qc = q.reshape(bsz, nq, qb, ng, nr, dh).swapaxes(0, 1)
    c0 = jnp.arange(nq, dtype=jnp.int32) * qb
    base = q_start - k_start + 1

    def step(args):
        qi, c = args
        start = base + c
        kg = lax.dynamic_slice_in_dim(kp, start, span, axis=1)
        vg = lax.dynamic_slice_in_dim(vp, start, span, axis=1)
        kidx = start - WINDOW + jnp.arange(span)
        kpos = k_start + kidx
        tpos = q_start + c + jnp.arange(qb)
        mask = (kidx[None, :] >= 0) & (kpos[None, :] <= tpos[:, None]) & (kpos[None, :] > tpos[:, None] - WINDOW)
        return gqa_attend(qi, kg, vg, mask)

    o = lax.map(step, (qc, c0))
    return o.swapaxes(0, 1).reshape(bsz, tq, ng, nr, dh)


def _even_layer(x, q_start, k_past, v_past, conv_prev, ssm_prev,
                g_norm, w_in, w_out, qk_g, conv_w, conv_b, dt_bias, a_log, d_skip, g_out):
    f32 = jnp.float32
    bsz, t, _ = x.shape
    h = rmsnorm(x, g_norm)
    q, k, v, z, xbc, dt = _split(h @ w_in, EVEN_SPLIT)
    heads = lambda a: a.reshape(bsz, t, MOBA_HEADS, HEAD_DIM)
    q = rmsnorm(heads(q), qk_g[0])
    k = rmsnorm(heads(k), qk_g[1])
    v = heads(v)
    q_pos = q_start + jnp.arange(t, dtype=jnp.int32)
    o_attn = moba_attention(q, jnp.concatenate([k_past, k], axis=1), jnp.concatenate([v_past, v], axis=1), q_pos)
    xbc, conv_state = causal_dwconv(xbc, conv_prev, conv_w, conv_b)
    xbc = jax.nn.silu(xbc)
    xs, bm, cm = _split(xbc, (D_INNER, SSD_GROUPS * SSD_STATE, SSD_GROUPS * SSD_STATE))
    xs = xs.reshape(bsz, t, SSD_HEADS, SSD_HEAD_DIM)
    dt = jax.nn.softplus(dt.astype(f32) + dt_bias.astype(f32))
    a = -jnp.exp(a_log.astype(f32))
    y, ssm_state = ssd_scan(xs, dt, a, bm.reshape(bsz, t, SSD_GROUPS, SSD_STATE), cm.reshape(bsz, t, SSD_GROUPS, SSD_STATE), ssm_prev)
    y = y + xs.astype(f32) * d_skip.astype(f32)[:, None]
    y = y.reshape(bsz, t, D_INNER) * jax.nn.silu(z.astype(f32))
    y = rmsnorm(y.reshape(bsz, t, SSD_GROUPS, D_INNER // SSD_GROUPS), g_out.reshape(SSD_GROUPS, D_INNER // SSD_GROUPS))
    y = y.reshape(bsz, t, D_INNER).astype(x.dtype)
    out = jnp.concatenate([o_attn.astype(x.dtype), y], axis=-1) @ w_out
    return x + out, k, v, conv_state, ssm_state.astype(x.dtype)


def _odd_layer(x, q_start, ck_past, cv_past, sk_past, sv_past, wk_buf, wv_buf,
               g_norm, w_in, w_out, qk_g, pe, w1, b1, w2, b2):
    bsz, t, _ = x.shape
    h = rmsnorm(x, g_norm)
    q, ck, cv, sk, sv, wk, wv, gt = _split(h @ w_in, ODD_SPLIT)
    kvh = lambda a: a.reshape(bsz, t, NSA_KV, HEAD_DIM)
    q = rmsnorm(q.reshape(bsz, t, NSA_KV, NSA_HPG, HEAD_DIM), qk_g[0])
    ck, cv, sv, wv = kvh(ck), kvh(cv), kvh(sv), kvh(wv)
    sk = rmsnorm(kvh(sk), qk_g[2])
    wk = rmsnorm(kvh(wk), qk_g[3])
    gates = jax.nn.sigmoid(gt.astype(jnp.float32)).reshape(bsz, t, NSA_KV, NSA_HPG, 3)
    q_pos = q_start + jnp.arange(t, dtype=jnp.int32)
    ck_all = jnp.concatenate([ck_past, ck], axis=1)
    cv_all = jnp.concatenate([cv_past, cv], axis=1)
    kc = rmsnorm(compress_rows(ck_all, pe[0], w1[0], b1[0], w2[0], b2[0]), qk_g[1])
    vc = compress_rows(cv_all, pe[1], w1[1], b1[1], w2[1], b2[1])
    o_cmp, imp = nsa_compressed(q, kc, vc, q_pos)
    o_sel = nsa_selected(q, jnp.concatenate([sk_past, sk], axis=1), jnp.concatenate([sv_past, sv], axis=1), q_pos, imp)
    wk_all = jnp.concatenate([wk_buf.astype(wk.dtype), wk], axis=1)
    wv_all = jnp.concatenate([wv_buf.astype(wv.dtype), wv], axis=1)
    o_win = nsa_window(q, wk_all, wv_all, q_start, q_start - wk_buf.shape[1])
    o = gates[..., 0:1] * o_cmp + gates[..., 1:2] * o_sel + gates[..., 2:3] * o_win
    out = o.reshape(bsz, t, NSA_Q).astype(x.dtype) @ w_out
    keep = min(WINDOW, wk_all.shape[1])
    return x + out, ck, cv, sk, sv, wk_all[:, wk_all.shape[1] - keep:], wv_all[:, wv_all.shape[1] - keep:]


def _conv_ffn(x, conv_prev, g_norm, w_up, conv_w, conv_b, w_down):
    u, state = causal_dwconv(rmsnorm(x, g_norm) @ w_up, conv_prev, conv_w, conv_b)
    a, g = jnp.split(u, 2, axis=-1)
    return x + (a * jax.nn.silu(g)) @ w_down, state


def setup_inputs(seed: int = 0) -> dict:
    key = jax.random.key(seed)
    keys = iter(jax.random.split(key, 64))
    f32 = jnp.float32

    def normal(shape, scale=1.0):
        return jax.random.normal(next(keys), shape, f32) * scale

    def gain(shape):
        return 1.0 + 0.02 * jax.random.normal(next(keys), shape, f32)

    n_pages = PAST_LEN // PAGE_SIZE
    n_used = DEC_BATCH * n_pages
    n_pool = n_used + max(1, n_used // 4)
    win_buf = min(WINDOW, PAST_LEN)
    paged_moba = (N_EVEN, n_pool, PAGE_SIZE, MOBA_HEADS, HEAD_DIM)
    paged_nsa = (N_ODD, n_pool, PAGE_SIZE, NSA_KV, HEAD_DIM)
    win = (N_ODD, DEC_BATCH, win_buf, NSA_KV, HEAD_DIM)
    inputs = {
        'x_prompt': normal((BATCH, SEQ, D_MODEL)),
        'x_sample': normal((DEC_BATCH, DEC_SEQ, D_MODEL)),
        'cache_moba_k': normal(paged_moba),
        'cache_moba_v': normal(paged_moba),
        'state_ssd': normal((N_EVEN, DEC_BATCH, SSD_HEADS, SSD_HEAD_DIM, SSD_STATE), 0.1),
        'state_ssd_conv': normal((N_EVEN, DEC_BATCH, SSD_CONV - 1, CONV_DIM)),
        'cache_nsa_cmp_k': normal(paged_nsa),
        'cache_nsa_cmp_v': normal(paged_nsa),
        'cache_nsa_sel_k': normal(paged_nsa),
        'cache_nsa_sel_v': normal(paged_nsa),
        'state_nsa_win_k': normal(win),
        'state_nsa_win_v': normal(win),
        'state_ffn_conv': normal((DEPTH, DEC_BATCH, FFN_CONV - 1, 2 * D_FF)),
        'page_table': jax.random.permutation(next(keys), n_pool)[:n_used].reshape(DEC_BATCH, n_pages).astype(jnp.int32),
    }
    dt0 = jnp.exp(jax.random.uniform(next(keys), (N_EVEN, SSD_HEADS), f32, math.log(1e-3), math.log(1e-1)))
    inputs.update({
        'norm_mix': gain((DEPTH, D_MODEL)),
        'norm_ffn': gain((DEPTH, D_MODEL)),
        'even_w_in': normal((N_EVEN, D_MODEL, EVEN_IN), D_MODEL ** -0.5),
        'even_w_out': normal((N_EVEN, EVEN_OUT, D_MODEL), EVEN_OUT ** -0.5),
        'moba_qk_norm': gain((N_EVEN, 2, HEAD_DIM)),
        'ssd_conv_w': normal((N_EVEN, SSD_CONV, CONV_DIM), SSD_CONV ** -0.5),
        'ssd_conv_b': normal((N_EVEN, CONV_DIM), 0.02),
        'ssd_dt_bias': dt0 + jnp.log(-jnp.expm1(-dt0)),
        'ssd_a_log': jnp.log(jax.random.uniform(next(keys), (N_EVEN, SSD_HEADS), f32, 1.0, 16.0)),
        'ssd_d': gain((N_EVEN, SSD_HEADS)),
        'ssd_norm': gain((N_EVEN, D_INNER)),
        'odd_w_in': normal((N_ODD, D_MODEL, ODD_IN), D_MODEL ** -0.5),
        'odd_w_out': normal((N_ODD, NSA_Q, D_MODEL), NSA_Q ** -0.5),
        'nsa_qk_norm': gain((N_ODD, 4, HEAD_DIM)),
        'cmp_pe': normal((N_ODD, 2, CMP_LEN, HEAD_DIM), 0.1),
        'cmp_w1': normal((N_ODD, 2, CMP_LEN * HEAD_DIM, CMP_HIDDEN), (CMP_LEN * HEAD_DIM) ** -0.5),
        'cmp_b1': normal((N_ODD, 2, CMP_HIDDEN), 0.02),
        'cmp_w2': normal((N_ODD, 2, CMP_HIDDEN, HEAD_DIM), CMP_HIDDEN ** -0.5),
        'cmp_b2': normal((N_ODD, 2, HEAD_DIM), 0.02),
        'ffn_w_up': normal((DEPTH, D_MODEL, 2 * D_FF), D_MODEL ** -0.5),
        'ffn_conv_w': normal((DEPTH, FFN_CONV, 2 * D_FF), FFN_CONV ** -0.5),
        'ffn_conv_b': normal((DEPTH, 2 * D_FF), 0.02),
        'ffn_w_down': normal((DEPTH, D_FF, D_MODEL), D_FF ** -0.5),
    })
    return inputs


def reference(x_prompt, x_sample, cache_moba_k, cache_moba_v, state_ssd, state_ssd_conv,
              cache_nsa_cmp_k, cache_nsa_cmp_v, cache_nsa_sel_k, cache_nsa_sel_v,
              state_nsa_win_k, state_nsa_win_v, state_ffn_conv, page_table,
              norm_mix, norm_ffn, even_w_in, even_w_out, moba_qk_norm,
              ssd_conv_w, ssd_conv_b, ssd_dt_bias, ssd_a_log, ssd_d, ssd_norm,
              odd_w_in, odd_w_out, nsa_qk_norm, cmp_pe, cmp_w1, cmp_b1, cmp_w2, cmp_b2,
              ffn_w_up, ffn_conv_w, ffn_conv_b, ffn_w_down):
    n_pages = PAST_LEN // PAGE_SIZE

    def paged(cache, layer):
        rows = cache[layer, page_table]
        return rows.reshape((page_table.shape[0], n_pages * PAGE_SIZE) + rows.shape[3:])

    state_keys = ('moba_k', 'moba_v', 'ssd', 'ssd_conv', 'nsa_cmp_k', 'nsa_cmp_v',
                  'nsa_sel_k', 'nsa_sel_v', 'nsa_win_k', 'nsa_win_v', 'ffn_conv')

    def trunk(x, q_start, sample):
        bsz, dtype = x.shape[0], x.dtype
        new = {name: [] for name in state_keys}
        for li in range(DEPTH):
            e = li // 2
            if li % 2 == 0:
                if sample:
                    kp, vp = paged(cache_moba_k, e), paged(cache_moba_v, e)
                    cp, sp = state_ssd_conv[e], state_ssd[e]
                else:
                    kp = jnp.zeros((bsz, 0, MOBA_HEADS, HEAD_DIM), dtype)
                    vp = kp
                    cp = jnp.zeros((bsz, SSD_CONV - 1, CONV_DIM), dtype)
                    sp = jnp.zeros((bsz, SSD_HEADS, SSD_HEAD_DIM, SSD_STATE), dtype)
                x, k, v, cs, ss = _even_layer(x, q_start, kp, vp, cp, sp, norm_mix[li], even_w_in[e], even_w_out[e],
                                              moba_qk_norm[e], ssd_conv_w[e], ssd_conv_b[e], ssd_dt_bias[e],
                                              ssd_a_log[e], ssd_d[e], ssd_norm[e])
                new['moba_k'].append(k)
                new['moba_v'].append(v)
                new['ssd_conv'].append(cs)
                new['ssd'].append(ss)
            else:
                if sample:
                    ckp, cvp = paged(cache_nsa_cmp_k, e), paged(cache_nsa_cmp_v, e)
                    skp, svp = paged(cache_nsa_sel_k, e), paged(cache_nsa_sel_v, e)
                    wkb, wvb = state_nsa_win_k[e], state_nsa_win_v[e]
                else:
                    z0 = jnp.zeros((bsz, 0, NSA_KV, HEAD_DIM), dtype)
                    ckp = cvp = skp = svp = wkb = wvb = z0
                x, ck, cv, sk, sv, wk, wv = _odd_layer(x, q_start, ckp, cvp, skp, svp, wkb, wvb, norm_mix[li],
                                                       odd_w_in[e], odd_w_out[e], nsa_qk_norm[e], cmp_pe[e],
                                                       cmp_w1[e], cmp_b1[e], cmp_w2[e], cmp_b2[e])
                new['nsa_cmp_k'].append(ck)
                new['nsa_cmp_v'].append(cv)
                new['nsa_sel_k'].append(sk)
                new['nsa_sel_v'].append(sv)
                new['nsa_win_k'].append(wk)
                new['nsa_win_v'].append(wv)
            fp = state_ffn_conv[li] if sample else jnp.zeros((bsz, FFN_CONV - 1, 2 * D_FF), dtype)
            x, fs = _conv_ffn(x, fp, norm_ffn[li], ffn_w_up[li], ffn_conv_w[li], ffn_conv_b[li], ffn_w_down[li])
            new['ffn_conv'].append(fs)
        return x, {name: jnp.stack(rows) for name, rows in new.items()}

    y_prompt, sp = trunk(x_prompt, 0, False)
    y_sample, ss = trunk(x_sample, PAST_LEN, True)
    return (y_prompt, y_sample,
            sp['moba_k'], ss['moba_k'], sp['moba_v'], ss['moba_v'],
            sp['ssd'], ss['ssd'], sp['ssd_conv'], ss['ssd_conv'],
            sp['nsa_cmp_k'], ss['nsa_cmp_k'], sp['nsa_cmp_v'], ss['nsa_cmp_v'],
            sp['nsa_sel_k'], ss['nsa_sel_k'], sp['nsa_sel_v'], ss['nsa_sel_v'],
            sp['nsa_win_k'], ss['nsa_win_k'], sp['nsa_win_v'], ss['nsa_win_v'],
            sp['ffn_conv'], ss['ffn_conv'])
```

```python
import functools

import numpy as np
import jax
import jax.numpy as jnp
from jax import lax
from jax.experimental import pallas as pl
from jax.experimental.pallas import tpu as pltpu

F32 = jnp.float32
BF16 = jnp.bfloat16

LANE = 128
VMEM_LIMIT_BYTES = 56 * 1024 * 1024

EPS = 1e-6
HEAD_DIM = 128
SCALE = HEAD_DIM ** -0.5
NEG = -1e30

PAGE = 128
MOBA_HEADS = 16
MOBA_BLOCK = 256
MOBA_TOPK = 3
SSD_HEADS = 64
SSD_HEAD_DIM = 64
SSD_GROUPS = 8
SSD_STATE = 128
SSD_CONV = 4
SSD_CHUNK = 128
D_INNER = 4096
GROUP_W = D_INNER // SSD_GROUPS
HEADS_PER_GROUP = SSD_HEADS // SSD_GROUPS
NSA_KV = 4
NSA_HPG = 8
KV_W = NSA_KV * HEAD_DIM
CMP_LEN = 32
CMP_STRIDE = 16
SEL_BLOCK = 64
SEL_TOPN = 16
WINDOW = 512
FFN_CONV = 3
CHUNK_W = CMP_STRIDE * KV_W


def _params(*sem):
    return pltpu.CompilerParams(dimension_semantics=sem, vmem_limit_bytes=VMEM_LIMIT_BYTES)


def _dot(a, b):
    return jnp.dot(a, b, preferred_element_type=F32)


def _dot_nt(a, b):
    return lax.dot_general(a, b, (((1,), (1,)), ((), ())), preferred_element_type=F32)


def _dot_tn(a, b):
    return lax.dot_general(a, b, (((0,), (0,)), ((), ())), preferred_element_type=F32)


def _split3(x):
    hi = x.astype(BF16)
    r1 = x - hi.astype(F32)
    mid = r1.astype(BF16)
    lo = (r1 - mid.astype(F32)).astype(BF16)
    return hi, mid, lo


def _sigmoid(x):
    return 1.0 / (1.0 + jnp.exp(-x))


def _silu(x):
    return x * _sigmoid(x)


def _softplus(x):
    return jnp.maximum(x, 0.0) + jnp.log1p(jnp.exp(-jnp.abs(x)))


def _gelu_tanh(x):
    return 0.5 * x * (1.0 + jnp.tanh(np.sqrt(2.0 / np.pi).astype(np.float32) * (x + 0.044715 * (x * x * x))))


def _rms(x, g):
    return x * lax.rsqrt(jnp.mean(x * x, axis=-1, keepdims=True) + EPS) * g


def _iota(shape, axis):
    return lax.broadcasted_iota(jnp.int32, shape, axis)


def _rank_before(score, n_cols):
    col = _iota(score.shape, 1)
    rank = jnp.zeros(score.shape, F32)
    for c in range(n_cols):
        sc = score[:, c:c + 1]
        rank = rank + jnp.where((sc > score) | ((sc == score) & (c < col)), 1.0, 0.0)
    return rank


def _pick_col(x, c):
    return jnp.sum(jnp.where(_iota(x.shape, 1) == c, x, 0.0), axis=1, keepdims=True)


def _diag_blocks(res, n, rows):
    return jnp.concatenate([res[i * rows:(i + 1) * rows, i * LANE:(i + 1) * LANE] for i in range(n)], axis=0)


def _online_update(carry, s, msk, v_bf16):
    m, l, acc = carry
    s = jnp.where(msk, s, NEG)
    m_new = jnp.maximum(m, jnp.max(s, axis=1, keepdims=True))
    p = jnp.where(msk, jnp.exp(s - m_new), 0.0)
    alpha = jnp.exp(m - m_new)
    l = alpha * l + jnp.sum(p, axis=1, keepdims=True)
    acc = alpha * acc + _dot(p.astype(BF16), v_bf16)
    return m_new, l, acc


def _flash_init(rows, dv):
    return (jnp.full((rows, 1), NEG, F32), jnp.zeros((rows, 1), F32), jnp.zeros((rows, dv), F32))


def _flash_out(carry):
    _, l, acc = carry
    return acc / jnp.where(l > 0.0, l, 1.0)


def _rmsnorm_kernel(x_ref, g_ref, o_ref):
    o_ref[...] = _rms(x_ref[...], g_ref[...]).astype(o_ref.dtype)


def rmsnorm_rows(x2d, g):
    m, d = x2d.shape
    tm = min(m, 256)
    return pl.pallas_call(
        _rmsnorm_kernel,
        grid=(m // tm,),
        in_specs=[pl.BlockSpec((tm, d), lambda i: (i, 0)), pl.BlockSpec((1, d), lambda i: (0, 0))],
        out_specs=pl.BlockSpec((tm, d), lambda i: (i, 0)),
        out_shape=jax.ShapeDtypeStruct((m, d), BF16),
        compiler_params=_params("parallel"),
        name="rmsnorm_rows",
    )(x2d, g.reshape(1, d))


def _matmul_kernel(*refs, nk, has_res):
    if has_res:
        a_ref, b_ref, r_ref, o_ref = refs
    else:
        a_ref, b_ref, o_ref = refs
    part = _dot(a_ref[...].astype(BF16), b_ref[...])
    if nk == 1:
        o_ref[...] = part + r_ref[...] if has_res else part
        return
    k = pl.program_id(2)

    @pl.when(k == 0)
    def _():
        o_ref[...] = part + r_ref[...] if has_res else part

    @pl.when(k > 0)
    def _():
        o_ref[...] += part


def matmul(a, b, res=None, tm=1024, tn=512, tk=4096):
    m, k = a.shape
    n = b.shape[1]
    tm, tn, tk = min(tm, m), min(tn, n), min(tk, k)
    assert m % tm == 0 and n % tn == 0 and k % tk == 0, (a.shape, b.shape)
    nk = k // tk
    in_specs = [pl.BlockSpec((tm, tk), lambda i, j, kk: (i, kk)), pl.BlockSpec((tk, tn), lambda i, j, kk: (kk, j))]
    args = [a, b]
    if res is not None:
        in_specs.append(pl.BlockSpec((tm, tn), lambda i, j, kk: (i, j)))
        args.append(res)
    return pl.pallas_call(
        functools.partial(_matmul_kernel, nk=nk, has_res=res is not None),
        grid=(m // tm, n // tn, nk),
        in_specs=in_specs,
        out_specs=pl.BlockSpec((tm, tn), lambda i, j, kk: (i, j)),
        out_shape=jax.ShapeDtypeStruct((m, n), F32),
        compiler_params=_params("parallel", "parallel", "arbitrary"),
        name="matmul",
    )(*args)


def _head_norm_kernel(x_ref, g_ref, o_ref, *, heads):
    g = g_ref[...]
    for h in range(heads):
        sl = slice(h * HEAD_DIM, (h + 1) * HEAD_DIM)
        o_ref[:, sl] = _rms(x_ref[:, sl], g).astype(o_ref.dtype)


def head_norm(x, col0, n_heads, g, out_dtype):
    b, t, _ = x.shape
    hb = min(n_heads, 4)
    w = hb * HEAD_DIM
    assert col0 % w == 0 and n_heads % hb == 0
    tq = min(t, 512)
    return pl.pallas_call(
        functools.partial(_head_norm_kernel, heads=hb),
        grid=(b, t // tq, n_heads // hb),
        in_specs=[pl.BlockSpec((None, tq, w), lambda bi, i, j: (bi, i, col0 // w + j)),
                  pl.BlockSpec((1, HEAD_DIM), lambda bi, i, j: (0, 0))],
        out_specs=pl.BlockSpec((None, tq, w), lambda bi, i, j: (bi, i, j)),
        out_shape=jax.ShapeDtypeStruct((b, t, n_heads * HEAD_DIM), out_dtype),
        compiler_params=_params("parallel", "parallel", "parallel"),
        name="head_norm",
    )(x, g.reshape(1, HEAD_DIM))


def _dwconv_kernel(*refs, width, t, glu):
    if glu:
        xa_ref, xg_ref, pa_ref, pg_ref, wa_ref, wg_ref, ba_ref, bg_ref, o_ref, sa_ref, sg_ref = refs
        pairs = ((xa_ref, pa_ref, sa_ref), (xg_ref, pg_ref, sg_ref))
    else:
        xa_ref, pa_ref, wa_ref, ba_ref, o_ref, sa_ref = refs
        pairs = ((xa_ref, pa_ref, sa_ref),)
    for x_ref, p_ref, s_ref in pairs:
        s_ref[0:8, :] = p_ref[...]
        s_ref[8:8 + t, :] = x_ref[...]
    tc = min(t, 256)

    def conv(s_ref, w_ref, b_ref, r0):
        acc = b_ref[...]
        for i in range(width):
            acc = acc + w_ref[i:i + 1, :] * s_ref[pl.ds(8 - (width - 1) + i + r0, tc), :]
        return acc

    for c in range(t // tc):
        r0 = c * tc
        a = conv(sa_ref, wa_ref, ba_ref, r0)
        if glu:
            o_ref[r0:r0 + tc, :] = (a * _silu(conv(sg_ref, wg_ref, bg_ref, r0))).astype(o_ref.dtype)
        else:
            o_ref[r0:r0 + tc, :] = _silu(a).astype(o_ref.dtype)


def dwconv(x, col0, c_out, prev8, w, bias, glu, out_dtype, tc=256):
    b, t, _ = x.shape
    width = w.shape[0]
    assert col0 % tc == 0 and c_out % tc == 0
    nj = c_out // tc
    x_spec = lambda off: pl.BlockSpec((None, t, tc), lambda bi, j: (bi, 0, off + j))
    p_spec = lambda off: pl.BlockSpec((None, 8, tc), lambda bi, j: (bi, 0, off + j))
    w_spec = lambda off: pl.BlockSpec((width, tc), lambda bi, j: (0, off + j))
    b_spec = lambda off: pl.BlockSpec((1, tc), lambda bi, j: (0, off + j))
    bias2 = bias.reshape(1, -1)
    if glu:
        in_specs = [x_spec(col0 // tc), x_spec(col0 // tc + nj), p_spec(0), p_spec(nj),
                    w_spec(0), w_spec(nj), b_spec(0), b_spec(nj)]
        args = (x, x, prev8, prev8, w, w, bias2, bias2)
        scratch = [pltpu.VMEM((8 + t, tc), F32), pltpu.VMEM((8 + t, tc), F32)]
    else:
        in_specs = [x_spec(col0 // tc), p_spec(0), w_spec(0), b_spec(0)]
        args = (x, prev8, w, bias2)
        scratch = [pltpu.VMEM((8 + t, tc), F32)]
    return pl.pallas_call(
        functools.partial(_dwconv_kernel, width=width, t=t, glu=glu),
        grid=(b, nj),
        in_specs=in_specs,
        out_specs=pl.BlockSpec((None, t, tc), lambda bi, j: (bi, 0, j)),
        out_shape=jax.ShapeDtypeStruct((b, t, c_out), out_dtype),
        scratch_shapes=scratch,
        compiler_params=_params("parallel", "parallel"),
        name="dwconv_glu" if glu else "dwconv_silu",
    )(*args)


def _history8(state, b, c):
    if state is None:
        return jnp.zeros((b, 8, c), F32)
    return jnp.concatenate([jnp.zeros((b, 8 - state.shape[1], c), F32), state], axis=1)


def _moba_prompt_kernel(q_ref, k_ref, v_ref, o_ref, kmean_ref, *, nb):
    blk = MOBA_BLOCK
    i = pl.program_id(2)

    @pl.when(i == 0)
    def _():
        kmean_ref[...] = jnp.zeros_like(kmean_ref)
        for n in range(nb):
            kmean_ref[n:n + 1, :] = jnp.mean(k_ref[n * blk:(n + 1) * blk, :], axis=0, keepdims=True)

    q = q_ref[...]
    gate = _dot_nt(q, kmean_ref[...].astype(BF16))
    col = _iota(gate.shape, 1)
    valid = col < i
    gate = jnp.where(valid, gate, -jnp.inf)
    sel = jnp.where(valid & (_rank_before(gate, nb) < min(MOBA_TOPK, nb)), 1.0, 0.0)

    def past_block(n, carry):
        rows = pl.ds(pl.multiple_of(n * blk, blk), blk)
        s = _dot_nt(q, k_ref[rows, :].astype(BF16)) * SCALE
        msk = jnp.broadcast_to(_pick_col(sel, n) > 0.5, s.shape)
        return _online_update(carry, s, msk, v_ref[rows, :].astype(BF16))

    carry = lax.fori_loop(0, i, past_block, _flash_init(blk, HEAD_DIM))
    rows = pl.ds(pl.multiple_of(i * blk, blk), blk)
    s = _dot_nt(q, k_ref[rows, :].astype(BF16)) * SCALE
    causal = _iota(s.shape, 1) <= _iota(s.shape, 0)
    carry = _online_update(carry, s, causal, v_ref[rows, :].astype(BF16))
    o_ref[...] = _flash_out(carry).astype(o_ref.dtype)


def moba_prompt(qn, kn, proj, v_col0):
    b, t, w = qn.shape
    nh = w // HEAD_DIM
    nb = t // MOBA_BLOCK
    assert t % MOBA_BLOCK == 0 and nb <= LANE
    return pl.pallas_call(
        functools.partial(_moba_prompt_kernel, nb=nb),
        grid=(b, nh, nb),
        in_specs=[pl.BlockSpec((None, MOBA_BLOCK, HEAD_DIM), lambda bi, h, i: (bi, i, h)),
                  pl.BlockSpec((None, t, HEAD_DIM), lambda bi, h, i: (bi, 0, h)),
                  pl.BlockSpec((None, t, HEAD_DIM), lambda bi, h, i: (bi, 0, v_col0 // HEAD_DIM + h))],
        out_specs=pl.BlockSpec((None, MOBA_BLOCK, HEAD_DIM), lambda bi, h, i: (bi, i, h)),
        out_shape=jax.ShapeDtypeStruct((b, t, w), BF16),
        scratch_shapes=[pltpu.VMEM((LANE, HEAD_DIM), F32)],
        compiler_params=_params("parallel", "parallel", "arbitrary"),
        name="moba_prompt",
    )(qn, kn, proj)


def _moba_dec_block_kernel(pt_ref, q_ref, ka_ref, kb_ref, va_ref, vb_ref,
                           acc_ref, g_ref, m_ref, l_ref, *, nh, tq):
    n = pl.program_id(1)
    q = q_ref[...]
    kb = jnp.concatenate([ka_ref[...], kb_ref[...]], axis=0).astype(BF16)
    vb = jnp.concatenate([va_ref[...], vb_ref[...]], axis=0).astype(BF16)
    s_raw = _dot_nt(q, kb)
    gate = jnp.mean(s_raw, axis=1, keepdims=True)
    s = s_raw * SCALE
    m = jnp.max(s, axis=1, keepdims=True)
    p = jnp.exp(s - m)
    l = jnp.sum(p, axis=1, keepdims=True)
    acc_ref[...] = _diag_blocks(_dot(p.astype(BF16), vb), nh, tq)

    @pl.when(n == 0)
    def _():
        g_ref[...] = jnp.zeros_like(g_ref)
        m_ref[...] = jnp.zeros_like(m_ref)
        l_ref[...] = jnp.zeros_like(l_ref)

    lane = _iota(g_ref.shape, 1)
    g_ref[...] = jnp.where(lane == n, gate, g_ref[...])
    m_ref[...] = jnp.where(lane == n, m, m_ref[...])
    l_ref[...] = jnp.where(lane == n, l, l_ref[...])


def _moba_dec_combine_kernel(g_ref, m_ref, l_ref, acc_ref, q_ref, kn_ref, vn_ref, o_ref, *, nbk, nh, tq):
    gate = g_ref[...]
    lane = _iota(gate.shape, 1)
    row = _iota(gate.shape, 0)
    valid = lane < nbk
    gate = jnp.where(valid, gate, -jnp.inf)
    sel = valid & (_rank_before(gate, nbk) < min(MOBA_TOPK, nbk + 1))
    q = q_ref[...]
    s_own = _dot_nt(q, kn_ref[...]) * SCALE
    t_row = row % tq
    own = (lane <= t_row) & (lane < tq)
    m_blk = m_ref[...]
    m_tot = jnp.maximum(jnp.max(jnp.where(sel, m_blk, NEG), axis=1, keepdims=True),
                        jnp.max(jnp.where(own, s_own, NEG), axis=1, keepdims=True))
    wgt = jnp.where(sel, jnp.exp(m_blk - m_tot), 0.0)
    p_own = jnp.where(own, jnp.exp(s_own - m_tot), 0.0)
    l_tot = jnp.sum(wgt * l_ref[...], axis=1, keepdims=True) + jnp.sum(p_own, axis=1, keepdims=True)
    acc = _diag_blocks(_dot(p_own.astype(BF16), vn_ref[...]), nh, tq)
    for n in range(nbk):
        acc = acc + wgt[:, n:n + 1] * acc_ref[n]
    o_ref[...] = acc / jnp.where(l_tot > 0.0, l_tot, 1.0)


def moba_sample(qn, kn_new, v_new, cache_k, cache_v, layer, page_table):
    b, tq, w = qn.shape
    nh = w // HEAD_DIM
    rows = nh * tq
    npages = page_table.shape[1]
    ppb = MOBA_BLOCK // PAGE
    nbk = npages // ppb
    assert rows == LANE and ppb == 2 and nbk <= LANE and tq <= LANE
    eye = jnp.eye(nh, dtype=F32)
    q4 = qn.astype(F32).reshape(b, tq, nh, HEAD_DIM)
    qrows = jnp.einsum("bthd,hg->bhtgd", q4, eye).reshape(b, rows, w).astype(BF16)
    pad = ((0, 0), (0, LANE - tq), (0, 0))
    kn_pad = jnp.pad(kn_new, pad).astype(BF16)
    vn_pad = jnp.pad(v_new, pad).astype(BF16)

    page = lambda j: pl.BlockSpec((None, None, PAGE, w), lambda bi, n, pt: (layer, pt[bi, ppb * n + j], 0, 0))
    stat = pl.BlockSpec((None, rows, LANE), lambda bi, n, pt: (bi, 0, 0))
    acc, g, m, l = pl.pallas_call(
        functools.partial(_moba_dec_block_kernel, nh=nh, tq=tq),
        grid_spec=pltpu.PrefetchScalarGridSpec(
            num_scalar_prefetch=1,
            grid=(b, nbk),
            in_specs=[pl.BlockSpec((None, rows, w), lambda bi, n, pt: (bi, 0, 0)),
                      page(0), page(1), page(0), page(1)],
            out_specs=[pl.BlockSpec((None, None, rows, HEAD_DIM), lambda bi, n, pt: (bi, n, 0, 0)), stat, stat, stat],
        ),
        out_shape=[jax.ShapeDtypeStruct((b, nbk, rows, HEAD_DIM), F32)] + [jax.ShapeDtypeStruct((b, rows, LANE), F32)] * 3,
        compiler_params=_params("parallel", "arbitrary"),
        name="moba_sample_blocks",
    )(page_table, qrows, cache_k, cache_k, cache_v, cache_v)

    per_b = lambda *shape: pl.BlockSpec((None,) + shape, lambda bi: (bi,) + (0,) * len(shape))
    o = pl.pallas_call(
        functools.partial(_moba_dec_combine_kernel, nbk=nbk, nh=nh, tq=tq),
        grid=(b,),
        in_specs=[per_b(rows, LANE), per_b(rows, LANE), per_b(rows, LANE), per_b(nbk, rows, HEAD_DIM),
                  per_b(rows, w), per_b(LANE, w), per_b(LANE, w)],
        out_specs=per_b(rows, HEAD_DIM),
        out_shape=jax.ShapeDtypeStruct((b, rows, HEAD_DIM), F32),
        compiler_params=_params("parallel"),
        name="moba_sample_combine",
    )(g, m, l, acc, qrows, kn_pad, vn_pad)
    return o.reshape(b, nh, tq, HEAD_DIM).transpose(0, 2, 1, 3).reshape(b, tq, w)


def _ssd_kernel(x_ref, bm_ref, cm_ref, z_ref, dtc_ref, dtr_ref, bias_c_ref, bias_r_ref, alog_c_ref, alog_r_ref,
                dskip_ref, gout_ref, s0_ref, y_ref, sfin_ref, state_ref, *, q, t_valid):
    c = pl.program_id(1)
    nc = pl.num_programs(1)

    @pl.when(c == 0)
    def _():
        state_ref[...] = s0_ref[...]

    dtc = _softplus(dtc_ref[...] + bias_c_ref[...])
    dtr = _softplus(dtr_ref[...] + bias_r_ref[...])
    if t_valid is not None:
        dtc = jnp.where(c * q + _iota(dtc.shape, 0) < t_valid, dtc, 0.0)
        dtr = jnp.where(c * q + _iota(dtr.shape, 1) < t_valid, dtr, 0.0)
    da_c = dtc * -jnp.exp(alog_c_ref[...])
    da_r = dtr * -jnp.exp(alog_r_ref[...])
    tri = _iota((q, q), 0) >= _iota((q, q), 1)
    tri_bf = jnp.where(tri, 1.0, 0.0).astype(BF16)
    cum_c = sum(_dot(tri_bf, part) for part in _split3(da_c))
    cum_r = sum(_dot_nt(part, tri_bf) for part in _split3(da_r))
    lane = _iota((q, LANE), 1)
    lo_half = lane < SSD_HEAD_DIM
    row_lo = _iota((LANE, 1), 0) < SSD_HEAD_DIM
    pairs = HEADS_PER_GROUP // 2

    for g in range(SSD_GROUPS):
        xg = x_ref[:, g * GROUP_W:(g + 1) * GROUP_W]
        bm = bm_ref[:, g * SSD_STATE:(g + 1) * SSD_STATE].astype(BF16)
        cm = cm_ref[:, g * SSD_STATE:(g + 1) * SSD_STATE].astype(BF16)
        cb = _dot_nt(cm, bm)
        y_parts = []
        for pr in range(pairs):
            x_pair = xg[:, pr * LANE:(pr + 1) * LANE]
            x_bf = x_pair.astype(BF16)
            ys, cols, lasts = [], [], []
            for h in (g * HEADS_PER_GROUP + 2 * pr, g * HEADS_PER_GROUP + 2 * pr + 1):
                col = cum_c[:, h:h + 1]
                seg = col - cum_r[h:h + 1, :]
                decay = jnp.where(tri, jnp.exp(jnp.where(tri, seg, 0.0)), 0.0)
                wgt = cb * decay * dtr[h:h + 1, :]
                ys.append(_dot(wgt.astype(BF16), x_bf))
                cols.append(col)
                lasts.append(cum_c[q - 1:q, h:h + 1])
            sidx = g * pairs + pr
            state = state_ref[sidx]
            y_pair = jnp.where(lo_half, ys[0], ys[1])
            carry_in = _dot_nt(cm, state.astype(BF16))
            y_pair = y_pair + carry_in * jnp.exp(jnp.where(lo_half, cols[0], cols[1]))
            y_parts.append(y_pair)
            tail = jnp.where(lo_half,
                             jnp.exp(lasts[0] - cols[0]) * dtc[:, 2 * sidx:2 * sidx + 1],
                             jnp.exp(lasts[1] - cols[1]) * dtc[:, 2 * sidx + 1:2 * sidx + 2])
            upd = _dot_tn((x_pair * tail).astype(BF16), bm)
            state_ref[sidx] = state * jnp.where(row_lo, jnp.exp(lasts[0]), jnp.exp(lasts[1])) + upd
        sl = slice(g * GROUP_W, (g + 1) * GROUP_W)
        yg = jnp.concatenate(y_parts, axis=1) + xg * dskip_ref[:, sl]
        yg = yg * _silu(z_ref[:, sl])
        y_ref[:, sl] = _rms(yg, gout_ref[:, sl]).astype(y_ref.dtype)

    @pl.when(c == nc - 1)
    def _():
        sfin_ref[...] = state_ref[...]


def ssd_mixer(conv, z_src, z_col0, dt_raw, s0, dt_bias, a_log, d_skip, g_out, t_valid):
    b, t, _ = conv.shape
    q = SSD_CHUNK
    assert t % q == 0
    nc = t // q
    dtr = dt_raw.transpose(0, 2, 1)
    pad_h = LANE - SSD_HEADS
    bias_c = jnp.pad(dt_bias, (0, pad_h)).reshape(1, LANE)
    bias_r = bias_c.reshape(LANE, 1)
    alog_c = jnp.pad(a_log, (0, pad_h)).reshape(1, LANE)
    alog_r = alog_c.reshape(LANE, 1)
    dskip = jnp.repeat(d_skip, SSD_HEAD_DIM).reshape(1, D_INNER)
    npair = SSD_HEADS // 2
    s0p = s0.reshape(b, npair, 2 * SSD_HEAD_DIM, SSD_STATE)
    nbc = D_INNER // (SSD_GROUPS * SSD_STATE)
    const = lambda shape: pl.BlockSpec(shape, lambda bi, c: (0,) * len(shape))
    y, sfin = pl.pallas_call(
        functools.partial(_ssd_kernel, q=q, t_valid=t_valid),
        grid=(b, nc),
        in_specs=[pl.BlockSpec((None, q, D_INNER), lambda bi, c: (bi, c, 0)),
                  pl.BlockSpec((None, q, SSD_GROUPS * SSD_STATE), lambda bi, c: (bi, c, nbc)),
                  pl.BlockSpec((None, q, SSD_GROUPS * SSD_STATE), lambda bi, c: (bi, c, nbc + 1)),
                  pl.BlockSpec((None, q, D_INNER), lambda bi, c: (bi, c, z_col0 // D_INNER)),
                  pl.BlockSpec((None, q, LANE), lambda bi, c: (bi, c, 0)),
                  pl.BlockSpec((None, LANE, q), lambda bi, c: (bi, 0, c)),
                  const((1, LANE)), const((LANE, 1)), const((1, LANE)), const((LANE, 1)),
                  const((1, D_INNER)), const((1, D_INNER)),
                  pl.BlockSpec((None, npair, LANE, SSD_STATE), lambda bi, c: (bi, 0, 0, 0))],
        out_specs=[pl.BlockSpec((None, q, D_INNER), lambda bi, c: (bi, c, 0)),
                   pl.BlockSpec((None, npair, LANE, SSD_STATE), lambda bi, c: (bi, 0, 0, 0))],
        out_shape=[jax.ShapeDtypeStruct((b, t, D_INNER), BF16),
                   jax.ShapeDtypeStruct((b, npair, LANE, SSD_STATE), F32)],
        scratch_shapes=[pltpu.VMEM((npair, LANE, SSD_STATE), F32)],
        compiler_params=_params("parallel", "arbitrary"),
        name="ssd_scan",
    )(conv, conv, conv, z_src, dt_raw, dtr, bias_c, bias_r, alog_c, alog_r, dskip, g_out.reshape(1, D_INNER), s0p)
    return y, sfin.reshape(b, SSD_HEADS, SSD_HEAD_DIM, SSD_STATE)


def _compress_kernel(pt_ref, a_ref, b_ref, cpe_a_ref, cpe_b_ref, b1_ref, w2_ref, b2_ref, g_ref, o_ref,
                     ha_ref, hb_ref, *, npg, norm):
    bi = pl.program_id(0)
    ncp = npg * 8

    def gather(p, _):
        src = pl.ds(pl.multiple_of(pt_ref[bi, p] * 8, 8), 8)
        dst = pl.ds(pl.multiple_of(p * 8, 8), 8)
        ha_ref[dst, :] = a_ref[src, :]
        hb_ref[dst, :] = b_ref[src, :]
        return 0

    lax.fori_loop(0, npg, gather, 0)
    hb_ref[ncp:ncp + 8, :] = jnp.zeros((8, HEAD_DIM), F32)
    hid = ha_ref[...] + hb_ref[pl.ds(1, ncp), :] + (cpe_a_ref[0:1, :] + cpe_b_ref[1:2, :] + b1_ref[...])
    out = _dot(_gelu_tanh(hid).astype(BF16), w2_ref[...]) + b2_ref[...]
    if norm:
        out = _rms(out, g_ref[...])
    o_ref[...] = out


def compress_tokens(rows_flat, page_table, pe, w1, b1, w2, b2, g_norm):
    nck = rows_flat.shape[0]
    b, npg = page_table.shape
    ncp = npg * 8
    half = CMP_STRIDE * HEAD_DIM
    eye = jnp.eye(NSA_KV, dtype=F32)
    blockdiag = lambda wh: jnp.einsum("ldj,gh->lgdhj", wh.reshape(CMP_STRIDE, HEAD_DIM, HEAD_DIM), eye).reshape(CHUNK_W, KV_W)
    w_big = jnp.concatenate([blockdiag(w1[:half]), blockdiag(w1[half:])], axis=1).astype(BF16)
    ab = matmul(rows_flat, w_big, tm=256, tk=2048)
    pe_rows = jnp.broadcast_to(pe.reshape(2, CMP_STRIDE, 1, HEAD_DIM), (2, CMP_STRIDE, NSA_KV, HEAD_DIM)).reshape(2, CHUNK_W)
    cpe = matmul(jnp.pad(pe_rows, ((0, 6), (0, 0))), w_big, tk=2048)
    col = lambda off: (lambda bi, g, pt: (0, off + g))
    return pl.pallas_call(
        functools.partial(_compress_kernel, npg=npg, norm=g_norm is not None),
        grid_spec=pltpu.PrefetchScalarGridSpec(
            num_scalar_prefetch=1,
            grid=(b, NSA_KV),
            in_specs=[pl.BlockSpec((nck, HEAD_DIM), col(0)), pl.BlockSpec((nck, HEAD_DIM), col(NSA_KV)),
                      pl.BlockSpec((8, HEAD_DIM), col(0)), pl.BlockSpec((8, HEAD_DIM), col(NSA_KV)),
                      pl.BlockSpec((1, HEAD_DIM), lambda bi, g, pt: (0, 0)),
                      pl.BlockSpec((HEAD_DIM, HEAD_DIM), lambda bi, g, pt: (0, 0)),
                      pl.BlockSpec((1, HEAD_DIM), lambda bi, g, pt: (0, 0)),
                      pl.BlockSpec((1, HEAD_DIM), lambda bi, g, pt: (0, 0))],
            out_specs=pl.BlockSpec((None, ncp, HEAD_DIM), lambda bi, g, pt: (bi, 0, g)),
            scratch_shapes=[pltpu.VMEM((ncp, HEAD_DIM), F32), pltpu.VMEM((ncp + 8, HEAD_DIM), F32)],
        ),
        out_shape=jax.ShapeDtypeStruct((b, ncp, KV_W), F32),
        compiler_params=_params("parallel", "parallel"),
        name="nsa_compress",
    )(page_table, ab, ab, cpe, cpe, b1.reshape(1, -1), w2.astype(BF16), b2.reshape(1, -1),
      (g_norm if g_norm is not None else jnp.ones((HEAD_DIM,), F32)).reshape(1, -1))


def _nsa_cmp_kernel(q_ref, kc_ref, vc_ref, o_ref, sel_ref, *, tq, q_start, n_sel, nsp):
    i = pl.program_id(2)
    kc = kc_ref[...].astype(BF16)
    vc = vc_ref[...].astype(BF16)
    ncp = kc.shape[0]
    qpos = q_start + i * tq + _iota((tq, 1), 0)
    end = _iota((tq, ncp), 1) * CMP_STRIDE + (CMP_LEN - 1)
    msk = end <= qpos
    imp = jnp.zeros((tq, ncp), F32)
    for r in range(NSA_HPG):
        sl = slice(r * HEAD_DIM, (r + 1) * HEAD_DIM)
        s = jnp.where(msk, _dot_nt(q_ref[:, sl], kc) * SCALE, NEG)
        p = jnp.where(msk, jnp.exp(s - jnp.max(s, axis=1, keepdims=True)), 0.0)
        d = jnp.sum(p, axis=1, keepdims=True)
        p = p / jnp.where(d > 0.0, d, 1.0)
        o_ref[:, sl] = _dot(p.astype(BF16), vc)
        imp = imp + p
    cn = _iota((ncp, nsp), 0)
    sj = _iota((ncp, nsp), 1)
    overlap = jnp.where((cn >= 4 * sj - 1) & (cn <= 4 * sj + 3), 1.0, 0.0).astype(BF16)
    score = sum(_dot(part, overlap) for part in _split3(imp))
    j = _iota((tq, nsp), 1)
    qblk = qpos // SEL_BLOCK
    valid = j <= qblk
    forced = (j == 0) | (j == qblk) | (j == qblk - 1)
    score = jnp.where(valid, jnp.where(forced, jnp.inf, score), -jnp.inf)
    sel_ref[...] = jnp.where(valid & (_rank_before(score, n_sel) < min(SEL_TOPN, n_sel)), 1.0, 0.0)


def nsa_compressed(qn, kc, vc, q_start, n_sel):
    b, t, w = qn.shape
    ncp = kc.shape[1]
    gw = NSA_HPG * HEAD_DIM
    tq = min(t, 256)
    nsp = -(-n_sel // LANE) * LANE
    return pl.pallas_call(
        functools.partial(_nsa_cmp_kernel, tq=tq, q_start=q_start, n_sel=n_sel, nsp=nsp),
        grid=(b, NSA_KV, t // tq),
        in_specs=[pl.BlockSpec((None, tq, gw), lambda bi, g, i: (bi, i, g)),
                  pl.BlockSpec((None, ncp, HEAD_DIM), lambda bi, g, i: (bi, 0, g)),
                  pl.BlockSpec((None, ncp, HEAD_DIM), lambda bi, g, i: (bi, 0, g))],
        out_specs=[pl.BlockSpec((None, tq, gw), lambda bi, g, i: (bi, i, g)),
                   pl.BlockSpec((None, None, tq, nsp), lambda bi, g, i: (bi, g, i, 0))],
        out_shape=[jax.ShapeDtypeStruct((b, t, w), F32), jax.ShapeDtypeStruct((b, NSA_KV, t, nsp), F32)],
        compiler_params=_params("parallel", "parallel", "parallel"),
        name="nsa_compressed",
    )(qn, kc, vc)


def _nsa_prompt_kernel(q_ref, sk_ref, sv_ref, wk_ref, wv_ref, sel_ref, ocmp_ref, gl_ref, o_ref, *, tq, tk):
    i = pl.program_id(2)
    q0 = i * tq
    rows = NSA_HPG * tq
    qs = jnp.concatenate([q_ref[:, r * HEAD_DIM:(r + 1) * HEAD_DIM] for r in range(NSA_HPG)], axis=0)
    tpos = q0 + _iota((tq, tk), 0)
    sel = sel_ref[...].astype(BF16)
    nsp = sel.shape[1]

    def step(k_ref, v_ref, mask_fn):
        def body(n, carry):
            k0 = pl.multiple_of(n * tk, tk)
            s = _dot_nt(qs, k_ref[pl.ds(k0, tk), :].astype(BF16)) * SCALE
            kpos = k0 + _iota((tq, tk), 1)
            msk = mask_fn(kpos, k0)
            msk = jnp.concatenate([msk] * NSA_HPG, axis=0)
            return _online_update(carry, s, msk, v_ref[pl.ds(k0, tk), :].astype(BF16))
        return body

    def sel_mask(kpos, k0):
        expand = jnp.where(_iota((nsp, tk), 0) == (k0 + _iota((nsp, tk), 1)) // SEL_BLOCK, 1.0, 0.0).astype(BF16)
        return (_dot(sel, expand) > 0.5) & (kpos <= tpos)

    def win_mask(kpos, k0):
        return (kpos <= tpos) & (kpos > tpos - WINDOW)

    n_hi = (q0 + tq - 1) // tk + 1
    o_sel = _flash_out(lax.fori_loop(0, n_hi, step(sk_ref, sv_ref, sel_mask), _flash_init(rows, HEAD_DIM)))
    n_lo = jnp.maximum(q0 - (WINDOW - 1), 0) // tk
    o_win = _flash_out(lax.fori_loop(n_lo, n_hi, step(wk_ref, wv_ref, win_mask), _flash_init(rows, HEAD_DIM)))
    gates = [_sigmoid(gl_ref[c]) for c in range(3)]
    for r in range(NSA_HPG):
        sl = slice(r * HEAD_DIM, (r + 1) * HEAD_DIM)
        rs = slice(r * tq, (r + 1) * tq)
        o = (gates[0][:, r:r + 1] * ocmp_ref[:, sl] + gates[1][:, r:r + 1] * o_sel[rs]
             + gates[2][:, r:r + 1] * o_win[rs])
        o_ref[:, sl] = o.astype(o_ref.dtype)


def nsa_prompt(qn, skn, wkn, proj, sv_col0, wv_col0, selm, ocmp, gate_logits):
    b, t, w = qn.shape
    gw = NSA_HPG * HEAD_DIM
    tq = min(t, 128)
    tk = min(t, 256)
    nsp = selm.shape[-1]
    kv = lambda off: pl.BlockSpec((None, t, HEAD_DIM), lambda bi, g, i: (bi, 0, off + g))
    return pl.pallas_call(
        functools.partial(_nsa_prompt_kernel, tq=tq, tk=tk),
        grid=(b, NSA_KV, t // tq),
        in_specs=[pl.BlockSpec((None, tq, gw), lambda bi, g, i: (bi, i, g)),
                  kv(0), kv(sv_col0 // HEAD_DIM), kv(0), kv(wv_col0 // HEAD_DIM),
                  pl.BlockSpec((None, None, tq, nsp), lambda bi, g, i: (bi, g, i, 0)),
                  pl.BlockSpec((None, tq, gw), lambda bi, g, i: (bi, i, g)),
                  pl.BlockSpec((3, None, None, tq, NSA_HPG), lambda bi, g, i: (0, bi, g, i, 0))],
        out_specs=pl.BlockSpec((None, tq, gw), lambda bi, g, i: (bi, i, g)),
        out_shape=jax.ShapeDtypeStruct((b, t, w), BF16),
        compiler_params=_params("parallel", "parallel", "arbitrary"),
        name="nsa_prompt",
    )(qn, skn, proj, wkn, proj, selm, ocmp, gate_logits)


def _nsa_dec_kernel(pt_ref, q_ref, sel_ref, ocmp_ref, gl_ref, kp_ref, vp_ref, skn_ref, svn_ref,
                    wkb_ref, wvb_ref, wkn_ref, wvn_ref, o_ref, m_ref, l_ref, acc_ref, *, tq, past_len):
    p = pl.program_id(1)
    npages = pl.num_programs(1)
    rows = NSA_KV * NSA_HPG * tq
    per_g = NSA_HPG * tq
    q = q_ref[...]
    sel = sel_ref[...]
    lane = _iota((rows, PAGE), 1)
    t_row = _iota((rows, PAGE), 0) % tq

    @pl.when(p == 0)
    def _():
        m_ref[...] = jnp.full(m_ref.shape, NEG, F32)
        l_ref[...] = jnp.zeros_like(l_ref)
        acc_ref[...] = jnp.zeros_like(acc_ref)

    def update(carry, k_rows, v_rows, msk):
        s = _dot_nt(q, k_rows.astype(BF16)) * SCALE
        m, l, acc = carry
        s = jnp.where(msk, s, NEG)
        m_new = jnp.maximum(m, jnp.max(s, axis=1, keepdims=True))
        pr = jnp.where(msk, jnp.exp(s - m_new), 0.0)
        alpha = jnp.exp(m - m_new)
        l = alpha * l + jnp.sum(pr, axis=1, keepdims=True)
        acc = alpha * acc + _diag_blocks(_dot(pr.astype(BF16), v_rows.astype(BF16)), NSA_KV, per_g)
        return m_new, l, acc

    blocks_per_page = PAGE // SEL_BLOCK
    picked = jnp.where(lane < SEL_BLOCK, _pick_col(sel, blocks_per_page * p), _pick_col(sel, blocks_per_page * p + 1))
    carry = update((m_ref[...], l_ref[...], acc_ref[...]), kp_ref[...], vp_ref[...], picked > 0.5)
    m_ref[...], l_ref[...], acc_ref[...] = carry

    @pl.when(p == npages - 1)
    def _():
        own = (lane <= t_row) & (lane < tq)
        new_blk = _pick_col(sel, past_len // SEL_BLOCK) > 0.5
        o_sel = _flash_out(update(carry, skn_ref[...], svn_ref[...], own & new_blk))
        wlane = _iota((rows, WINDOW), 1)
        w_carry = update(_flash_init(rows, HEAD_DIM), wkb_ref[...], wvb_ref[...], wlane > _iota((rows, WINDOW), 0) % tq)
        o_win = _flash_out(update(w_carry, wkn_ref[...], wvn_ref[...], own))
        gates = _sigmoid(gl_ref[...])
        o_ref[...] = gates[:, 0:1] * ocmp_ref[...] + gates[:, 1:2] * o_sel + gates[:, 2:3] * o_win


def nsa_sample(qn, skn_new, sv_new, wkn_new, wv_new, cache_sk, cache_sv, win_k, win_v, layer,
               page_table, selm, ocmp, gate_logits):
    b, tq, w = qn.shape
    rows = NSA_KV * NSA_HPG * tq
    npages = page_table.shape[1]
    past_len = npages * PAGE
    nsp = selm.shape[-1]
    assert win_k.shape[2] == WINDOW and past_len % SEL_BLOCK == 0
    eye = jnp.eye(NSA_KV, dtype=F32)
    q5 = qn.astype(F32).reshape(b, tq, NSA_KV, NSA_HPG, HEAD_DIM)
    qrows = jnp.einsum("btgrd,gh->bgrthd", q5, eye).reshape(b, rows, KV_W).astype(BF16)
    selrows = jnp.broadcast_to(selm[:, :, None], (b, NSA_KV, NSA_HPG, tq, nsp)).reshape(b, rows, nsp)
    to_rows = lambda a: a.reshape(b, tq, NSA_KV, NSA_HPG, -1).transpose(0, 2, 3, 1, 4).reshape(b, rows, -1)
    pad = lambda a: jnp.pad(a, ((0, 0), (0, PAGE - tq), (0, 0)))
    per_b = lambda *shape: pl.BlockSpec((None,) + shape, lambda bi, p, pt: (bi,) + (0,) * len(shape))
    page = pl.BlockSpec((None, None, PAGE, KV_W), lambda bi, p, pt: (layer, pt[bi, p], 0, 0))
    wbuf = pl.BlockSpec((None, None, WINDOW, KV_W), lambda bi, p, pt: (layer, bi, 0, 0))
    o = pl.pallas_call(
        functools.partial(_nsa_dec_kernel, tq=tq, past_len=past_len),
        grid_spec=pltpu.PrefetchScalarGridSpec(
            num_scalar_prefetch=1,
            grid=(b, npages),
            in_specs=[per_b(rows, KV_W), per_b(rows, nsp), per_b(rows, HEAD_DIM), per_b(rows, 3),
                      page, page, per_b(PAGE, KV_W), per_b(PAGE, KV_W), wbuf, wbuf, per_b(PAGE, KV_W), per_b(PAGE, KV_W)],
            out_specs=per_b(rows, HEAD_DIM),
            scratch_shapes=[pltpu.VMEM((rows, 1), F32), pltpu.VMEM((rows, 1), F32), pltpu.VMEM((rows, HEAD_DIM), F32)],
        ),
        out_shape=jax.ShapeDtypeStruct((b, rows, HEAD_DIM), F32),
        compiler_params=_params("parallel", "arbitrary"),
        name="nsa_sample",
    )(page_table, qrows, selrows, to_rows(ocmp), to_rows(gate_logits), cache_sk, cache_sv,
      pad(skn_new), pad(sv_new), win_k, win_v, pad(wkn_new), pad(wv_new))
    return o.reshape(b, NSA_KV, NSA_HPG, tq, HEAD_DIM).transpose(0, 3, 1, 2, 4).reshape(b, tq, w)


MOBA_W = MOBA_HEADS * HEAD_DIM
CONV_DIM = D_INNER + 2 * SSD_GROUPS * SSD_STATE
EVEN_MAIN = 3 * MOBA_W + D_INNER + CONV_DIM
Z_COL0 = 0
XBC_COL0 = D_INNER
Q_COL0 = XBC_COL0 + CONV_DIM
K_COL0 = Q_COL0 + MOBA_W
V_COL0 = K_COL0 + MOBA_W
NSA_Q = NSA_KV * NSA_HPG * HEAD_DIM
ODD_MAIN = NSA_Q + 6 * KV_W
D_FF = 8192


def _pad_cols(w, n):
    return jnp.pad(w, ((0, 0), (0, n - w.shape[1])))


def _proj(h2d, w, main, reorder=None):
    wb = w.astype(BF16)
    w_main = wb[:, :main]
    if reorder is not None:
        w_main = jnp.concatenate([w_main[:, reorder:], w_main[:, :reorder]], axis=1)
    return matmul(h2d, w_main), matmul(h2d, _pad_cols(wb[:, main:], LANE))


def _even_layer(x, sample, q_start, cache_k, cache_v, e, page_table, conv_prev, ssm_prev,
                g_norm, w_in, w_out, qk_g, conv_w, conv_b, dt_bias, a_log, d_skip, g_out):
    b, t, d = x.shape
    x2 = x.reshape(b * t, d)
    h = rmsnorm_rows(x2, g_norm)
    proj, dt_raw = _proj(h, w_in, EVEN_MAIN, reorder=3 * MOBA_W)
    proj = proj.reshape(b, t, EVEN_MAIN)
    dt_raw = dt_raw.reshape(b, t, LANE)
    qn = head_norm(proj, Q_COL0, MOBA_HEADS, qk_g[0], BF16)
    kn = head_norm(proj, K_COL0, MOBA_HEADS, qk_g[1], F32)
    v = proj[..., V_COL0:]
    if sample:
        o_attn = moba_sample(qn, kn, v, cache_k, cache_v, e, page_table).astype(BF16)
    else:
        o_attn = moba_prompt(qn, kn, proj, V_COL0)
    conv = dwconv(proj, XBC_COL0, CONV_DIM, _history8(conv_prev, b, CONV_DIM), conv_w, conv_b, False, F32)
    xbc = proj[..., XBC_COL0:Q_COL0]
    conv_state = jnp.concatenate([_history8(conv_prev, b, CONV_DIM), xbc], axis=1)[:, -(SSD_CONV - 1):]
    if t % SSD_CHUNK:
        tp = -(-t // SSD_CHUNK) * SSD_CHUNK
        padt = lambda a: jnp.pad(a, ((0, 0), (0, tp - t), (0, 0)))
        y, ssm = ssd_mixer(padt(conv), padt(proj[..., Z_COL0:XBC_COL0]), 0, padt(dt_raw), ssm_prev,
                           dt_bias, a_log, d_skip, g_out, t)
        y = y[:, :t]
    else:
        y, ssm = ssd_mixer(conv, proj, Z_COL0, dt_raw, ssm_prev, dt_bias, a_log, d_skip, g_out, None)
    wo = w_out.astype(BF16)
    x2 = matmul(o_attn.reshape(b * t, MOBA_W), wo[:MOBA_W], res=x2, tk=MOBA_W)
    x2 = matmul(y.reshape(b * t, D_INNER), wo[MOBA_W:], res=x2)
    heads = lambda a: a.reshape(b, t, MOBA_HEADS, HEAD_DIM)
    return x2.reshape(b, t, d), heads(kn), heads(v), conv_state, ssm


def _odd_layer(x, sample, q_start, caches, e, page_table, win_k, win_v,
               g_norm, w_in, w_out, qk_g, pe, w1, b1, w2, b2):
    b, t, d = x.shape
    x2 = x.reshape(b * t, d)
    h = rmsnorm_rows(x2, g_norm)
    proj, gl = _proj(h, w_in, ODD_MAIN)
    proj = proj.reshape(b, t, ODD_MAIN)
    gl = gl[:, :3 * NSA_KV * NSA_HPG].reshape(b, t, NSA_KV, NSA_HPG, 3)
    col = lambda i: NSA_Q + i * KV_W
    qn = head_norm(proj, 0, NSA_KV * NSA_HPG, qk_g[0], BF16)
    skn = head_norm(proj, col(2), NSA_KV, qk_g[2], F32)
    wkn = head_norm(proj, col(4), NSA_KV, qk_g[3], F32)
    ck, cv, sv, wv = (proj[..., col(i):col(i + 1)] for i in (0, 1, 3, 5))
    if sample:
        cache_ck, cache_cv, cache_sk, cache_sv = caches
        n_pool = cache_ck.shape[1]
        flat = lambda c: c[e].reshape(n_pool * (PAGE // CMP_STRIDE), CHUNK_W)
        rows_k, rows_v, pt = flat(cache_ck), flat(cache_cv), page_table
        n_rows = page_table.shape[1] * PAGE + t
    else:
        flat = lambda a: a.reshape(b * t // CMP_STRIDE, CHUNK_W)
        rows_k, rows_v = flat(ck), flat(cv)
        npg = t // PAGE
        pt = (jnp.arange(b, dtype=jnp.int32)[:, None] * npg + jnp.arange(npg, dtype=jnp.int32)[None, :])
        n_rows = t
    kc = compress_tokens(rows_k, pt, pe[0], w1[0], b1[0], w2[0], b2[0], qk_g[1])
    vc = compress_tokens(rows_v, pt, pe[1], w1[1], b1[1], w2[1], b2[1], None)
    n_sel = -(-n_rows // SEL_BLOCK)
    ocmp, selm = nsa_compressed(qn, kc, vc, q_start, n_sel)
    if sample:
        cache = lambda c: c.reshape(c.shape[0], c.shape[1], PAGE, KV_W)
        wbuf = lambda s: s.reshape(s.shape[0], s.shape[1], s.shape[2], KV_W)
        o = nsa_sample(qn, skn, sv, wkn, wv, cache(cache_sk), cache(cache_sv), wbuf(win_k), wbuf(win_v), e,
                       page_table, selm, ocmp, gl).astype(BF16)
        wk_all = jnp.concatenate([wbuf(win_k)[e], wkn], axis=1)
        wv_all = jnp.concatenate([wbuf(win_v)[e], wv], axis=1)
    else:
        o = nsa_prompt(qn, skn, wkn, proj, col(3), col(5), selm, ocmp, gl.transpose(4, 0, 2, 1, 3))
        wk_all, wv_all = wkn, wv
    x2 = matmul(o.reshape(b * t, NSA_Q), w_out.astype(BF16), res=x2)
    keep = min(WINDOW, wk_all.shape[1])
    kvh = lambda a: a.reshape(b, a.shape[1], NSA_KV, HEAD_DIM)
    return (x2.reshape(b, t, d), kvh(ck), kvh(cv), kvh(skn), kvh(sv),
            kvh(wk_all[:, wk_all.shape[1] - keep:]), kvh(wv_all[:, wv_all.shape[1] - keep:]))


def _conv_ffn(x, conv_prev, g_norm, w_up, conv_w, conv_b, w_down):
    b, t, d = x.shape
    x2 = x.reshape(b * t, d)
    u = matmul(rmsnorm_rows(x2, g_norm), w_up.astype(BF16)).reshape(b, t, 2 * D_FF)
    hist = _history8(conv_prev, b, 2 * D_FF)
    act = dwconv(u, 0, D_FF, hist, conv_w, conv_b, True, BF16)
    state = jnp.concatenate([hist, u], axis=1)[:, -(FFN_CONV - 1):]
    x2 = matmul(act.reshape(b * t, D_FF), w_down.astype(BF16), res=x2)
    return x2.reshape(b, t, d), state


def kernel(x_prompt, x_sample, cache_moba_k, cache_moba_v, state_ssd, state_ssd_conv, cache_nsa_cmp_k, cache_nsa_cmp_v, cache_nsa_sel_k, cache_nsa_sel_v, state_nsa_win_k, state_nsa_win_v, state_ffn_conv, page_table, norm_mix, norm_ffn, even_w_in, even_w_out, moba_qk_norm, ssd_conv_w, ssd_conv_b, ssd_dt_bias, ssd_a_log, ssd_d, ssd_norm, odd_w_in, odd_w_out, nsa_qk_norm, cmp_pe, cmp_w1, cmp_b1, cmp_w2, cmp_b2, ffn_w_up, ffn_conv_w, ffn_conv_b, ffn_w_down):
    depth = norm_mix.shape[0]
    past_len = page_table.shape[1] * PAGE
    state_keys = ("moba_k", "moba_v", "ssd", "ssd_conv", "nsa_cmp_k", "nsa_cmp_v",
                  "nsa_sel_k", "nsa_sel_v", "nsa_win_k", "nsa_win_v", "ffn_conv")
    moba_pool = lambda c: c.reshape(c.shape[0], c.shape[1], PAGE, MOBA_W)
    cache_mk, cache_mv = moba_pool(cache_moba_k), moba_pool(cache_moba_v)

    def trunk(x, q_start, sample):
        b = x.shape[0]
        new = {name: [] for name in state_keys}
        for li in range(depth):
            e = li // 2
            if li % 2 == 0:
                cp = state_ssd_conv[e] if sample else None
                sp = state_ssd[e] if sample else jnp.zeros((b, SSD_HEADS, SSD_HEAD_DIM, SSD_STATE), F32)
                x, k, v, cs, ss = _even_layer(x, sample, q_start, cache_mk, cache_mv, e, page_table, cp, sp,
                                              norm_mix[li], even_w_in[e], even_w_out[e], moba_qk_norm[e],
                                              ssd_conv_w[e], ssd_conv_b[e], ssd_dt_bias[e], ssd_a_log[e],
                                              ssd_d[e], ssd_norm[e])
                for name, val in zip(("moba_k", "moba_v", "ssd_conv", "ssd"), (k, v, cs, ss)):
                    new[name].append(val)
            else:
                caches = (cache_nsa_cmp_k, cache_nsa_cmp_v, cache_nsa_sel_k, cache_nsa_sel_v)
                outs = _odd_layer(x, sample, q_start, caches, e, page_table, state_nsa_win_k, state_nsa_win_v,
                                  norm_mix[li], odd_w_in[e], odd_w_out[e], nsa_qk_norm[e], cmp_pe[e],
                                  cmp_w1[e], cmp_b1[e], cmp_w2[e], cmp_b2[e])
                x = outs[0]
                for name, val in zip(("nsa_cmp_k", "nsa_cmp_v", "nsa_sel_k", "nsa_sel_v", "nsa_win_k", "nsa_win_v"), outs[1:]):
                    new[name].append(val)
            fp = state_ffn_conv[li] if sample else None
            x, fs = _conv_ffn(x, fp, norm_ffn[li], ffn_w_up[li], ffn_conv_w[li], ffn_conv_b[li], ffn_w_down[li])
            new["ffn_conv"].append(fs)
        return x, {name: jnp.stack(rows) for name, rows in new.items()}

    y_prompt, sp = trunk(x_prompt, 0, False)
    y_sample, ss = trunk(x_sample, past_len, True)
    out = [y_prompt, y_sample]
    for name in state_keys:
        out += [sp[name], ss[name]]
    return tuple(out)
```

```python
import functools

import numpy as np
import jax
import jax.numpy as jnp
from jax import lax
from jax.experimental import pallas as pl
from jax.experimental.pallas import tpu as pltpu

F32 = jnp.float32
BF16 = jnp.bfloat16

LANE = 128
VMEM_LIMIT_BYTES = 56 * 1024 * 1024

EPS = 1e-6
HEAD_DIM = 128
SCALE = HEAD_DIM ** -0.5
EXP2_SCALE = SCALE * float(np.log2(np.e))
NEG = -1e30

PAGE = 128
MOBA_HEADS = 16
MOBA_BLOCK = 256
MOBA_TOPK = 3
SSD_HEADS = 64
SSD_HEAD_DIM = 64
SSD_GROUPS = 8
SSD_STATE = 128
SSD_CONV = 4
SSD_CHUNK = 128
D_INNER = 4096
GROUP_W = D_INNER // SSD_GROUPS
HEADS_PER_GROUP = SSD_HEADS // SSD_GROUPS
NSA_KV = 4
NSA_HPG = 8
KV_W = NSA_KV * HEAD_DIM
CMP_LEN = 32
CMP_STRIDE = 16
SEL_BLOCK = 64
SEL_TOPN = 16
WINDOW = 512
FFN_CONV = 3
CHUNK_W = CMP_STRIDE * KV_W


def _params(*sem):
    return pltpu.CompilerParams(dimension_semantics=sem, vmem_limit_bytes=VMEM_LIMIT_BYTES)


def _dot(a, b):
    return jnp.dot(a, b, preferred_element_type=F32)


def _dot_nt(a, b):
    return lax.dot_general(a, b, (((1,), (1,)), ((), ())), preferred_element_type=F32)


def _dot_tn(a, b):
    return lax.dot_general(a, b, (((0,), (0,)), ((), ())), preferred_element_type=F32)


def _split3(x):
    hi = x.astype(BF16)
    r1 = x - hi.astype(F32)
    mid = r1.astype(BF16)
    lo = (r1 - mid.astype(F32)).astype(BF16)
    return hi, mid, lo


def _sigmoid(x):
    return 1.0 / (1.0 + jnp.exp(-x))


def _silu(x):
    return x * _sigmoid(x)


def _softplus(x):
    return jnp.maximum(x, 0.0) + jnp.log1p(jnp.exp(-jnp.abs(x)))


def _gelu_tanh(x):
    return 0.5 * x * (1.0 + jnp.tanh(np.sqrt(2.0 / np.pi).astype(np.float32) * (x + 0.044715 * (x * x * x))))


def _rms(x, g):
    return x * lax.rsqrt(jnp.mean(x * x, axis=-1, keepdims=True) + EPS) * g


def _iota(shape, axis):
    return lax.broadcasted_iota(jnp.int32, shape, axis)


def _rank_before(score, n_cols):
    col = _iota(score.shape, 1)
    rank = jnp.zeros(score.shape, F32)
    for c in range(n_cols):
        sc = score[:, c:c + 1]
        rank = rank + jnp.where((sc > score) | ((sc == score) & (c < col)), 1.0, 0.0)
    return rank


def _pick_col(x, c):
    return jnp.sum(jnp.where(_iota(x.shape, 1) == c, x, 0.0), axis=1, keepdims=True)


def _online_update(carry, s, msk, v_bf16):
    m, l, acc = carry
    s = jnp.where(msk, s, NEG)
    m_new = jnp.maximum(m, jnp.max(s, axis=1, keepdims=True))
    p = jnp.where(msk, jnp.exp(s - m_new), 0.0)
    alpha = jnp.exp(m - m_new)
    l = alpha * l + jnp.sum(p, axis=1, keepdims=True)
    acc = alpha * acc + _dot(p.astype(BF16), v_bf16)
    return m_new, l, acc


def _flash_init(rows, dv):
    return (jnp.full((rows, 1), NEG, F32), jnp.zeros((rows, 1), F32), jnp.zeros((rows, dv), F32))


def _flash_out(carry):
    _, l, acc = carry
    return acc / jnp.where(l > 0.0, l, 1.0)


def _rmsnorm_kernel(x_ref, g_ref, o_ref):
    o_ref[...] = _rms(x_ref[...], g_ref[...]).astype(o_ref.dtype)


def rmsnorm_rows(x2d, g):
    m, d = x2d.shape
    tm = min(m, 256)
    return pl.pallas_call(
        _rmsnorm_kernel,
        grid=(m // tm,),
        in_specs=[pl.BlockSpec((tm, d), lambda i: (i, 0)), pl.BlockSpec((1, d), lambda i: (0, 0))],
        out_specs=pl.BlockSpec((tm, d), lambda i: (i, 0)),
        out_shape=jax.ShapeDtypeStruct((m, d), BF16),
        compiler_params=_params("parallel"),
        name="rmsnorm_rows",
    )(x2d, g.reshape(1, d))


def _matmul_kernel(*refs, nk, has_res):
    if has_res:
        a_ref, b_ref, r_ref, o_ref = refs
    else:
        a_ref, b_ref, o_ref = refs
    part = _dot(a_ref[...].astype(BF16), b_ref[...])
    if nk == 1:
        o_ref[...] = part + r_ref[...] if has_res else part
        return
    k = pl.program_id(2)

    @pl.when(k == 0)
    def _():
        o_ref[...] = part + r_ref[...] if has_res else part

    @pl.when(k > 0)
    def _():
        o_ref[...] += part


def matmul(a, b, res=None, tm=1024, tn=512, tk=4096):
    m, k = a.shape
    n = b.shape[1]
    tm, tn, tk = min(tm, m), min(tn, n), min(tk, k)
    assert m % tm == 0 and n % tn == 0 and k % tk == 0, (a.shape, b.shape)
    nk = k // tk
    in_specs = [pl.BlockSpec((tm, tk), lambda i, j, kk: (i, kk)), pl.BlockSpec((tk, tn), lambda i, j, kk: (kk, j))]
    args = [a, b]
    if res is not None:
        in_specs.append(pl.BlockSpec((tm, tn), lambda i, j, kk: (i, j)))
        args.append(res)
    return pl.pallas_call(
        functools.partial(_matmul_kernel, nk=nk, has_res=res is not None),
        grid=(m // tm, n // tn, nk),
        in_specs=in_specs,
        out_specs=pl.BlockSpec((tm, tn), lambda i, j, kk: (i, j)),
        out_shape=jax.ShapeDtypeStruct((m, n), F32),
        compiler_params=_params("parallel", "parallel", "arbitrary"),
        name="matmul",
    )(*args)


def _matmul_ws_kernel(*refs, nk, has_res):
    a_ref, w_ref = refs[:2]
    r_ref = refs[2] if has_res else None
    o_ref, wbf_ref = refs[-2:]
    i = pl.program_id(1)
    kk = pl.program_id(2)

    @pl.when(i == 0)
    def _():
        wbf_ref[kk] = w_ref[...].astype(BF16)

    part = _dot(a_ref[...], wbf_ref[kk])
    if nk == 1:
        o_ref[...] = part + r_ref[...] if has_res else part
        return

    @pl.when(kk == 0)
    def _():
        o_ref[...] = part + r_ref[...] if has_res else part

    @pl.when(kk > 0)
    def _():
        o_ref[...] += part


def matmul_ws(a, w, layer, *, k, n, res=None, out_cols=None, tm=1024, tn=512, tk=4096):
    m = a.shape[0]
    tm, tn, tk = min(tm, m), min(tn, n), min(tk, k)
    assert m % tm == 0 and n % tn == 0 and k % tk == 0, (a.shape, w.shape, k, n)
    nk = k // tk
    if nk == 1:
        w_map = lambda j, i, kk: (layer, 0, j)
    else:
        w_map = lambda j, i, kk: (layer, jnp.where(i == 0, kk, nk - 1), j)
    o_map = (lambda j, i, kk: (i, j)) if out_cols is None else (lambda j, i, kk: (i, out_cols(j)))
    in_specs = [pl.BlockSpec((tm, tk), lambda j, i, kk: (i, kk)), pl.BlockSpec((None, tk, tn), w_map)]
    args = [a, w]
    if res is not None:
        in_specs.append(pl.BlockSpec((tm, tn), o_map))
        args.append(res)
    return pl.pallas_call(
        functools.partial(_matmul_ws_kernel, nk=nk, has_res=res is not None),
        grid=(n // tn, m // tm, nk),
        in_specs=in_specs,
        out_specs=pl.BlockSpec((tm, tn), o_map),
        out_shape=jax.ShapeDtypeStruct((m, n), F32),
        scratch_shapes=[pltpu.VMEM((nk, tk, tn), BF16)],
        compiler_params=_params("arbitrary", "arbitrary", "arbitrary"),
        name="matmul_ws",
    )(*args)


def _head_norm_kernel(x_ref, g_ref, o_ref, *, heads):
    g = g_ref[...]
    for h in range(heads):
        sl = slice(h * HEAD_DIM, (h + 1) * HEAD_DIM)
        o_ref[:, sl] = _rms(x_ref[:, sl], g).astype(o_ref.dtype)


def head_norm(x, col0, n_heads, g, out_dtype):
    b, t, _ = x.shape
    hb = min(n_heads, 4)
    w = hb * HEAD_DIM
    assert col0 % w == 0 and n_heads % hb == 0
    tq = min(t, 512)
    return pl.pallas_call(
        functools.partial(_head_norm_kernel, heads=hb),
        grid=(b, t // tq, n_heads // hb),
        in_specs=[pl.BlockSpec((None, tq, w), lambda bi, i, j: (bi, i, col0 // w + j)),
                  pl.BlockSpec((1, HEAD_DIM), lambda bi, i, j: (0, 0))],
        out_specs=pl.BlockSpec((None, tq, w), lambda bi, i, j: (bi, i, j)),
        out_shape=jax.ShapeDtypeStruct((b, t, n_heads * HEAD_DIM), out_dtype),
        compiler_params=_params("parallel", "parallel", "parallel"),
        name="head_norm",
    )(x, g.reshape(1, HEAD_DIM))


def _dwconv_kernel(*refs, width, t, glu):
    if glu:
        xa_ref, xg_ref, pa_ref, pg_ref, wa_ref, wg_ref, ba_ref, bg_ref, o_ref, sa_ref, sg_ref = refs
        pairs = ((xa_ref, pa_ref, sa_ref), (xg_ref, pg_ref, sg_ref))
    else:
        xa_ref, pa_ref, wa_ref, ba_ref, o_ref, sa_ref = refs
        pairs = ((xa_ref, pa_ref, sa_ref),)
    for x_ref, p_ref, s_ref in pairs:
        s_ref[0:8, :] = p_ref[...]
        s_ref[8:8 + t, :] = x_ref[...]
    tc = min(t, 256)

    def conv(s_ref, w_ref, b_ref, r0):
        acc = b_ref[...]
        for i in range(width):
            acc = acc + w_ref[i:i + 1, :] * s_ref[pl.ds(8 - (width - 1) + i + r0, tc), :]
        return acc

    for c in range(t // tc):
        r0 = c * tc
        a = conv(sa_ref, wa_ref, ba_ref, r0)
        if glu:
            o_ref[r0:r0 + tc, :] = (a * _silu(conv(sg_ref, wg_ref, bg_ref, r0))).astype(o_ref.dtype)
        else:
            o_ref[r0:r0 + tc, :] = _silu(a).astype(o_ref.dtype)


def dwconv(x, col0, c_out, prev8, w, bias, glu, out_dtype):
    b, t, _ = x.shape
    width = w.shape[0]
    tc = min(256 if t > 64 else 2048, c_out)
    assert col0 % tc == 0 and c_out % tc == 0
    nj = c_out // tc
    x_spec = lambda off: pl.BlockSpec((None, t, tc), lambda bi, j: (bi, 0, off + j))
    p_spec = lambda off: pl.BlockSpec((None, 8, tc), lambda bi, j: (bi, 0, off + j))
    w_spec = lambda off: pl.BlockSpec((width, tc), lambda bi, j: (0, off + j))
    b_spec = lambda off: pl.BlockSpec((1, tc), lambda bi, j: (0, off + j))
    bias2 = bias.reshape(1, -1)
    if glu:
        in_specs = [x_spec(col0 // tc), x_spec(col0 // tc + nj), p_spec(0), p_spec(nj),
                    w_spec(0), w_spec(nj), b_spec(0), b_spec(nj)]
        args = (x, x, prev8, prev8, w, w, bias2, bias2)
        scratch = [pltpu.VMEM((8 + t, tc), F32), pltpu.VMEM((8 + t, tc), F32)]
    else:
        in_specs = [x_spec(col0 // tc), p_spec(0), w_spec(0), b_spec(0)]
        args = (x, prev8, w, bias2)
        scratch = [pltpu.VMEM((8 + t, tc), F32)]
    return pl.pallas_call(
        functools.partial(_dwconv_kernel, width=width, t=t, glu=glu),
        grid=(b, nj),
        in_specs=in_specs,
        out_specs=pl.BlockSpec((None, t, tc), lambda bi, j: (bi, 0, j)),
        out_shape=jax.ShapeDtypeStruct((b, t, c_out), out_dtype),
        scratch_shapes=scratch,
        compiler_params=_params("parallel", "parallel"),
        name="dwconv_glu" if glu else "dwconv_silu",
    )(*args)


def _history8(state, b, c):
    if state is None:
        return jnp.zeros((b, 8, c), F32)
    return jnp.concatenate([jnp.zeros((b, 8 - state.shape[1], c), F32), state], axis=1)


def _moba_prompt_kernel(q_ref, k_ref, v_ref, o_ref, kmean_ref, kbf_ref, vbf_ref, selb_ref, m_ref, part_ref, acc_ref,
                        *, nb):
    blk = MOBA_BLOCK
    half = blk // 2
    i = pl.program_id(2)

    @pl.when(i == 0)
    def _():
        kmean_ref[...] = jnp.zeros_like(kmean_ref)
        for n in range(nb):
            kmean_ref[n:n + 1, :] = jnp.mean(k_ref[n * blk:(n + 1) * blk, :], axis=0, keepdims=True)
        kbf_ref[...] = k_ref[...].astype(BF16)
        vbf_ref[...] = v_ref[...].astype(BF16)

    halves = range(2)
    fold = lambda a, op: functools.reduce(op, [a[:, c * LANE:(c + 1) * LANE] for c in range(a.shape[1] // LANE)])
    own_rows = pl.ds(pl.multiple_of(i * blk, blk), blk)
    k_own, v_own = kbf_ref[own_rows, :], vbf_ref[own_rows, :]
    qs, own_scores = [], []
    spread = jnp.where(_iota((LANE, nb * LANE), 0) == _iota((LANE, nb * LANE), 1) // LANE, 1.0, 0.0).astype(BF16)
    for hq in halves:
        q = q_ref[hq * half:(hq + 1) * half, :]
        gate = _dot_nt(q, kmean_ref[...].astype(BF16))
        col = _iota(gate.shape, 1)
        valid = col < i
        gate = jnp.where(valid, gate, -jnp.inf)
        sel = jnp.where(valid & (_rank_before(gate, nb) < min(MOBA_TOPK, nb)), 1.0, 0.0).astype(BF16)
        selb_ref[hq] = _dot(sel, spread)
        s = _dot_nt(q, k_own)
        s = jnp.where(_iota(s.shape, 1) <= _iota(s.shape, 0) + hq * half, s, NEG)
        part_ref[hq] = fold(s, jnp.maximum)
        qs.append(q), own_scores.append(s)

    def pair_scores(n2, hq):
        r0 = pl.multiple_of(n2 * 2 * blk, 2 * blk)
        picked = selb_ref[hq, :, pl.ds(pl.multiple_of(n2 * 2 * LANE, 2 * LANE), 2 * LANE)]
        keep = jnp.concatenate([picked[:, (c * LANE // blk) * LANE:(c * LANE // blk + 1) * LANE]
                                for c in range(2 * blk // LANE)], axis=1) > 0.5
        return r0, jnp.where(keep, _dot_nt(qs[hq], kbf_ref[pl.ds(r0, 2 * blk), :]), NEG)

    n_pairs = (i + 1) // 2

    def max_sweep(n2, _):
        for hq in halves:
            part_ref[hq] = jnp.maximum(part_ref[hq], fold(pair_scores(n2, hq)[1], jnp.maximum))
        return 0

    lax.fori_loop(0, n_pairs, max_sweep, 0)
    for hq in halves:
        m = jnp.max(part_ref[hq], axis=1, keepdims=True)
        m_ref[hq] = m
        p = jnp.exp2((own_scores[hq] - m) * EXP2_SCALE)
        part_ref[hq] = fold(p, jnp.add)
        acc_ref[hq] = _dot(p.astype(BF16), v_own)

    def exp_sweep(n2, _):
        for hq in halves:
            r0, s = pair_scores(n2, hq)
            p = jnp.exp2((s - m_ref[hq]) * EXP2_SCALE)
            part_ref[hq] += fold(p, jnp.add)
            acc_ref[hq] += _dot(p.astype(BF16), vbf_ref[pl.ds(r0, 2 * blk), :])
        return 0

    lax.fori_loop(0, n_pairs, exp_sweep, 0)
    for hq in halves:
        l = jnp.sum(part_ref[hq], axis=1, keepdims=True)
        o_ref[hq * half:(hq + 1) * half, :] = (acc_ref[hq] / l).astype(o_ref.dtype)


def moba_prompt(qn, kn, proj, v_col0):
    b, t, w = qn.shape
    nh = w // HEAD_DIM
    nb = t // MOBA_BLOCK
    half = MOBA_BLOCK // 2
    assert t % (2 * MOBA_BLOCK) == 0 and nb <= LANE and half == LANE
    return pl.pallas_call(
        functools.partial(_moba_prompt_kernel, nb=nb),
        grid=(b, nh, nb),
        in_specs=[pl.BlockSpec((None, MOBA_BLOCK, HEAD_DIM), lambda bi, h, i: (bi, i, h)),
                  pl.BlockSpec((None, t, HEAD_DIM), lambda bi, h, i: (bi, 0, h)),
                  pl.BlockSpec((None, t, HEAD_DIM), lambda bi, h, i: (bi, 0, v_col0 // HEAD_DIM + h))],
        out_specs=pl.BlockSpec((None, MOBA_BLOCK, HEAD_DIM), lambda bi, h, i: (bi, i, h)),
        out_shape=jax.ShapeDtypeStruct((b, t, w), BF16),
        scratch_shapes=[pltpu.VMEM((LANE, HEAD_DIM), F32), pltpu.VMEM((t, HEAD_DIM), BF16), pltpu.VMEM((t, HEAD_DIM), BF16),
                        pltpu.VMEM((2, half, nb * LANE), F32), pltpu.VMEM((2, half, 1), F32),
                        pltpu.VMEM((2, half, LANE), F32), pltpu.VMEM((2, half, HEAD_DIM), F32)],
        compiler_params=_params("parallel", "parallel", "arbitrary"),
        name="moba_prompt",
    )(qn, kn, proj)


def _moba_dec_block_kernel(pt_ref, q_ref, ka_ref, kb_ref, va_ref, vb_ref,
                           acc_ref, g_ref, m_ref, l_ref, *, nh, tq):
    n = pl.program_id(1)
    q = q_ref[...]
    kb = jnp.concatenate([ka_ref[...], kb_ref[...]], axis=0).astype(BF16)
    vb = jnp.concatenate([va_ref[...], vb_ref[...]], axis=0).astype(BF16)
    s_raw = _dot_nt(q, kb)
    same_head = _iota(s_raw.shape, 1) % nh == _iota(s_raw.shape, 0) // tq
    gate = jnp.sum(jnp.where(same_head, s_raw, 0.0), axis=1, keepdims=True) * (1.0 / MOBA_BLOCK)
    s = jnp.where(same_head, s_raw * SCALE, NEG)
    m = jnp.max(s, axis=1, keepdims=True)
    p = jnp.exp(s - m)
    l = jnp.sum(p, axis=1, keepdims=True)
    acc_ref[...] = _dot(p.astype(BF16), vb)

    @pl.when(n == 0)
    def _():
        g_ref[...] = jnp.zeros_like(g_ref)
        m_ref[...] = jnp.zeros_like(m_ref)
        l_ref[...] = jnp.zeros_like(l_ref)

    lane = _iota(g_ref.shape, 1)
    g_ref[...] = jnp.where(lane == n, gate, g_ref[...])
    m_ref[...] = jnp.where(lane == n, m, m_ref[...])
    l_ref[...] = jnp.where(lane == n, l, l_ref[...])


def _moba_dec_combine_kernel(g_ref, m_ref, l_ref, acc_ref, q_ref, kn_ref, vn_ref, o_ref, *, nbk, nh, tq):
    gate = g_ref[...]
    lane = _iota(gate.shape, 1)
    row = _iota(gate.shape, 0)
    valid = lane < nbk
    gate = jnp.where(valid, gate, -jnp.inf)
    sel = valid & (_rank_before(gate, nbk) < min(MOBA_TOPK, nbk + 1))
    q = q_ref[...]
    s_own = _dot_nt(q, kn_ref[...].astype(BF16)) * SCALE
    own = (lane % nh == row // tq) & (lane // nh <= row % tq)
    m_blk = m_ref[...]
    m_tot = jnp.maximum(jnp.max(jnp.where(sel, m_blk, NEG), axis=1, keepdims=True),
                        jnp.max(jnp.where(own, s_own, NEG), axis=1, keepdims=True))
    wgt = jnp.where(sel, jnp.exp(m_blk - m_tot), 0.0)
    p_own = jnp.where(own, jnp.exp(s_own - m_tot), 0.0)
    l_tot = jnp.sum(wgt * l_ref[...], axis=1, keepdims=True) + jnp.sum(p_own, axis=1, keepdims=True)
    acc = _dot(p_own.astype(BF16), vn_ref[...].astype(BF16))
    for n in range(nbk):
        acc = acc + wgt[:, n:n + 1] * acc_ref[n]
    o_ref[...] = acc / jnp.where(l_tot > 0.0, l_tot, 1.0)


def moba_sample(qn, kn_new, v_new, cache_k, cache_v, layer, page_table):
    b, tq, w = qn.shape
    nh = w // HEAD_DIM
    rows = nh * tq
    npages = page_table.shape[1]
    ppb = MOBA_BLOCK // PAGE
    nbk = npages // ppb
    assert rows == LANE and ppb == 2 and nbk <= LANE
    qrows = qn.reshape(b, tq, nh, HEAD_DIM).transpose(0, 2, 1, 3).reshape(b, rows, HEAD_DIM)
    kn_rows = kn_new.reshape(b, tq * nh, HEAD_DIM)
    vn_rows = v_new.reshape(b, tq * nh, HEAD_DIM)

    page = lambda j: pl.BlockSpec((None, None, PAGE * nh, HEAD_DIM), lambda bi, n, pt: (layer, pt[bi, ppb * n + j], 0, 0))
    stat = pl.BlockSpec((None, rows, LANE), lambda bi, n, pt: (bi, 0, 0))
    acc, g, m, l = pl.pallas_call(
        functools.partial(_moba_dec_block_kernel, nh=nh, tq=tq),
        grid_spec=pltpu.PrefetchScalarGridSpec(
            num_scalar_prefetch=1,
            grid=(b, nbk),
            in_specs=[pl.BlockSpec((None, rows, HEAD_DIM), lambda bi, n, pt: (bi, 0, 0)),
                      page(0), page(1), page(0), page(1)],
            out_specs=[pl.BlockSpec((None, None, rows, HEAD_DIM), lambda bi, n, pt: (bi, n, 0, 0)), stat, stat, stat],
        ),
        out_shape=[jax.ShapeDtypeStruct((b, nbk, rows, HEAD_DIM), F32)] + [jax.ShapeDtypeStruct((b, rows, LANE), F32)] * 3,
        compiler_params=_params("parallel", "arbitrary"),
        name="moba_sample_blocks",
    )(page_table, qrows, cache_k, cache_k, cache_v, cache_v)

    per_b = lambda *shape: pl.BlockSpec((None,) + shape, lambda bi: (bi,) + (0,) * len(shape))
    o = pl.pallas_call(
        functools.partial(_moba_dec_combine_kernel, nbk=nbk, nh=nh, tq=tq),
        grid=(b,),
        in_specs=[per_b(rows, LANE), per_b(rows, LANE), per_b(rows, LANE), per_b(nbk, rows, HEAD_DIM),
                  per_b(rows, HEAD_DIM), per_b(tq * nh, HEAD_DIM), per_b(tq * nh, HEAD_DIM)],
        out_specs=per_b(rows, HEAD_DIM),
        out_shape=jax.ShapeDtypeStruct((b, rows, HEAD_DIM), F32),
        compiler_params=_params("parallel"),
        name="moba_sample_combine",
    )(g, m, l, acc, qrows, kn_rows, vn_rows)
    return o.reshape(b, nh, tq, HEAD_DIM).transpose(0, 2, 1, 3).reshape(b, tq, w)


def _ssd_kernel(x_ref, bm_ref, cm_ref, z_ref, dtc_ref, dtr_ref, bias_c_ref, bias_r_ref, alog_c_ref, alog_r_ref,
                dskip_ref, gout_ref, s0_ref, y_ref, sfin_ref, state_ref, *, q, t_valid):
    c = pl.program_id(1)
    nc = pl.num_programs(1)

    @pl.when(c == 0)
    def _():
        state_ref[...] = s0_ref[...]

    dtc = _softplus(dtc_ref[...] + bias_c_ref[...])
    dtr = _softplus(dtr_ref[...] + bias_r_ref[...])
    if t_valid is not None:
        dtc = jnp.where(c * q + _iota(dtc.shape, 0) < t_valid, dtc, 0.0)
        dtr = jnp.where(c * q + _iota(dtr.shape, 1) < t_valid, dtr, 0.0)
    da_c = dtc * -jnp.exp(alog_c_ref[...])
    da_r = dtr * -jnp.exp(alog_r_ref[...])
    tri = _iota((q, q), 0) >= _iota((q, q), 1)
    tri_bf = jnp.where(tri, 1.0, 0.0).astype(BF16)
    cum_c = sum(_dot(tri_bf, part) for part in _split3(da_c))
    cum_r = sum(_dot_nt(part, tri_bf) for part in _split3(da_r))
    lane = _iota((q, LANE), 1)
    lo_half = lane < SSD_HEAD_DIM
    row_lo = _iota((LANE, 1), 0) < SSD_HEAD_DIM
    pairs = HEADS_PER_GROUP // 2

    for g in range(SSD_GROUPS):
        xg = x_ref[:, g * GROUP_W:(g + 1) * GROUP_W]
        bm = bm_ref[:, g * SSD_STATE:(g + 1) * SSD_STATE].astype(BF16)
        cm = cm_ref[:, g * SSD_STATE:(g + 1) * SSD_STATE].astype(BF16)
        cb = _dot_nt(cm, bm)
        y_parts = []
        for pr in range(pairs):
            x_pair = xg[:, pr * LANE:(pr + 1) * LANE]
            x_bf = x_pair.astype(BF16)
            ys, cols, lasts = [], [], []
            for h in (g * HEADS_PER_GROUP + 2 * pr, g * HEADS_PER_GROUP + 2 * pr + 1):
                col = cum_c[:, h:h + 1]
                seg = col - cum_r[h:h + 1, :]
                decay = jnp.where(tri, jnp.exp(jnp.where(tri, seg, 0.0)), 0.0)
                wgt = cb * decay * dtr[h:h + 1, :]
                ys.append(_dot(wgt.astype(BF16), x_bf))
                cols.append(col)
                lasts.append(cum_c[q - 1:q, h:h + 1])
            sidx = g * pairs + pr
            state = state_ref[sidx]
            y_pair = jnp.where(lo_half, ys[0], ys[1])
            carry_in = _dot_nt(cm, state.astype(BF16))
            y_pair = y_pair + carry_in * jnp.exp(jnp.where(lo_half, cols[0], cols[1]))
            y_parts.append(y_pair)
            tail = jnp.where(lo_half,
                             jnp.exp(lasts[0] - cols[0]) * dtc[:, 2 * sidx:2 * sidx + 1],
                             jnp.exp(lasts[1] - cols[1]) * dtc[:, 2 * sidx + 1:2 * sidx + 2])
            upd = _dot_tn((x_pair * tail).astype(BF16), bm)
            state_ref[sidx] = state * jnp.where(row_lo, jnp.exp(lasts[0]), jnp.exp(lasts[1])) + upd
        sl = slice(g * GROUP_W, (g + 1) * GROUP_W)
        yg = jnp.concatenate(y_parts, axis=1) + xg * dskip_ref[:, sl]
        yg = yg * _silu(z_ref[:, sl])
        y_ref[:, sl] = _rms(yg, gout_ref[:, sl]).astype(y_ref.dtype)

    @pl.when(c == nc - 1)
    def _():
        sfin_ref[...] = state_ref[...]


def ssd_mixer(conv, z_src, z_col0, dt_raw, s0, dt_bias, a_log, d_skip, g_out, t_valid):
    b, t, _ = conv.shape
    q = SSD_CHUNK
    assert t % q == 0
    nc = t // q
    dtr = dt_raw.transpose(0, 2, 1)
    pad_h = LANE - SSD_HEADS
    bias_c = jnp.pad(dt_bias, (0, pad_h)).reshape(1, LANE)
    bias_r = bias_c.reshape(LANE, 1)
    alog_c = jnp.pad(a_log, (0, pad_h)).reshape(1, LANE)
    alog_r = alog_c.reshape(LANE, 1)
    dskip = jnp.repeat(d_skip, SSD_HEAD_DIM).reshape(1, D_INNER)
    npair = SSD_HEADS // 2
    s0p = s0.reshape(b, npair, 2 * SSD_HEAD_DIM, SSD_STATE)
    nbc = D_INNER // (SSD_GROUPS * SSD_STATE)
    const = lambda shape: pl.BlockSpec(shape, lambda bi, c: (0,) * len(shape))
    y, sfin = pl.pallas_call(
        functools.partial(_ssd_kernel, q=q, t_valid=t_valid),
        grid=(b, nc),
        in_specs=[pl.BlockSpec((None, q, D_INNER), lambda bi, c: (bi, c, 0)),
                  pl.BlockSpec((None, q, SSD_GROUPS * SSD_STATE), lambda bi, c: (bi, c, nbc)),
                  pl.BlockSpec((None, q, SSD_GROUPS * SSD_STATE), lambda bi, c: (bi, c, nbc + 1)),
                  pl.BlockSpec((None, q, D_INNER), lambda bi, c: (bi, c, z_col0 // D_INNER)),
                  pl.BlockSpec((None, q, LANE), lambda bi, c: (bi, c, 0)),
                  pl.BlockSpec((None, LANE, q), lambda bi, c: (bi, 0, c)),
                  const((1, LANE)), const((LANE, 1)), const((1, LANE)), const((LANE, 1)),
                  const((1, D_INNER)), const((1, D_INNER)),
                  pl.BlockSpec((None, npair, LANE, SSD_STATE), lambda bi, c: (bi, 0, 0, 0))],
        out_specs=[pl.BlockSpec((None, q, D_INNER), lambda bi, c: (bi, c, 0)),
                   pl.BlockSpec((None, npair, LANE, SSD_STATE), lambda bi, c: (bi, 0, 0, 0))],
        out_shape=[jax.ShapeDtypeStruct((b, t, D_INNER), BF16),
                   jax.ShapeDtypeStruct((b, npair, LANE, SSD_STATE), F32)],
        scratch_shapes=[pltpu.VMEM((npair, LANE, SSD_STATE), F32)],
        compiler_params=_params("parallel", "arbitrary"),
        name="ssd_scan",
    )(conv, conv, conv, z_src, dt_raw, dtr, bias_c, bias_r, alog_c, alog_r, dskip, g_out.reshape(1, D_INNER), s0p)
    return y, sfin.reshape(b, SSD_HEADS, SSD_HEAD_DIM, SSD_STATE)


def _compress_kernel(pt_ref, a_ref, b_ref, cpe_a_ref, cpe_b_ref, b1_ref, w2_ref, b2_ref, g_ref, o_ref,
                     ha_ref, hb_ref, *, npg, norm):
    bi = pl.program_id(0)
    ncp = npg * 8

    def gather(p, _):
        src = pl.ds(pl.multiple_of(pt_ref[bi, p] * 8, 8), 8)
        dst = pl.ds(pl.multiple_of(p * 8, 8), 8)
        ha_ref[dst, :] = a_ref[src, :]
        hb_ref[dst, :] = b_ref[src, :]
        return 0

    lax.fori_loop(0, npg, gather, 0)
    hb_ref[ncp:ncp + 8, :] = jnp.zeros((8, HEAD_DIM), F32)
    hid = ha_ref[...] + hb_ref[pl.ds(1, ncp), :] + (cpe_a_ref[0:1, :] + cpe_b_ref[1:2, :] + b1_ref[...])
    out = _dot(_gelu_tanh(hid).astype(BF16), w2_ref[...]) + b2_ref[...]
    if norm:
        out = _rms(out, g_ref[...])
    o_ref[...] = out


def compress_tokens(rows_flat, page_table, pe, w1, b1, w2, b2, g_norm):
    nck = rows_flat.shape[0]
    b, npg = page_table.shape
    ncp = npg * 8
    half = CMP_STRIDE * HEAD_DIM
    eye = jnp.eye(NSA_KV, dtype=F32)
    blockdiag = lambda wh: jnp.einsum("ldj,gh->lgdhj", wh.reshape(CMP_STRIDE, HEAD_DIM, HEAD_DIM), eye).reshape(CHUNK_W, KV_W)
    w_big = jnp.concatenate([blockdiag(w1[:half]), blockdiag(w1[half:])], axis=1).astype(BF16)
    ab = matmul(rows_flat, w_big, tm=256, tk=2048)
    pe_rows = jnp.broadcast_to(pe.reshape(2, CMP_STRIDE, 1, HEAD_DIM), (2, CMP_STRIDE, NSA_KV, HEAD_DIM)).reshape(2, CHUNK_W)
    cpe = matmul(jnp.pad(pe_rows, ((0, 6), (0, 0))), w_big, tk=2048)
    col = lambda off: (lambda bi, g, pt: (0, off + g))
    return pl.pallas_call(
        functools.partial(_compress_kernel, npg=npg, norm=g_norm is not None),
        grid_spec=pltpu.PrefetchScalarGridSpec(
            num_scalar_prefetch=1,
            grid=(b, NSA_KV),
            in_specs=[pl.BlockSpec((nck, HEAD_DIM), col(0)), pl.BlockSpec((nck, HEAD_DIM), col(NSA_KV)),
                      pl.BlockSpec((8, HEAD_DIM), col(0)), pl.BlockSpec((8, HEAD_DIM), col(NSA_KV)),
                      pl.BlockSpec((1, HEAD_DIM), lambda bi, g, pt: (0, 0)),
                      pl.BlockSpec((HEAD_DIM, HEAD_DIM), lambda bi, g, pt: (0, 0)),
                      pl.BlockSpec((1, HEAD_DIM), lambda bi, g, pt: (0, 0)),
                      pl.BlockSpec((1, HEAD_DIM), lambda bi, g, pt: (0, 0))],
            out_specs=pl.BlockSpec((None, ncp, HEAD_DIM), lambda bi, g, pt: (bi, 0, g)),
            scratch_shapes=[pltpu.VMEM((ncp, HEAD_DIM), F32), pltpu.VMEM((ncp + 8, HEAD_DIM), F32)],
        ),
        out_shape=jax.ShapeDtypeStruct((b, ncp, KV_W), F32),
        compiler_params=_params("parallel", "parallel"),
        name="nsa_compress",
    )(page_table, ab, ab, cpe, cpe, b1.reshape(1, -1), w2.astype(BF16), b2.reshape(1, -1),
      (g_norm if g_norm is not None else jnp.ones((HEAD_DIM,), F32)).reshape(1, -1))


def _nsa_cmp_kernel(q_ref, kc_ref, vc_ref, o_ref, sel_ref, *, tq, q_start, n_sel, nsp):
    i = pl.program_id(2)
    kc = kc_ref[...].astype(BF16)
    vc = vc_ref[...].astype(BF16)
    ncp = kc.shape[0]
    qpos = q_start + i * tq + _iota((tq, 1), 0)
    end = _iota((tq, ncp), 1) * CMP_STRIDE + (CMP_LEN - 1)
    msk = end <= qpos
    imp = jnp.zeros((tq, ncp), F32)
    for r in range(NSA_HPG):
        sl = slice(r * HEAD_DIM, (r + 1) * HEAD_DIM)
        s = jnp.where(msk, _dot_nt(q_ref[:, sl], kc) * SCALE, NEG)
        p = jnp.where(msk, jnp.exp(s - jnp.max(s, axis=1, keepdims=True)), 0.0)
        d = jnp.sum(p, axis=1, keepdims=True)
        p = p / jnp.where(d > 0.0, d, 1.0)
        o_ref[:, sl] = _dot(p.astype(BF16), vc)
        imp = imp + p
    cn = _iota((ncp, nsp), 0)
    sj = _iota((ncp, nsp), 1)
    overlap = jnp.where((cn >= 4 * sj - 1) & (cn <= 4 * sj + 3), 1.0, 0.0).astype(BF16)
    score = sum(_dot(part, overlap) for part in _split3(imp))
    j = _iota((tq, nsp), 1)
    qblk = qpos // SEL_BLOCK
    valid = j <= qblk
    forced = (j == 0) | (j == qblk) | (j == qblk - 1)
    score = jnp.where(valid, jnp.where(forced, jnp.inf, score), -jnp.inf)
    sel_ref[...] = jnp.where(valid & (_rank_before(score, n_sel) < min(SEL_TOPN, n_sel)), 1.0, 0.0)


def nsa_compressed(qn, kc, vc, q_start, n_sel):
    b, t, w = qn.shape
    ncp = kc.shape[1]
    gw = NSA_HPG * HEAD_DIM
    tq = min(t, 256)
    nsp = -(-n_sel // LANE) * LANE
    return pl.pallas_call(
        functools.partial(_nsa_cmp_kernel, tq=tq, q_start=q_start, n_sel=n_sel, nsp=nsp),
        grid=(b, NSA_KV, t // tq),
        in_specs=[pl.BlockSpec((None, tq, gw), lambda bi, g, i: (bi, i, g)),
                  pl.BlockSpec((None, ncp, HEAD_DIM), lambda bi, g, i: (bi, 0, g)),
                  pl.BlockSpec((None, ncp, HEAD_DIM), lambda bi, g, i: (bi, 0, g))],
        out_specs=[pl.BlockSpec((None, tq, gw), lambda bi, g, i: (bi, i, g)),
                   pl.BlockSpec((None, None, tq, nsp), lambda bi, g, i: (bi, g, i, 0))],
        out_shape=[jax.ShapeDtypeStruct((b, t, w), F32), jax.ShapeDtypeStruct((b, NSA_KV, t, nsp), F32)],
        compiler_params=_params("parallel", "parallel", "parallel"),
        name="nsa_compressed",
    )(qn, kc, vc)


def _nsa_prompt_kernel(q_ref, sk_ref, sv_ref, wk_ref, wv_ref, sel_ref, ocmp_ref, gl_ref, o_ref,
                       m_ref, part_ref, acc_ref, *, tq, tk):
    i = pl.program_id(2)
    q0 = i * tq
    tpos = q0 + _iota((tq, tk), 0)
    sel = sel_ref[...].astype(BF16)
    nsp = sel.shape[1]

    def sel_mask(k0):
        kpos = k0 + _iota((tq, tk), 1)
        expand = jnp.where(_iota((nsp, tk), 0) == (k0 + _iota((nsp, tk), 1)) // SEL_BLOCK, 1.0, 0.0).astype(BF16)
        return (_dot(sel, expand) > 0.5) & (kpos <= tpos)

    def win_mask(k0):
        kpos = k0 + _iota((tq, tk), 1)
        return (kpos <= tpos) & (kpos > tpos - WINDOW)

    n_hi = (q0 + tq - 1) // tk + 1
    branches = ((sk_ref, sv_ref, sel_mask, 0), (wk_ref, wv_ref, win_mask, jnp.maximum(q0 - (WINDOW - 1), 0) // tk))
    fold = lambda a, op: functools.reduce(op, [a[:, c * LANE:(c + 1) * LANE] for c in range(tk // LANE)])
    for br, (k_ref, v_ref, mask_fn, n_lo) in enumerate(branches):
        base = br * NSA_HPG
        for r in range(NSA_HPG):
            part_ref[base + r] = jnp.full((tq, LANE), NEG, F32)
            acc_ref[base + r] = jnp.zeros((tq, HEAD_DIM), F32)

        def key_tile(n, k_ref=k_ref, mask_fn=mask_fn):
            k0 = pl.multiple_of(n * tk, tk)
            kb, msk = k_ref[pl.ds(k0, tk), :].astype(BF16), mask_fn(k0)
            return k0, lambda r: jnp.where(msk, _dot_nt(q_ref[:, r * HEAD_DIM:(r + 1) * HEAD_DIM], kb), NEG)

        def max_sweep(n, _, base=base, key_tile=key_tile):
            _, scores = key_tile(n)
            for r in range(NSA_HPG):
                part_ref[base + r] = jnp.maximum(part_ref[base + r], fold(scores(r), jnp.maximum))
            return 0

        lax.fori_loop(n_lo, n_hi, max_sweep, 0)
        for r in range(NSA_HPG):
            m_ref[base + r] = jnp.max(part_ref[base + r], axis=1, keepdims=True)
            part_ref[base + r] = jnp.zeros((tq, LANE), F32)

        def exp_sweep(n, _, base=base, key_tile=key_tile, v_ref=v_ref):
            k0, scores = key_tile(n)
            vb = v_ref[pl.ds(k0, tk), :].astype(BF16)
            for r in range(NSA_HPG):
                p = jnp.exp2((scores(r) - m_ref[base + r]) * EXP2_SCALE)
                part_ref[base + r] += fold(p, jnp.add)
                acc_ref[base + r] += _dot(p.astype(BF16), vb)
            return 0

        lax.fori_loop(n_lo, n_hi, exp_sweep, 0)

    gates = [_sigmoid(gl_ref[c]) for c in range(3)]
    for r in range(NSA_HPG):
        sl = slice(r * HEAD_DIM, (r + 1) * HEAD_DIM)
        o_sel = acc_ref[r] / jnp.sum(part_ref[r], axis=1, keepdims=True)
        o_win = acc_ref[NSA_HPG + r] / jnp.sum(part_ref[NSA_HPG + r], axis=1, keepdims=True)
        o = gates[0][:, r:r + 1] * ocmp_ref[:, sl] + gates[1][:, r:r + 1] * o_sel + gates[2][:, r:r + 1] * o_win
        o_ref[:, sl] = o.astype(o_ref.dtype)


def nsa_prompt(qn, skn, wkn, proj, sv_col0, wv_col0, selm, ocmp, gate_logits, tq=256, tk=256):
    b, t, w = qn.shape
    gw = NSA_HPG * HEAD_DIM
    tq, tk = min(t, tq), min(t, tk)
    nsp = selm.shape[-1]
    kv = lambda off: pl.BlockSpec((None, t, HEAD_DIM), lambda bi, g, i: (bi, 0, off + g))
    return pl.pallas_call(
        functools.partial(_nsa_prompt_kernel, tq=tq, tk=tk),
        grid=(b, NSA_KV, t // tq),
        in_specs=[pl.BlockSpec((None, tq, gw), lambda bi, g, i: (bi, i, g)),
                  kv(0), kv(sv_col0 // HEAD_DIM), kv(0), kv(wv_col0 // HEAD_DIM),
                  pl.BlockSpec((None, None, tq, nsp), lambda bi, g, i: (bi, g, i, 0)),
                  pl.BlockSpec((None, tq, gw), lambda bi, g, i: (bi, i, g)),
                  pl.BlockSpec((3, None, None, tq, NSA_HPG), lambda bi, g, i: (0, bi, g, i, 0))],
        out_specs=pl.BlockSpec((None, tq, gw), lambda bi, g, i: (bi, i, g)),
        out_shape=jax.ShapeDtypeStruct((b, t, w), BF16),
        scratch_shapes=[pltpu.VMEM((2 * NSA_HPG, tq, 1), F32), pltpu.VMEM((2 * NSA_HPG, tq, LANE), F32),
                        pltpu.VMEM((2 * NSA_HPG, tq, HEAD_DIM), F32)],
        compiler_params=_params("parallel", "parallel", "arbitrary"),
        name="nsa_prompt",
    )(qn, skn, proj, wkn, proj, selm, ocmp, gate_logits)


def _nsa_dec_kernel(pt_ref, q_ref, sel_ref, ocmp_ref, gl_ref, kp_ref, vp_ref, skn_ref, svn_ref,
                    wkb_ref, wvb_ref, wkn_ref, wvn_ref, o_ref, m_ref, l_ref, acc_ref, *, tq, past_len):
    p = pl.program_id(1)
    npages = pl.num_programs(1)
    rows = NSA_KV * NSA_HPG * tq
    per_g = NSA_HPG * tq
    q = q_ref[...]
    sel = sel_ref[...]

    @pl.when(p == 0)
    def _():
        m_ref[...] = jnp.full(m_ref.shape, NEG, F32)
        l_ref[...] = jnp.zeros_like(l_ref)
        acc_ref[...] = jnp.zeros_like(acc_ref)

    def grid_of(n_keys):
        shape = (rows, n_keys * NSA_KV)
        lane, row = _iota(shape, 1), _iota(shape, 0)
        return lane // NSA_KV, lane % NSA_KV == row // per_g, row % tq

    def update(carry, k_rows, v_rows, msk):
        s = _dot_nt(q, k_rows.astype(BF16)) * SCALE
        return _online_update(carry, s, msk, v_rows.astype(BF16))

    key, same_g, _ = grid_of(PAGE)
    blocks_per_page = PAGE // SEL_BLOCK
    picked = jnp.where(key < SEL_BLOCK, _pick_col(sel, blocks_per_page * p), _pick_col(sel, blocks_per_page * p + 1))
    carry = update((m_ref[...], l_ref[...], acc_ref[...]), kp_ref[...], vp_ref[...], same_g & (picked > 0.5))
    m_ref[...], l_ref[...], acc_ref[...] = carry

    @pl.when(p == npages - 1)
    def _():
        n_new = skn_ref.shape[0] // NSA_KV
        key, same_g, t_row = grid_of(n_new)
        own = same_g & (key <= t_row) & (key < tq)
        new_blk = _pick_col(sel, past_len // SEL_BLOCK) > 0.5
        o_sel = _flash_out(update(carry, skn_ref[...], svn_ref[...], own & new_blk))
        wkey, wsame, wt = grid_of(WINDOW)
        w_carry = update(_flash_init(rows, HEAD_DIM), wkb_ref[...], wvb_ref[...], wsame & (wkey > wt))
        o_win = _flash_out(update(w_carry, wkn_ref[...], wvn_ref[...], own))
        gates = _sigmoid(gl_ref[...])
        o_ref[...] = gates[:, 0:1] * ocmp_ref[...] + gates[:, 1:2] * o_sel + gates[:, 2:3] * o_win


def nsa_sample(qn, skn_new, sv_new, wkn_new, wv_new, cache_sk, cache_sv, win_k, win_v, layer,
               page_table, selm, ocmp, gate_logits):
    b, tq, w = qn.shape
    rows = NSA_KV * NSA_HPG * tq
    npages = page_table.shape[1]
    past_len = npages * PAGE
    nsp = selm.shape[-1]
    assert win_k.shape[2] == WINDOW * NSA_KV and past_len % SEL_BLOCK == 0 and tq * NSA_KV <= LANE
    selrows = jnp.broadcast_to(selm[:, :, None], (b, NSA_KV, NSA_HPG, tq, nsp)).reshape(b, rows, nsp)
    to_rows = lambda a: a.reshape(b, tq, NSA_KV, NSA_HPG, -1).transpose(0, 2, 3, 1, 4).reshape(b, rows, -1)
    new_rows = lambda a: jnp.pad(a.reshape(b, tq * NSA_KV, HEAD_DIM), ((0, 0), (0, LANE - tq * NSA_KV), (0, 0)))
    per_b = lambda *shape: pl.BlockSpec((None,) + shape, lambda bi, p, pt: (bi,) + (0,) * len(shape))
    page = pl.BlockSpec((None, None, PAGE * NSA_KV, HEAD_DIM), lambda bi, p, pt: (layer, pt[bi, p], 0, 0))
    wbuf = pl.BlockSpec((None, None, WINDOW * NSA_KV, HEAD_DIM), lambda bi, p, pt: (layer, bi, 0, 0))
    o = pl.pallas_call(
        functools.partial(_nsa_dec_kernel, tq=tq, past_len=past_len),
        grid_spec=pltpu.PrefetchScalarGridSpec(
            num_scalar_prefetch=1,
            grid=(b, npages),
            in_specs=[per_b(rows, HEAD_DIM), per_b(rows, nsp), per_b(rows, HEAD_DIM), per_b(rows, 3),
                      page, page, per_b(LANE, HEAD_DIM), per_b(LANE, HEAD_DIM), wbuf, wbuf,
                      per_b(LANE, HEAD_DIM), per_b(LANE, HEAD_DIM)],
            out_specs=per_b(rows, HEAD_DIM),
            scratch_shapes=[pltpu.VMEM((rows, 1), F32), pltpu.VMEM((rows, 1), F32), pltpu.VMEM((rows, HEAD_DIM), F32)],
        ),
        out_shape=jax.ShapeDtypeStruct((b, rows, HEAD_DIM), F32),
        compiler_params=_params("parallel", "arbitrary"),
        name="nsa_sample",
    )(page_table, to_rows(qn), selrows, to_rows(ocmp), to_rows(gate_logits), cache_sk, cache_sv,
      new_rows(skn_new), new_rows(sv_new), win_k, win_v, new_rows(wkn_new), new_rows(wv_new))
    return o.reshape(b, NSA_KV, NSA_HPG, tq, HEAD_DIM).transpose(0, 3, 1, 2, 4).reshape(b, tq, w)


MOBA_W = MOBA_HEADS * HEAD_DIM
CONV_DIM = D_INNER + 2 * SSD_GROUPS * SSD_STATE
EVEN_MAIN = 3 * MOBA_W + D_INNER + CONV_DIM
Z_COL0 = 0
XBC_COL0 = D_INNER
Q_COL0 = XBC_COL0 + CONV_DIM
K_COL0 = Q_COL0 + MOBA_W
V_COL0 = K_COL0 + MOBA_W
NSA_Q = NSA_KV * NSA_HPG * HEAD_DIM
ODD_MAIN = NSA_Q + 6 * KV_W
D_FF = 8192


PROJ_TN = 512


def _proj(h2d, w, layer, main, out_cols=None):
    tail = w[layer, :, main:].astype(BF16)
    tail = jnp.pad(tail, ((0, 0), (0, LANE - tail.shape[1])))
    return matmul_ws(h2d, w, layer, k=h2d.shape[1], n=main, out_cols=out_cols, tn=PROJ_TN), matmul(h2d, tail)


def _even_layer(x, sample, q_start, cache_k, cache_v, e, page_table, conv_prev, ssm_prev,
                g_norm, w_in, w_out, qk_g, conv_w, conv_b, dt_bias, a_log, d_skip, g_out):
    b, t, d = x.shape
    x2 = x.reshape(b * t, d)
    h = rmsnorm_rows(x2, g_norm)
    n_tiles = EVEN_MAIN // PROJ_TN
    proj, dt_raw = _proj(h, w_in, e, EVEN_MAIN, out_cols=lambda j: (j + Q_COL0 // PROJ_TN) % n_tiles)
    proj = proj.reshape(b, t, EVEN_MAIN)
    dt_raw = dt_raw.reshape(b, t, LANE)
    qn = head_norm(proj, Q_COL0, MOBA_HEADS, qk_g[0], BF16)
    kn = head_norm(proj, K_COL0, MOBA_HEADS, qk_g[1], F32)
    v = proj[..., V_COL0:]
    if sample:
        o_attn = moba_sample(qn, kn, v, cache_k, cache_v, e, page_table).astype(BF16)
    else:
        o_attn = moba_prompt(qn, kn, proj, V_COL0)
    conv = dwconv(proj, XBC_COL0, CONV_DIM, _history8(conv_prev, b, CONV_DIM), conv_w, conv_b, False, F32)
    xbc = proj[..., XBC_COL0:Q_COL0]
    conv_state = jnp.concatenate([_history8(conv_prev, b, CONV_DIM), xbc], axis=1)[:, -(SSD_CONV - 1):]
    if t % SSD_CHUNK:
        tp = -(-t // SSD_CHUNK) * SSD_CHUNK
        padt = lambda a: jnp.pad(a, ((0, 0), (0, tp - t), (0, 0)))
        y, ssm = ssd_mixer(padt(conv), padt(proj[..., Z_COL0:XBC_COL0]), 0, padt(dt_raw), ssm_prev,
                           dt_bias, a_log, d_skip, g_out, t)
        y = y[:, :t]
    else:
        y, ssm = ssd_mixer(conv, proj, Z_COL0, dt_raw, ssm_prev, dt_bias, a_log, d_skip, g_out, None)
    mixed = jnp.concatenate([o_attn, y], axis=-1).reshape(b * t, MOBA_W + D_INNER)
    x2 = matmul_ws(mixed, w_out, e, k=MOBA_W + D_INNER, n=d, res=x2, tk=MOBA_W)
    heads = lambda a: a.reshape(b, t, MOBA_HEADS, HEAD_DIM)
    return x2.reshape(b, t, d), heads(kn), heads(v), conv_state, ssm


def _odd_layer(x, sample, q_start, caches, e, page_table, win_k, win_v,
               g_norm, w_in, w_out, qk_g, pe, w1, b1, w2, b2):
    b, t, d = x.shape
    x2 = x.reshape(b * t, d)
    h = rmsnorm_rows(x2, g_norm)
    proj, gl = _proj(h, w_in, e, ODD_MAIN)
    proj = proj.reshape(b, t, ODD_MAIN)
    gl = gl[:, :3 * NSA_KV * NSA_HPG].reshape(b, t, NSA_KV, NSA_HPG, 3)
    col = lambda i: NSA_Q + i * KV_W
    qn = head_norm(proj, 0, NSA_KV * NSA_HPG, qk_g[0], BF16)
    skn = head_norm(proj, col(2), NSA_KV, qk_g[2], F32)
    wkn = head_norm(proj, col(4), NSA_KV, qk_g[3], F32)
    ck, cv, sv, wv = (proj[..., col(i):col(i + 1)] for i in (0, 1, 3, 5))
    if sample:
        cache_ck, cache_cv, cache_sk, cache_sv = caches
        n_pool = cache_ck.shape[1]
        flat = lambda c: c[e].reshape(n_pool * (PAGE // CMP_STRIDE), CHUNK_W)
        rows_k, rows_v, pt = flat(cache_ck), flat(cache_cv), page_table
        n_rows = page_table.shape[1] * PAGE + t
    else:
        flat = lambda a: a.reshape(b * t // CMP_STRIDE, CHUNK_W)
        rows_k, rows_v = flat(ck), flat(cv)
        npg = t // PAGE
        pt = (jnp.arange(b, dtype=jnp.int32)[:, None] * npg + jnp.arange(npg, dtype=jnp.int32)[None, :])
        n_rows = t
    kc = compress_tokens(rows_k, pt, pe[0], w1[0], b1[0], w2[0], b2[0], qk_g[1])
    vc = compress_tokens(rows_v, pt, pe[1], w1[1], b1[1], w2[1], b2[1], None)
    n_sel = -(-n_rows // SEL_BLOCK)
    ocmp, selm = nsa_compressed(qn, kc, vc, q_start, n_sel)
    if sample:
        rows_view = lambda c: c.reshape(c.shape[0], c.shape[1], c.shape[2] * NSA_KV, HEAD_DIM)
        o = nsa_sample(qn, skn, sv, wkn, wv, rows_view(cache_sk), rows_view(cache_sv), rows_view(win_k),
                       rows_view(win_v), e, page_table, selm, ocmp, gl).astype(BF16)
        wk_all = jnp.concatenate([win_k[e].reshape(b, -1, KV_W), wkn], axis=1)
        wv_all = jnp.concatenate([win_v[e].reshape(b, -1, KV_W), wv], axis=1)
    else:
        o = nsa_prompt(qn, skn, wkn, proj, col(3), col(5), selm, ocmp, gl.transpose(4, 0, 2, 1, 3))
        wk_all, wv_all = wkn, wv
    x2 = matmul_ws(o.reshape(b * t, NSA_Q), w_out, e, k=NSA_Q, n=d, res=x2)
    keep = min(WINDOW, wk_all.shape[1])
    kvh = lambda a: a.reshape(b, a.shape[1], NSA_KV, HEAD_DIM)
    return (x2.reshape(b, t, d), kvh(ck), kvh(cv), kvh(skn), kvh(sv),
            kvh(wk_all[:, wk_all.shape[1] - keep:]), kvh(wv_all[:, wv_all.shape[1] - keep:]))


def _conv_ffn(x, conv_prev, li, g_norm, w_up, conv_w, conv_b, w_down):
    b, t, d = x.shape
    x2 = x.reshape(b * t, d)
    u = matmul_ws(rmsnorm_rows(x2, g_norm), w_up, li, k=d, n=2 * D_FF).reshape(b, t, 2 * D_FF)
    hist = _history8(conv_prev, b, 2 * D_FF)
    act = dwconv(u, 0, D_FF, hist, conv_w, conv_b, True, BF16)
    state = jnp.concatenate([hist, u], axis=1)[:, -(FFN_CONV - 1):]
    x2 = matmul_ws(act.reshape(b * t, D_FF), w_down, li, k=D_FF, n=d, res=x2)
    return x2.reshape(b, t, d), state


def kernel(x_prompt, x_sample, cache_moba_k, cache_moba_v, state_ssd, state_ssd_conv, cache_nsa_cmp_k, cache_nsa_cmp_v, cache_nsa_sel_k, cache_nsa_sel_v, state_nsa_win_k, state_nsa_win_v, state_ffn_conv, page_table, norm_mix, norm_ffn, even_w_in, even_w_out, moba_qk_norm, ssd_conv_w, ssd_conv_b, ssd_dt_bias, ssd_a_log, ssd_d, ssd_norm, odd_w_in, odd_w_out, nsa_qk_norm, cmp_pe, cmp_w1, cmp_b1, cmp_w2, cmp_b2, ffn_w_up, ffn_conv_w, ffn_conv_b, ffn_w_down):
    depth = norm_mix.shape[0]
    past_len = page_table.shape[1] * PAGE
    state_keys = ("moba_k", "moba_v", "ssd", "ssd_conv", "nsa_cmp_k", "nsa_cmp_v",
                  "nsa_sel_k", "nsa_sel_v", "nsa_win_k", "nsa_win_v", "ffn_conv")
    moba_pool = lambda c: c.reshape(c.shape[0], c.shape[1], PAGE * MOBA_HEADS, HEAD_DIM)
    cache_mk, cache_mv = moba_pool(cache_moba_k), moba_pool(cache_moba_v)

    def trunk(x, q_start, sample):
        b = x.shape[0]
        new = {name: [] for name in state_keys}
        for li in range(depth):
            e = li // 2
            if li % 2 == 0:
                cp = state_ssd_conv[e] if sample else None
                sp = state_ssd[e] if sample else jnp.zeros((b, SSD_HEADS, SSD_HEAD_DIM, SSD_STATE), F32)
                x, k, v, cs, ss = _even_layer(x, sample, q_start, cache_mk, cache_mv, e, page_table, cp, sp,
                                              norm_mix[li], even_w_in, even_w_out, moba_qk_norm[e],
                                              ssd_conv_w[e], ssd_conv_b[e], ssd_dt_bias[e], ssd_a_log[e],
                                              ssd_d[e], ssd_norm[e])
                for name, val in zip(("moba_k", "moba_v", "ssd_conv", "ssd"), (k, v, cs, ss)):
                    new[name].append(val)
            else:
                caches = (cache_nsa_cmp_k, cache_nsa_cmp_v, cache_nsa_sel_k, cache_nsa_sel_v)
                outs = _odd_layer(x, sample, q_start, caches, e, page_table, state_nsa_win_k, state_nsa_win_v,
                                  norm_mix[li], odd_w_in, odd_w_out, nsa_qk_norm[e], cmp_pe[e],
                                  cmp_w1[e], cmp_b1[e], cmp_w2[e], cmp_b2[e])
                x = outs[0]
                for name, val in zip(("nsa_cmp_k", "nsa_cmp_v", "nsa_sel_k", "nsa_sel_v", "nsa_win_k", "nsa_win_v"), outs[1:]):
                    new[name].append(val)
            fp = state_ffn_conv[li] if sample else None
            x, fs = _conv_ffn(x, fp, li, norm_ffn[li], ffn_w_up, ffn_conv_w[li], ffn_conv_b[li], ffn_w_down)
            new["ffn_conv"].append(fs)
        return x, {name: jnp.stack(rows) for name, rows in new.items()}

    y_prompt, sp = trunk(x_prompt, 0, False)
    y_sample, ss = trunk(x_sample, past_len, True)
    out = [y_prompt, y_sample]
    for name in state_keys:
        out += [sp[name], ss[name]]
    return tuple(out)
```

```python
import functools

import numpy as np
import jax
import jax.numpy as jnp
from jax import lax
from jax.experimental import pallas as pl
from jax.experimental.pallas import tpu as pltpu

F32 = jnp.float32
BF16 = jnp.bfloat16

LANE = 128
VMEM_LIMIT_BYTES = 56 * 1024 * 1024

EPS = 1e-6
HEAD_DIM = 128
SCALE = HEAD_DIM ** -0.5
EXP2_SCALE = SCALE * float(np.log2(np.e))
NEG = -1e30

PAGE = 128
MOBA_HEADS = 16
MOBA_BLOCK = 256
MOBA_TOPK = 3
SSD_HEADS = 64
SSD_HEAD_DIM = 64
SSD_GROUPS = 8
SSD_STATE = 128
SSD_CONV = 4
SSD_CHUNK = 128
D_INNER = 4096
GROUP_W = D_INNER // SSD_GROUPS
HEADS_PER_GROUP = SSD_HEADS // SSD_GROUPS
NSA_KV = 4
NSA_HPG = 8
KV_W = NSA_KV * HEAD_DIM
CMP_LEN = 32
CMP_STRIDE = 16
SEL_BLOCK = 64
SEL_TOPN = 16
WINDOW = 512
FFN_CONV = 3
CHUNK_W = CMP_STRIDE * KV_W


def _params(*sem):
    return pltpu.CompilerParams(dimension_semantics=sem, vmem_limit_bytes=VMEM_LIMIT_BYTES)


def _dot(a, b):
    return jnp.dot(a, b, preferred_element_type=F32)


def _dot_nt(a, b):
    return lax.dot_general(a, b, (((1,), (1,)), ((), ())), preferred_element_type=F32)


def _dot_tn(a, b):
    return lax.dot_general(a, b, (((0,), (0,)), ((), ())), preferred_element_type=F32)


def _split3(x):
    hi = x.astype(BF16)
    r1 = x - hi.astype(F32)
    mid = r1.astype(BF16)
    lo = (r1 - mid.astype(F32)).astype(BF16)
    return hi, mid, lo


def _sigmoid(x):
    return 1.0 / (1.0 + jnp.exp(-x))


def _silu(x):
    return x * _sigmoid(x)


def _softplus(x):
    return jnp.maximum(x, 0.0) + jnp.log1p(jnp.exp(-jnp.abs(x)))


def _gelu_tanh(x):
    return 0.5 * x * (1.0 + jnp.tanh(np.sqrt(2.0 / np.pi).astype(np.float32) * (x + 0.044715 * (x * x * x))))


def _rms(x, g):
    return x * lax.rsqrt(jnp.mean(x * x, axis=-1, keepdims=True) + EPS) * g


def _iota(shape, axis):
    return lax.broadcasted_iota(jnp.int32, shape, axis)


def _rank_before(score, n_cols):
    col = _iota(score.shape, 1)
    rank = jnp.zeros(score.shape, F32)
    for c in range(n_cols):
        sc = score[:, c:c + 1]
        rank = rank + jnp.where((sc > score) | ((sc == score) & (c < col)), 1.0, 0.0)
    return rank


def _pick_col(x, c):
    return jnp.sum(jnp.where(_iota(x.shape, 1) == c, x, 0.0), axis=1, keepdims=True)


def _online_update(carry, s, msk, v_bf16):
    m, l, acc = carry
    s = jnp.where(msk, s, NEG)
    m_new = jnp.maximum(m, jnp.max(s, axis=1, keepdims=True))
    p = jnp.where(msk, jnp.exp(s - m_new), 0.0)
    alpha = jnp.exp(m - m_new)
    l = alpha * l + jnp.sum(p, axis=1, keepdims=True)
    acc = alpha * acc + _dot(p.astype(BF16), v_bf16)
    return m_new, l, acc


def _flash_init(rows, dv):
    return (jnp.full((rows, 1), NEG, F32), jnp.zeros((rows, 1), F32), jnp.zeros((rows, dv), F32))


def _flash_out(carry):
    _, l, acc = carry
    return acc / jnp.where(l > 0.0, l, 1.0)


def _rmsnorm_kernel(x_ref, g_ref, o_ref):
    o_ref[...] = _rms(x_ref[...], g_ref[...]).astype(o_ref.dtype)


def rmsnorm_rows(x2d, g):
    m, d = x2d.shape
    tm = min(m, 256)
    return pl.pallas_call(
        _rmsnorm_kernel,
        grid=(m // tm,),
        in_specs=[pl.BlockSpec((tm, d), lambda i: (i, 0)), pl.BlockSpec((1, d), lambda i: (0, 0))],
        out_specs=pl.BlockSpec((tm, d), lambda i: (i, 0)),
        out_shape=jax.ShapeDtypeStruct((m, d), BF16),
        compiler_params=_params("parallel"),
        name="rmsnorm_rows",
    )(x2d, g.reshape(1, d))


def _matmul_kernel(*refs, nk, has_res, b_t):
    if has_res:
        a_ref, b_ref, r_ref, o_ref = refs
    else:
        a_ref, b_ref, o_ref = refs
    part = (_dot_nt if b_t else _dot)(a_ref[...].astype(BF16), b_ref[...].astype(BF16))
    if nk == 1:
        o_ref[...] = part + r_ref[...] if has_res else part
        return
    k = pl.program_id(2)

    @pl.when(k == 0)
    def _():
        o_ref[...] = part + r_ref[...] if has_res else part

    @pl.when(k > 0)
    def _():
        o_ref[...] += part


def matmul(a, b, res=None, b_t=False, tm=1024, tn=512, tk=4096):
    m, k = a.shape
    n = b.shape[0] if b_t else b.shape[1]
    tm, tn, tk = min(tm, m), min(tn, n), min(tk, k)
    assert m % tm == 0 and n % tn == 0 and k % tk == 0, (a.shape, b.shape)
    nk = k // tk
    b_spec = pl.BlockSpec((tn, tk), lambda i, j, kk: (j, kk)) if b_t else pl.BlockSpec((tk, tn), lambda i, j, kk: (kk, j))
    in_specs = [pl.BlockSpec((tm, tk), lambda i, j, kk: (i, kk)), b_spec]
    args = [a, b]
    if res is not None:
        in_specs.append(pl.BlockSpec((tm, tn), lambda i, j, kk: (i, j)))
        args.append(res)
    return pl.pallas_call(
        functools.partial(_matmul_kernel, nk=nk, has_res=res is not None, b_t=b_t),
        grid=(m // tm, n // tn, nk),
        in_specs=in_specs,
        out_specs=pl.BlockSpec((tm, tn), lambda i, j, kk: (i, j)),
        out_shape=jax.ShapeDtypeStruct((m, n), F32),
        compiler_params=_params("parallel", "parallel", "arbitrary"),
        name="matmul",
    )(*args)


def _matmul_ws_kernel(*refs, nk, has_res, has_small, w_t):
    refs = list(refs)
    a_ref, w_ref = refs.pop(0), refs.pop(0)
    r_ref = refs.pop(0) if has_res else None
    a2_ref = refs.pop(0) if has_small else None
    r2_ref = refs.pop(0) if has_small and has_res else None
    o_ref = refs.pop(0)
    o2_ref = refs.pop(0) if has_small else None
    wbf_ref, = refs
    i = pl.program_id(1)
    kk = pl.program_id(2)
    mm = _dot_nt if w_t else _dot

    def accumulate(out_ref, part, res_ref):
        if nk == 1:
            out_ref[...] = part + res_ref[...] if has_res else part
            return

        @pl.when(kk == 0)
        def _():
            out_ref[...] = part + res_ref[...] if has_res else part

        @pl.when(kk > 0)
        def _():
            out_ref[...] += part

    @pl.when(i == 0)
    def _():
        wbf_ref[kk] = w_ref[...].astype(BF16)
        if has_small:
            accumulate(o2_ref, mm(a2_ref[...], wbf_ref[kk]), r2_ref)

    accumulate(o_ref, mm(a_ref[...], wbf_ref[kk]), r_ref)


def matmul_ws(a, w, layer, *, k, n, res=None, small=None, small_res=None, out_cols=None, w_t=False,
              tm=1024, tn=512, tk=4096):
    m = a.shape[0]
    tm, tn, tk = min(tm, m), min(tn, n), min(tk, k)
    assert m % tm == 0 and n % tn == 0 and k % tk == 0, (a.shape, w.shape, k, n)
    assert (res is None) == (small_res is None) or small is None
    nk = k // tk
    k_idx = (lambda i, kk: 0) if nk == 1 else (lambda i, kk: jnp.where(i == 0, kk, nk - 1))
    if w_t:
        w_spec = pl.BlockSpec((None, tn, tk), lambda j, i, kk: (layer, j, k_idx(i, kk)))
    else:
        w_spec = pl.BlockSpec((None, tk, tn), lambda j, i, kk: (layer, k_idx(i, kk), j))
    col = (lambda j: j) if out_cols is None else out_cols
    o_spec = pl.BlockSpec((tm, tn), lambda j, i, kk: (i, col(j)))
    in_specs = [pl.BlockSpec((tm, tk), lambda j, i, kk: (i, kk)), w_spec]
    args = [a, w]
    out_specs, out_shape = [o_spec], [jax.ShapeDtypeStruct((m, n), F32)]
    if res is not None:
        in_specs.append(o_spec)
        args.append(res)
    if small is not None:
        m2 = small.shape[0]
        o2_spec = pl.BlockSpec((m2, tn), lambda j, i, kk: (0, col(j)))
        in_specs.append(pl.BlockSpec((m2, tk), lambda j, i, kk: (0, k_idx(i, kk))))
        args.append(small)
        if res is not None:
            in_specs.append(o2_spec)
            args.append(small_res)
        out_specs.append(o2_spec)
        out_shape.append(jax.ShapeDtypeStruct((m2, n), F32))
    outs = pl.pallas_call(
        functools.partial(_matmul_ws_kernel, nk=nk, has_res=res is not None, has_small=small is not None, w_t=w_t),
        grid=(n // tn, m // tm, nk),
        in_specs=in_specs,
        out_specs=out_specs,
        out_shape=out_shape,
        scratch_shapes=[pltpu.VMEM((nk, tn, tk) if w_t else (nk, tk, tn), BF16)],
        compiler_params=_params("arbitrary", "arbitrary", "arbitrary"),
        name="matmul_ws",
    )(*args)
    return outs[0] if small is None else tuple(outs)


def _head_norm_kernel(x_ref, g_ref, o_ref, *, heads):
    g = g_ref[...]
    for h in range(heads):
        sl = slice(h * HEAD_DIM, (h + 1) * HEAD_DIM)
        o_ref[:, sl] = _rms(x_ref[:, sl], g).astype(o_ref.dtype)


def head_norm(x, col0, n_heads, g, out_dtype):
    b, t, _ = x.shape
    hb = min(n_heads, 4)
    w = hb * HEAD_DIM
    assert col0 % w == 0 and n_heads % hb == 0
    tq = min(t, 512)
    return pl.pallas_call(
        functools.partial(_head_norm_kernel, heads=hb),
        grid=(b, t // tq, n_heads // hb),
        in_specs=[pl.BlockSpec((None, tq, w), lambda bi, i, j: (bi, i, col0 // w + j)),
                  pl.BlockSpec((1, HEAD_DIM), lambda bi, i, j: (0, 0))],
        out_specs=pl.BlockSpec((None, tq, w), lambda bi, i, j: (bi, i, j)),
        out_shape=jax.ShapeDtypeStruct((b, t, n_heads * HEAD_DIM), out_dtype),
        compiler_params=_params("parallel", "parallel", "parallel"),
        name="head_norm",
    )(x, g.reshape(1, HEAD_DIM))


def _dwconv_kernel(*refs, width, t, glu):
    if glu:
        xa_ref, xg_ref, pa_ref, pg_ref, wa_ref, wg_ref, ba_ref, bg_ref, o_ref, sa_ref, sg_ref = refs
        pairs = ((xa_ref, pa_ref, sa_ref), (xg_ref, pg_ref, sg_ref))
    else:
        xa_ref, pa_ref, wa_ref, ba_ref, o_ref, sa_ref = refs
        pairs = ((xa_ref, pa_ref, sa_ref),)
    for x_ref, p_ref, s_ref in pairs:
        s_ref[0:8, :] = p_ref[...]
        s_ref[8:8 + t, :] = x_ref[...]
    tc = min(t, 256)

    def conv(s_ref, w_ref, b_ref, r0):
        acc = b_ref[...]
        for i in range(width):
            acc = acc + w_ref[i:i + 1, :] * s_ref[pl.ds(8 - (width - 1) + i + r0, tc), :]
        return acc

    for c in range(t // tc):
        r0 = c * tc
        a = conv(sa_ref, wa_ref, ba_ref, r0)
        if glu:
            o_ref[r0:r0 + tc, :] = (a * _silu(conv(sg_ref, wg_ref, bg_ref, r0))).astype(o_ref.dtype)
        else:
            o_ref[r0:r0 + tc, :] = _silu(a).astype(o_ref.dtype)


def dwconv(x, col0, c_out, prev8, w, bias, glu, out_dtype):
    b, t, _ = x.shape
    width = w.shape[0]
    tc = min(256 if t > 64 else 2048, c_out)
    assert col0 % tc == 0 and c_out % tc == 0
    nj = c_out // tc
    x_spec = lambda off: pl.BlockSpec((None, t, tc), lambda bi, j: (bi, 0, off + j))
    p_spec = lambda off: pl.BlockSpec((None, 8, tc), lambda bi, j: (bi, 0, off + j))
    w_spec = lambda off: pl.BlockSpec((width, tc), lambda bi, j: (0, off + j))
    b_spec = lambda off: pl.BlockSpec((1, tc), lambda bi, j: (0, off + j))
    bias2 = bias.reshape(1, -1)
    if glu:
        in_specs = [x_spec(col0 // tc), x_spec(col0 // tc + nj), p_spec(0), p_spec(nj),
                    w_spec(0), w_spec(nj), b_spec(0), b_spec(nj)]
        args = (x, x, prev8, prev8, w, w, bias2, bias2)
        scratch = [pltpu.VMEM((8 + t, tc), F32), pltpu.VMEM((8 + t, tc), F32)]
    else:
        in_specs = [x_spec(col0 // tc), p_spec(0), w_spec(0), b_spec(0)]
        args = (x, prev8, w, bias2)
        scratch = [pltpu.VMEM((8 + t, tc), F32)]
    return pl.pallas_call(
        functools.partial(_dwconv_kernel, width=width, t=t, glu=glu),
        grid=(b, nj),
        in_specs=in_specs,
        out_specs=pl.BlockSpec((None, t, tc), lambda bi, j: (bi, 0, j)),
        out_shape=jax.ShapeDtypeStruct((b, t, c_out), out_dtype),
        scratch_shapes=scratch,
        compiler_params=_params("parallel", "parallel"),
        name="dwconv_glu" if glu else "dwconv_silu",
    )(*args)


def _history8(state, b, c):
    if state is None:
        return jnp.zeros((b, 8, c), F32)
    return jnp.concatenate([jnp.zeros((b, 8 - state.shape[1], c), F32), state], axis=1)


def _moba_tile(i, nb, q_ref, o_ref, kmean_ref, kbf_ref, vbf_ref):
    blk, half = MOBA_BLOCK, MOBA_BLOCK // 2
    fold = lambda a, op: functools.reduce(op, [a[:, c * LANE:(c + 1) * LANE] for c in range(a.shape[1] // LANE)])
    k_blk = lambda n: kbf_ref[n * blk:(n + 1) * blk, :]
    v_blk = lambda n: vbf_ref[n * blk:(n + 1) * blk, :]
    spread = jnp.where(_iota((LANE, max(i, 1) * LANE), 0) == _iota((LANE, max(i, 1) * LANE), 1) // LANE, 1.0, 0.0).astype(BF16)
    for hq in range(2):
        q = q_ref[hq * half:(hq + 1) * half, :]
        masks = []
        if i > 0:
            gate = _dot_nt(q, kmean_ref[...].astype(BF16))
            valid = _iota(gate.shape, 1) < i
            gate = jnp.where(valid, gate, -jnp.inf)
            sel = jnp.where(valid & (_rank_before(gate, i) < min(MOBA_TOPK, nb)), 1.0, 0.0).astype(BF16)
            selb = _dot(sel, spread)
            masks = [jnp.concatenate([selb[:, n * LANE:(n + 1) * LANE]] * (blk // LANE), axis=1) > 0.5 for n in range(i)]
        causal = _iota((half, blk), 1) <= _iota((half, blk), 0) + hq * half
        masks.append(causal)
        scores = lambda n: jnp.where(masks[n], _dot_nt(q, k_blk(n)), NEG)
        part = functools.reduce(jnp.maximum, [fold(scores(n), jnp.maximum) for n in range(i + 1)])
        m = jnp.max(part, axis=1, keepdims=True)
        l_part, acc = None, None
        for n in range(i + 1):
            p = jnp.exp2((scores(n) - m) * EXP2_SCALE)
            pv = _dot(p.astype(BF16), v_blk(n))
            l_part = fold(p, jnp.add) if n == 0 else l_part + fold(p, jnp.add)
            acc = pv if n == 0 else acc + pv
        l = jnp.sum(l_part, axis=1, keepdims=True)
        o_ref[hq * half:(hq + 1) * half, :] = (acc / l).astype(o_ref.dtype)


def _moba_tiles_kernel(q_ref, k_ref, v_ref, o_ref, kmean_ref, kbf_ref, vbf_ref, *, nb):
    blk = MOBA_BLOCK
    i = pl.program_id(2)

    @pl.when(i == 0)
    def _():
        kmean_ref[...] = jnp.zeros_like(kmean_ref)
        for n in range(nb):
            kmean_ref[n:n + 1, :] = jnp.mean(k_ref[n * blk:(n + 1) * blk, :], axis=0, keepdims=True)
        kbf_ref[...] = k_ref[...].astype(BF16)
        vbf_ref[...] = v_ref[...].astype(BF16)

    for c in range(nb):
        @pl.when(i == c)
        def _(c=c):
            _moba_tile(c, nb, q_ref, o_ref, kmean_ref, kbf_ref, vbf_ref)


def moba_prompt(qn, kn, proj, v_col0):
    b, t, w = qn.shape
    nh = w // HEAD_DIM
    nb = t // MOBA_BLOCK
    assert t % MOBA_BLOCK == 0 and nb <= LANE
    return pl.pallas_call(
        functools.partial(_moba_tiles_kernel, nb=nb),
        grid=(b, nh, nb),
        in_specs=[pl.BlockSpec((None, MOBA_BLOCK, HEAD_DIM), lambda bi, h, i: (bi, i, h)),
                  pl.BlockSpec((None, t, HEAD_DIM), lambda bi, h, i: (bi, 0, h)),
                  pl.BlockSpec((None, t, HEAD_DIM), lambda bi, h, i: (bi, 0, v_col0 // HEAD_DIM + h))],
        out_specs=pl.BlockSpec((None, MOBA_BLOCK, HEAD_DIM), lambda bi, h, i: (bi, i, h)),
        out_shape=jax.ShapeDtypeStruct((b, t, w), BF16),
        scratch_shapes=[pltpu.VMEM((LANE, HEAD_DIM), F32), pltpu.VMEM((t, HEAD_DIM), BF16), pltpu.VMEM((t, HEAD_DIM), BF16)],
        compiler_params=_params("parallel", "parallel", "arbitrary"),
        name="moba_prompt",
    )(qn, kn, proj)


def _moba_dec_block_kernel(pt_ref, q_ref, ka_ref, kb_ref, va_ref, vb_ref,
                           acc_ref, g_ref, m_ref, l_ref, *, nh, tq):
    n = pl.program_id(1)
    q = q_ref[...]
    kb = jnp.concatenate([ka_ref[...], kb_ref[...]], axis=0).astype(BF16)
    vb = jnp.concatenate([va_ref[...], vb_ref[...]], axis=0).astype(BF16)
    s_raw = _dot_nt(q, kb)
    same_head = _iota(s_raw.shape, 1) % nh == _iota(s_raw.shape, 0) // tq
    gate = jnp.sum(jnp.where(same_head, s_raw, 0.0), axis=1, keepdims=True) * (1.0 / MOBA_BLOCK)
    s = jnp.where(same_head, s_raw * SCALE, NEG)
    m = jnp.max(s, axis=1, keepdims=True)
    p = jnp.exp(s - m)
    l = jnp.sum(p, axis=1, keepdims=True)
    acc_ref[...] = _dot(p.astype(BF16), vb)

    @pl.when(n == 0)
    def _():
        g_ref[...] = jnp.zeros_like(g_ref)
        m_ref[...] = jnp.zeros_like(m_ref)
        l_ref[...] = jnp.zeros_like(l_ref)

    lane = _iota(g_ref.shape, 1)
    g_ref[...] = jnp.where(lane == n, gate, g_ref[...])
    m_ref[...] = jnp.where(lane == n, m, m_ref[...])
    l_ref[...] = jnp.where(lane == n, l, l_ref[...])


def _moba_dec_combine_kernel(g_ref, m_ref, l_ref, acc_ref, q_ref, kn_ref, vn_ref, o_ref, *, nbk, nh, tq):
    gate = g_ref[...]
    lane = _iota(gate.shape, 1)
    row = _iota(gate.shape, 0)
    valid = lane < nbk
    gate = jnp.where(valid, gate, -jnp.inf)
    sel = valid & (_rank_before(gate, nbk) < min(MOBA_TOPK, nbk + 1))
    q = q_ref[...]
    s_own = _dot_nt(q, kn_ref[...].astype(BF16)) * SCALE
    own = (lane % nh == row // tq) & (lane // nh <= row % tq)
    m_blk = m_ref[...]
    m_tot = jnp.maximum(jnp.max(jnp.where(sel, m_blk, NEG), axis=1, keepdims=True),
                        jnp.max(jnp.where(own, s_own, NEG), axis=1, keepdims=True))
    wgt = jnp.where(sel, jnp.exp(m_blk - m_tot), 0.0)
    p_own = jnp.where(own, jnp.exp(s_own - m_tot), 0.0)
    l_tot = jnp.sum(wgt * l_ref[...], axis=1, keepdims=True) + jnp.sum(p_own, axis=1, keepdims=True)
    acc = _dot(p_own.astype(BF16), vn_ref[...].astype(BF16))
    for n in range(nbk):
        acc = acc + wgt[:, n:n + 1] * acc_ref[n]
    o_ref[...] = acc / jnp.where(l_tot > 0.0, l_tot, 1.0)


def moba_sample(qn, kn_new, v_new, cache_k, cache_v, layer, page_table):
    b, tq, w = qn.shape
    nh = w // HEAD_DIM
    rows = nh * tq
    npages = page_table.shape[1]
    ppb = MOBA_BLOCK // PAGE
    nbk = npages // ppb
    assert rows == LANE and ppb == 2 and nbk <= LANE
    qrows = qn.reshape(b, tq, nh, HEAD_DIM).transpose(0, 2, 1, 3).reshape(b, rows, HEAD_DIM)
    kn_rows = kn_new.reshape(b, tq * nh, HEAD_DIM)
    vn_rows = v_new.reshape(b, tq * nh, HEAD_DIM)

    page = lambda j: pl.BlockSpec((None, None, PAGE * nh, HEAD_DIM), lambda bi, n, pt: (layer, pt[bi, ppb * n + j], 0, 0))
    stat = pl.BlockSpec((None, rows, LANE), lambda bi, n, pt: (bi, 0, 0))
    acc, g, m, l = pl.pallas_call(
        functools.partial(_moba_dec_block_kernel, nh=nh, tq=tq),
        grid_spec=pltpu.PrefetchScalarGridSpec(
            num_scalar_prefetch=1,
            grid=(b, nbk),
            in_specs=[pl.BlockSpec((None, rows, HEAD_DIM), lambda bi, n, pt: (bi, 0, 0)),
                      page(0), page(1), page(0), page(1)],
            out_specs=[pl.BlockSpec((None, None, rows, HEAD_DIM), lambda bi, n, pt: (bi, n, 0, 0)), stat, stat, stat],
        ),
        out_shape=[jax.ShapeDtypeStruct((b, nbk, rows, HEAD_DIM), F32)] + [jax.ShapeDtypeStruct((b, rows, LANE), F32)] * 3,
        compiler_params=_params("parallel", "arbitrary"),
        name="moba_sample_blocks",
    )(page_table, qrows, cache_k, cache_k, cache_v, cache_v)

    per_b = lambda *shape: pl.BlockSpec((None,) + shape, lambda bi: (bi,) + (0,) * len(shape))
    o = pl.pallas_call(
        functools.partial(_moba_dec_combine_kernel, nbk=nbk, nh=nh, tq=tq),
        grid=(b,),
        in_specs=[per_b(rows, LANE), per_b(rows, LANE), per_b(rows, LANE), per_b(nbk, rows, HEAD_DIM),
                  per_b(rows, HEAD_DIM), per_b(tq * nh, HEAD_DIM), per_b(tq * nh, HEAD_DIM)],
        out_specs=per_b(rows, HEAD_DIM),
        out_shape=jax.ShapeDtypeStruct((b, rows, HEAD_DIM), F32),
        compiler_params=_params("parallel"),
        name="moba_sample_combine",
    )(g, m, l, acc, qrows, kn_rows, vn_rows)
    return o.reshape(b, nh, tq, HEAD_DIM).transpose(0, 2, 1, 3).reshape(b, tq, w)


def _ssd_kernel(x_ref, bm_ref, cm_ref, z_ref, dtc_ref, dtr_ref, bias_c_ref, bias_r_ref, alog_c_ref, alog_r_ref,
                dskip_ref, gout_ref, s0_ref, pre_ref, y_ref, sfin_ref, state_ref, *, q, t_valid):
    c = pl.program_id(1)
    nc = pl.num_programs(1)
    pre_w = pre_ref.shape[1]
    y_ref[:, :pre_w] = pre_ref[...]

    @pl.when(c == 0)
    def _():
        state_ref[...] = s0_ref[...]

    dtc = _softplus(dtc_ref[...] + bias_c_ref[...])
    dtr = _softplus(dtr_ref[...] + bias_r_ref[...])
    if t_valid is not None:
        dtc = jnp.where(c * q + _iota(dtc.shape, 0) < t_valid, dtc, 0.0)
        dtr = jnp.where(c * q + _iota(dtr.shape, 1) < t_valid, dtr, 0.0)
    da_c = dtc * -jnp.exp(alog_c_ref[...])
    da_r = dtr * -jnp.exp(alog_r_ref[...])
    tri = _iota((q, q), 0) >= _iota((q, q), 1)
    tri_bf = jnp.where(tri, 1.0, 0.0).astype(BF16)
    cum_c = sum(_dot(tri_bf, part) for part in _split3(da_c))
    cum_r = sum(_dot_nt(part, tri_bf) for part in _split3(da_r))
    lane = _iota((q, LANE), 1)
    lo_half = lane < SSD_HEAD_DIM
    row_lo = _iota((LANE, 1), 0) < SSD_HEAD_DIM
    pairs = HEADS_PER_GROUP // 2

    for g in range(SSD_GROUPS):
        xg = x_ref[:, g * GROUP_W:(g + 1) * GROUP_W]
        bm = bm_ref[:, g * SSD_STATE:(g + 1) * SSD_STATE].astype(BF16)
        cm = cm_ref[:, g * SSD_STATE:(g + 1) * SSD_STATE].astype(BF16)
        cb = _dot_nt(cm, bm)
        y_parts = []
        for pr in range(pairs):
            x_pair = xg[:, pr * LANE:(pr + 1) * LANE]
            x_bf = x_pair.astype(BF16)
            ys, cols, lasts = [], [], []
            for h in (g * HEADS_PER_GROUP + 2 * pr, g * HEADS_PER_GROUP + 2 * pr + 1):
                col = cum_c[:, h:h + 1]
                seg = col - cum_r[h:h + 1, :]
                decay = jnp.where(tri, jnp.exp(jnp.where(tri, seg, 0.0)), 0.0)
                wgt = cb * decay * dtr[h:h + 1, :]
                ys.append(_dot(wgt.astype(BF16), x_bf))
                cols.append(col)
                lasts.append(cum_c[q - 1:q, h:h + 1])
            sidx = g * pairs + pr
            state = state_ref[sidx]
            y_pair = jnp.where(lo_half, ys[0], ys[1])
            carry_in = _dot_nt(cm, state.astype(BF16))
            y_pair = y_pair + carry_in * jnp.exp(jnp.where(lo_half, cols[0], cols[1]))
            y_parts.append(y_pair)
            tail = jnp.where(lo_half,
                             jnp.exp(lasts[0] - cols[0]) * dtc[:, 2 * sidx:2 * sidx + 1],
                             jnp.exp(lasts[1] - cols[1]) * dtc[:, 2 * sidx + 1:2 * sidx + 2])
            upd = _dot_tn((x_pair * tail).astype(BF16), bm)
            state_ref[sidx] = state * jnp.where(row_lo, jnp.exp(lasts[0]), jnp.exp(lasts[1])) + upd
        sl = slice(g * GROUP_W, (g + 1) * GROUP_W)
        yg = jnp.concatenate(y_parts, axis=1) + xg * dskip_ref[:, sl]
        yg = yg * _silu(z_ref[:, sl])
        y_ref[:, pre_w + g * GROUP_W:pre_w + (g + 1) * GROUP_W] = _rms(yg, gout_ref[:, sl]).astype(y_ref.dtype)

    @pl.when(c == nc - 1)
    def _():
        sfin_ref[...] = state_ref[...]


def ssd_mixer(conv, z_src, z_col0, dt_raw, s0, dt_bias, a_log, d_skip, g_out, t_valid, prefix):
    b, t, _ = conv.shape
    pre_w = prefix.shape[-1]
    q = SSD_CHUNK
    assert t % q == 0
    nc = t // q
    dtr = dt_raw.transpose(0, 2, 1)
    pad_h = LANE - SSD_HEADS
    bias_c = jnp.pad(dt_bias, (0, pad_h)).reshape(1, LANE)
    bias_r = bias_c.reshape(LANE, 1)
    alog_c = jnp.pad(a_log, (0, pad_h)).reshape(1, LANE)
    alog_r = alog_c.reshape(LANE, 1)
    dskip = jnp.repeat(d_skip, SSD_HEAD_DIM).reshape(1, D_INNER)
    npair = SSD_HEADS // 2
    s0p = s0.reshape(b, npair, 2 * SSD_HEAD_DIM, SSD_STATE)
    nbc = D_INNER // (SSD_GROUPS * SSD_STATE)
    const = lambda shape: pl.BlockSpec(shape, lambda bi, c: (0,) * len(shape))
    y, sfin = pl.pallas_call(
        functools.partial(_ssd_kernel, q=q, t_valid=t_valid),
        grid=(b, nc),
        in_specs=[pl.BlockSpec((None, q, D_INNER), lambda bi, c: (bi, c, 0)),
                  pl.BlockSpec((None, q, SSD_GROUPS * SSD_STATE), lambda bi, c: (bi, c, nbc)),
                  pl.BlockSpec((None, q, SSD_GROUPS * SSD_STATE), lambda bi, c: (bi, c, nbc + 1)),
                  pl.BlockSpec((None, q, D_INNER), lambda bi, c: (bi, c, z_col0 // D_INNER)),
                  pl.BlockSpec((None, q, LANE), lambda bi, c: (bi, c, 0)),
                  pl.BlockSpec((None, LANE, q), lambda bi, c: (bi, 0, c)),
                  const((1, LANE)), const((LANE, 1)), const((1, LANE)), const((LANE, 1)),
                  const((1, D_INNER)), const((1, D_INNER)),
                  pl.BlockSpec((None, npair, LANE, SSD_STATE), lambda bi, c: (bi, 0, 0, 0)),
                  pl.BlockSpec((None, q, pre_w), lambda bi, c: (bi, c, 0))],
        out_specs=[pl.BlockSpec((None, q, pre_w + D_INNER), lambda bi, c: (bi, c, 0)),
                   pl.BlockSpec((None, npair, LANE, SSD_STATE), lambda bi, c: (bi, 0, 0, 0))],
        out_shape=[jax.ShapeDtypeStruct((b, t, pre_w + D_INNER), BF16),
                   jax.ShapeDtypeStruct((b, npair, LANE, SSD_STATE), F32)],
        scratch_shapes=[pltpu.VMEM((npair, LANE, SSD_STATE), F32)],
        compiler_params=_params("parallel", "arbitrary"),
        name="ssd_scan",
    )(conv, conv, conv, z_src, dt_raw, dtr, bias_c, bias_r, alog_c, alog_r, dskip, g_out.reshape(1, D_INNER), s0p,
      prefix)
    return y, sfin.reshape(b, SSD_HEADS, SSD_HEAD_DIM, SSD_STATE)


def _compress_kernel(pt_ref, a_ref, b_ref, cpe_a_ref, cpe_b_ref, b1_ref, w2_ref, b2_ref, g_ref, o_ref,
                     ha_ref, hb_ref, *, npg, norm):
    bi = pl.program_id(0)
    ncp = npg * 8

    def gather(p, _):
        src = pl.ds(pl.multiple_of(pt_ref[bi, p] * 8, 8), 8)
        dst = pl.ds(pl.multiple_of(p * 8, 8), 8)
        ha_ref[dst, :] = a_ref[src, :]
        hb_ref[dst, :] = b_ref[src, :]
        return 0

    lax.fori_loop(0, npg, gather, 0)
    hb_ref[ncp:ncp + 8, :] = jnp.zeros((8, HEAD_DIM), F32)
    hid = ha_ref[...] + hb_ref[pl.ds(1, ncp), :] + (cpe_a_ref[0:1, :] + cpe_b_ref[1:2, :] + b1_ref[...])
    out = _dot(_gelu_tanh(hid).astype(BF16), w2_ref[...]) + b2_ref[...]
    if norm:
        out = _rms(out, g_ref[...])
    o_ref[...] = out


def compress_tokens(rows_flat, page_table, pe, w1, b1, w2, b2, g_norm):
    nck = rows_flat.shape[0]
    b, npg = page_table.shape
    ncp = npg * 8
    half = CMP_STRIDE * HEAD_DIM
    eye = jnp.eye(NSA_KV, dtype=F32)
    blockdiag = lambda wh: jnp.einsum("ldj,gh->lgdhj", wh.reshape(CMP_STRIDE, HEAD_DIM, HEAD_DIM), eye).reshape(CHUNK_W, KV_W)
    w_big = jnp.concatenate([blockdiag(w1[:half]), blockdiag(w1[half:])], axis=1).astype(BF16)
    ab = matmul(rows_flat, w_big, tm=256, tk=2048)
    pe_rows = jnp.broadcast_to(pe.reshape(2, CMP_STRIDE, 1, HEAD_DIM), (2, CMP_STRIDE, NSA_KV, HEAD_DIM)).reshape(2, CHUNK_W)
    cpe = matmul(jnp.pad(pe_rows, ((0, 6), (0, 0))), w_big, tk=2048)
    col = lambda off: (lambda bi, g, pt: (0, off + g))
    return pl.pallas_call(
        functools.partial(_compress_kernel, npg=npg, norm=g_norm is not None),
        grid_spec=pltpu.PrefetchScalarGridSpec(
            num_scalar_prefetch=1,
            grid=(b, NSA_KV),
            in_specs=[pl.BlockSpec((nck, HEAD_DIM), col(0)), pl.BlockSpec((nck, HEAD_DIM), col(NSA_KV)),
                      pl.BlockSpec((8, HEAD_DIM), col(0)), pl.BlockSpec((8, HEAD_DIM), col(NSA_KV)),
                      pl.BlockSpec((1, HEAD_DIM), lambda bi, g, pt: (0, 0)),
                      pl.BlockSpec((HEAD_DIM, HEAD_DIM), lambda bi, g, pt: (0, 0)),
                      pl.BlockSpec((1, HEAD_DIM), lambda bi, g, pt: (0, 0)),
                      pl.BlockSpec((1, HEAD_DIM), lambda bi, g, pt: (0, 0))],
            out_specs=pl.BlockSpec((None, ncp, HEAD_DIM), lambda bi, g, pt: (bi, 0, g)),
            scratch_shapes=[pltpu.VMEM((ncp, HEAD_DIM), F32), pltpu.VMEM((ncp + 8, HEAD_DIM), F32)],
        ),
        out_shape=jax.ShapeDtypeStruct((b, ncp, KV_W), F32),
        compiler_params=_params("parallel", "parallel"),
        name="nsa_compress",
    )(page_table, ab, ab, cpe, cpe, b1.reshape(1, -1), w2.astype(BF16), b2.reshape(1, -1),
      (g_norm if g_norm is not None else jnp.ones((HEAD_DIM,), F32)).reshape(1, -1))


def _nsa_cmp_kernel(q_ref, kc_ref, vc_ref, o_ref, sel_ref, *, tq, q_start, n_sel, nsp):
    i = pl.program_id(2)
    kc = kc_ref[...].astype(BF16)
    vc = vc_ref[...].astype(BF16)
    ncp = kc.shape[0]
    qpos = q_start + i * tq + _iota((tq, 1), 0)
    end = _iota((tq, ncp), 1) * CMP_STRIDE + (CMP_LEN - 1)
    msk = end <= qpos
    imp = jnp.zeros((tq, ncp), F32)
    for r in range(NSA_HPG):
        sl = slice(r * HEAD_DIM, (r + 1) * HEAD_DIM)
        s = jnp.where(msk, _dot_nt(q_ref[:, sl], kc) * SCALE, NEG)
        p = jnp.where(msk, jnp.exp(s - jnp.max(s, axis=1, keepdims=True)), 0.0)
        d = jnp.sum(p, axis=1, keepdims=True)
        p = p / jnp.where(d > 0.0, d, 1.0)
        o_ref[:, sl] = _dot(p.astype(BF16), vc)
        imp = imp + p
    cn = _iota((ncp, nsp), 0)
    sj = _iota((ncp, nsp), 1)
    overlap = jnp.where((cn >= 4 * sj - 1) & (cn <= 4 * sj + 3), 1.0, 0.0).astype(BF16)
    score = sum(_dot(part, overlap) for part in _split3(imp))
    j = _iota((tq, nsp), 1)
    qblk = qpos // SEL_BLOCK
    valid = j <= qblk
    forced = (j == 0) | (j == qblk) | (j == qblk - 1)
    score = jnp.where(valid, jnp.where(forced, jnp.inf, score), -jnp.inf)
    sel_ref[...] = jnp.where(valid & (_rank_before(score, n_sel) < min(SEL_TOPN, n_sel)), 1.0, 0.0)


def nsa_compressed(qn, kc, vc, q_start, n_sel):
    b, t, w = qn.shape
    ncp = kc.shape[1]
    gw = NSA_HPG * HEAD_DIM
    tq = min(t, 256)
    nsp = -(-n_sel // LANE) * LANE
    return pl.pallas_call(
        functools.partial(_nsa_cmp_kernel, tq=tq, q_start=q_start, n_sel=n_sel, nsp=nsp),
        grid=(b, NSA_KV, t // tq),
        in_specs=[pl.BlockSpec((None, tq, gw), lambda bi, g, i: (bi, i, g)),
                  pl.BlockSpec((None, ncp, HEAD_DIM), lambda bi, g, i: (bi, 0, g)),
                  pl.BlockSpec((None, ncp, HEAD_DIM), lambda bi, g, i: (bi, 0, g))],
        out_specs=[pl.BlockSpec((None, tq, gw), lambda bi, g, i: (bi, i, g)),
                   pl.BlockSpec((None, None, tq, nsp), lambda bi, g, i: (bi, g, i, 0))],
        out_shape=[jax.ShapeDtypeStruct((b, t, w), F32), jax.ShapeDtypeStruct((b, NSA_KV, t, nsp), F32)],
        compiler_params=_params("parallel", "parallel", "parallel"),
        name="nsa_compressed",
    )(qn, kc, vc)


def _nsa_prompt_kernel(q_ref, sk_ref, sv_ref, wk_ref, wv_ref, sel_ref, ocmp_ref, gl_ref, o_ref,
                       m_ref, part_ref, acc_ref, s_ref, *, tq, tk):
    i = pl.program_id(2)
    q0 = i * tq
    tpos = q0 + _iota((tq, tk), 0)
    sel = sel_ref[...].astype(BF16)
    nsp = sel.shape[1]

    def sel_mask(k0):
        kpos = k0 + _iota((tq, tk), 1)
        expand = jnp.where(_iota((nsp, tk), 0) == (k0 + _iota((nsp, tk), 1)) // SEL_BLOCK, 1.0, 0.0).astype(BF16)
        return (_dot(sel, expand) > 0.5) & (kpos <= tpos)

    def win_mask(k0):
        kpos = k0 + _iota((tq, tk), 1)
        return (kpos <= tpos) & (kpos > tpos - WINDOW)

    n_hi = (q0 + tq - 1) // tk + 1
    branches = ((sk_ref, sv_ref, sel_mask, 0), (wk_ref, wv_ref, win_mask, jnp.maximum(q0 - (WINDOW - 1), 0) // tk))
    fold = lambda a, op: functools.reduce(op, [a[:, c * LANE:(c + 1) * LANE] for c in range(tk // LANE)])
    for br, (k_ref, v_ref, mask_fn, n_lo) in enumerate(branches):
        base = br * NSA_HPG
        for r in range(NSA_HPG):
            part_ref[base + r] = jnp.full((tq, LANE), NEG, F32)
            acc_ref[base + r] = jnp.zeros((tq, HEAD_DIM), F32)

        def max_sweep(n, _, base=base, k_ref=k_ref, mask_fn=mask_fn, n_lo=n_lo):
            k0 = pl.multiple_of(n * tk, tk)
            kb, msk = k_ref[pl.ds(k0, tk), :].astype(BF16), mask_fn(k0)
            for r in range(NSA_HPG):
                s = jnp.where(msk, _dot_nt(q_ref[:, r * HEAD_DIM:(r + 1) * HEAD_DIM], kb), NEG)
                s_ref[n - n_lo, r] = s
                part_ref[base + r] = jnp.maximum(part_ref[base + r], fold(s, jnp.maximum))
            return 0

        lax.fori_loop(n_lo, n_hi, max_sweep, 0)
        for r in range(NSA_HPG):
            m_ref[base + r] = jnp.max(part_ref[base + r], axis=1, keepdims=True)
            part_ref[base + r] = jnp.zeros((tq, LANE), F32)

        def exp_sweep(n, _, base=base, v_ref=v_ref, n_lo=n_lo):
            vb = v_ref[pl.ds(pl.multiple_of(n * tk, tk), tk), :].astype(BF16)
            for r in range(NSA_HPG):
                p = jnp.exp2((s_ref[n - n_lo, r] - m_ref[base + r]) * EXP2_SCALE)
                part_ref[base + r] += fold(p, jnp.add)
                acc_ref[base + r] += _dot(p.astype(BF16), vb)
            return 0

        lax.fori_loop(n_lo, n_hi, exp_sweep, 0)

    gates = [_sigmoid(gl_ref[c]) for c in range(3)]
    for r in range(NSA_HPG):
        sl = slice(r * HEAD_DIM, (r + 1) * HEAD_DIM)
        o_sel = acc_ref[r] / jnp.sum(part_ref[r], axis=1, keepdims=True)
        o_win = acc_ref[NSA_HPG + r] / jnp.sum(part_ref[NSA_HPG + r], axis=1, keepdims=True)
        o = gates[0][:, r:r + 1] * ocmp_ref[:, sl] + gates[1][:, r:r + 1] * o_sel + gates[2][:, r:r + 1] * o_win
        o_ref[:, sl] = o.astype(o_ref.dtype)


def nsa_prompt(qn, skn, wkn, proj, sv_col0, wv_col0, selm, ocmp, gate_logits, tq=256, tk=256):
    b, t, w = qn.shape
    gw = NSA_HPG * HEAD_DIM
    tq, tk = min(t, tq), min(t, tk)
    nsp = selm.shape[-1]
    kv = lambda off: pl.BlockSpec((None, t, HEAD_DIM), lambda bi, g, i: (bi, 0, off + g))
    return pl.pallas_call(
        functools.partial(_nsa_prompt_kernel, tq=tq, tk=tk),
        grid=(b, NSA_KV, t // tq),
        in_specs=[pl.BlockSpec((None, tq, gw), lambda bi, g, i: (bi, i, g)),
                  kv(0), kv(sv_col0 // HEAD_DIM), kv(0), kv(wv_col0 // HEAD_DIM),
                  pl.BlockSpec((None, None, tq, nsp), lambda bi, g, i: (bi, g, i, 0)),
                  pl.BlockSpec((None, tq, gw), lambda bi, g, i: (bi, i, g)),
                  pl.BlockSpec((3, None, None, tq, NSA_HPG), lambda bi, g, i: (0, bi, g, i, 0))],
        out_specs=pl.BlockSpec((None, tq, gw), lambda bi, g, i: (bi, i, g)),
        out_shape=jax.ShapeDtypeStruct((b, t, w), BF16),
        scratch_shapes=[pltpu.VMEM((2 * NSA_HPG, tq, 1), F32), pltpu.VMEM((2 * NSA_HPG, tq, LANE), F32),
                        pltpu.VMEM((2 * NSA_HPG, tq, HEAD_DIM), F32), pltpu.VMEM((t // tk, NSA_HPG, tq, tk), F32)],
        compiler_params=_params("parallel", "parallel", "arbitrary"),
        name="nsa_prompt",
    )(qn, skn, proj, wkn, proj, selm, ocmp, gate_logits)


def _nsa_dec_kernel(pt_ref, q_ref, sel_ref, ocmp_ref, gl_ref, kp_ref, vp_ref, skn_ref, svn_ref,
                    wkb_ref, wvb_ref, wkn_ref, wvn_ref, o_ref, m_ref, l_ref, acc_ref, *, tq, past_len):
    p = pl.program_id(1)
    npages = pl.num_programs(1)
    rows = NSA_KV * NSA_HPG * tq
    per_g = NSA_HPG * tq
    q = q_ref[...]
    sel = sel_ref[...]

    @pl.when(p == 0)
    def _():
        m_ref[...] = jnp.full(m_ref.shape, NEG, F32)
        l_ref[...] = jnp.zeros_like(l_ref)
        acc_ref[...] = jnp.zeros_like(acc_ref)

    def grid_of(n_keys):
        shape = (rows, n_keys * NSA_KV)
        lane, row = _iota(shape, 1), _iota(shape, 0)
        return lane // NSA_KV, lane % NSA_KV == row // per_g, row % tq

    def update(carry, k_rows, v_rows, msk):
        s = _dot_nt(q, k_rows.astype(BF16)) * SCALE
        return _online_update(carry, s, msk, v_rows.astype(BF16))

    key, same_g, _ = grid_of(PAGE)
    blocks_per_page = PAGE // SEL_BLOCK
    picked = jnp.where(key < SEL_BLOCK, _pick_col(sel, blocks_per_page * p), _pick_col(sel, blocks_per_page * p + 1))
    carry = update((m_ref[...], l_ref[...], acc_ref[...]), kp_ref[...], vp_ref[...], same_g & (picked > 0.5))
    m_ref[...], l_ref[...], acc_ref[...] = carry

    @pl.when(p == npages - 1)
    def _():
        n_new = skn_ref.shape[0] // NSA_KV
        key, same_g, t_row = grid_of(n_new)
        own = same_g & (key <= t_row) & (key < tq)
        new_blk = _pick_col(sel, past_len // SEL_BLOCK) > 0.5
        o_sel = _flash_out(update(carry, skn_ref[...], svn_ref[...], own & new_blk))
        wkey, wsame, wt = grid_of(WINDOW)
        w_carry = update(_flash_init(rows, HEAD_DIM), wkb_ref[...], wvb_ref[...], wsame & (wkey > wt))
        o_win = _flash_out(update(w_carry, wkn_ref[...], wvn_ref[...], own))
        gates = _sigmoid(gl_ref[...])
        o_ref[...] = gates[:, 0:1] * ocmp_ref[...] + gates[:, 1:2] * o_sel + gates[:, 2:3] * o_win


def nsa_sample(qn, skn_new, sv_new, wkn_new, wv_new, cache_sk, cache_sv, win_k, win_v, layer,
               page_table, selm, ocmp, gate_logits):
    b, tq, w = qn.shape
    rows = NSA_KV * NSA_HPG * tq
    npages = page_table.shape[1]
    past_len = npages * PAGE
    nsp = selm.shape[-1]
    assert win_k.shape[2] == WINDOW * NSA_KV and past_len % SEL_BLOCK == 0 and tq * NSA_KV <= LANE
    selrows = jnp.broadcast_to(selm[:, :, None], (b, NSA_KV, NSA_HPG, tq, nsp)).reshape(b, rows, nsp)
    to_rows = lambda a: a.reshape(b, tq, NSA_KV, NSA_HPG, -1).transpose(0, 2, 3, 1, 4).reshape(b, rows, -1)
    new_rows = lambda a: jnp.pad(a.reshape(b, tq * NSA_KV, HEAD_DIM), ((0, 0), (0, LANE - tq * NSA_KV), (0, 0)))
    per_b = lambda *shape: pl.BlockSpec((None,) + shape, lambda bi, p, pt: (bi,) + (0,) * len(shape))
    page = pl.BlockSpec((None, None, PAGE * NSA_KV, HEAD_DIM), lambda bi, p, pt: (layer, pt[bi, p], 0, 0))
    wbuf = pl.BlockSpec((None, None, WINDOW * NSA_KV, HEAD_DIM), lambda bi, p, pt: (layer, bi, 0, 0))
    o = pl.pallas_call(
        functools.partial(_nsa_dec_kernel, tq=tq, past_len=past_len),
        grid_spec=pltpu.PrefetchScalarGridSpec(
            num_scalar_prefetch=1,
            grid=(b, npages),
            in_specs=[per_b(rows, HEAD_DIM), per_b(rows, nsp), per_b(rows, HEAD_DIM), per_b(rows, 3),
                      page, page, per_b(LANE, HEAD_DIM), per_b(LANE, HEAD_DIM), wbuf, wbuf,
                      per_b(LANE, HEAD_DIM), per_b(LANE, HEAD_DIM)],
            out_specs=per_b(rows, HEAD_DIM),
            scratch_shapes=[pltpu.VMEM((rows, 1), F32), pltpu.VMEM((rows, 1), F32), pltpu.VMEM((rows, HEAD_DIM), F32)],
        ),
        out_shape=jax.ShapeDtypeStruct((b, rows, HEAD_DIM), F32),
        compiler_params=_params("parallel", "arbitrary"),
        name="nsa_sample",
    )(page_table, to_rows(qn), selrows, to_rows(ocmp), to_rows(gate_logits), cache_sk, cache_sv,
      new_rows(skn_new), new_rows(sv_new), win_k, win_v, new_rows(wkn_new), new_rows(wv_new))
    return o.reshape(b, NSA_KV, NSA_HPG, tq, HEAD_DIM).transpose(0, 3, 1, 2, 4).reshape(b, tq, w)


MOBA_W = MOBA_HEADS * HEAD_DIM
CONV_DIM = D_INNER + 2 * SSD_GROUPS * SSD_STATE
EVEN_MAIN = 3 * MOBA_W + D_INNER + CONV_DIM
Z_COL0 = 0
XBC_COL0 = D_INNER
Q_COL0 = XBC_COL0 + CONV_DIM
K_COL0 = Q_COL0 + MOBA_W
V_COL0 = K_COL0 + MOBA_W
NSA_Q = NSA_KV * NSA_HPG * HEAD_DIM
ODD_MAIN = NSA_Q + 6 * KV_W
D_FF = 8192


PROJ_TN = 512


def _ws(a, w, layer, res=None, **kw):
    return a, w, layer, res, kw


def _run_pair(big, small):
    req_big, req_small = next(big), next(small)
    while True:
        a, w, layer, res, kw = req_big
        out_big, out_small = matmul_ws(a, w, layer, res=res, small=req_small[0], small_res=req_small[3], **kw)
        try:
            req_big = big.send(out_big)
        except StopIteration as done_big:
            try:
                small.send(out_small)
            except StopIteration as done_small:
                return done_big.value, done_small.value
            raise AssertionError("the two groups must issue the same matmul sequence")
        req_small = small.send(out_small)


def _proj(h2d, w, layer, main, out_cols=None):
    wt = jnp.swapaxes(w, 1, 2)
    tail = jnp.pad(wt[layer, main:, :], ((0, LANE - (w.shape[2] - main)), (0, 0)))
    main_out = yield _ws(h2d, wt, layer, k=h2d.shape[1], n=main, out_cols=out_cols, w_t=True, tn=PROJ_TN)
    return main_out, matmul(h2d, tail, b_t=True)


def _even_layer(x, sample, q_start, cache_k, cache_v, e, page_table, conv_prev, ssm_prev,
                g_norm, w_in, w_out, qk_g, conv_w, conv_b, dt_bias, a_log, d_skip, g_out):
    b, t, d = x.shape
    x2 = x.reshape(b * t, d)
    h = rmsnorm_rows(x2, g_norm)
    n_tiles = EVEN_MAIN // PROJ_TN
    proj, dt_raw = yield from _proj(h, w_in, e, EVEN_MAIN, out_cols=lambda j: (j + Q_COL0 // PROJ_TN) % n_tiles)
    proj = proj.reshape(b, t, EVEN_MAIN)
    dt_raw = dt_raw.reshape(b, t, LANE)
    qn = head_norm(proj, Q_COL0, MOBA_HEADS, qk_g[0], BF16)
    kn = head_norm(proj, K_COL0, MOBA_HEADS, qk_g[1], F32)
    v = proj[..., V_COL0:]
    if sample:
        o_attn = moba_sample(qn, kn, v, cache_k, cache_v, e, page_table).astype(BF16)
    else:
        o_attn = moba_prompt(qn, kn, proj, V_COL0)
    conv = dwconv(proj, XBC_COL0, CONV_DIM, _history8(conv_prev, b, CONV_DIM), conv_w, conv_b, False, F32)
    xbc = proj[..., XBC_COL0:Q_COL0]
    conv_state = jnp.concatenate([_history8(conv_prev, b, CONV_DIM), xbc], axis=1)[:, -(SSD_CONV - 1):]
    if t % SSD_CHUNK:
        tp = -(-t // SSD_CHUNK) * SSD_CHUNK
        padt = lambda a: jnp.pad(a, ((0, 0), (0, tp - t), (0, 0)))
        mixed, ssm = ssd_mixer(padt(conv), padt(proj[..., Z_COL0:XBC_COL0]), 0, padt(dt_raw), ssm_prev,
                               dt_bias, a_log, d_skip, g_out, t, padt(o_attn))
        mixed = mixed[:, :t]
    else:
        mixed, ssm = ssd_mixer(conv, proj, Z_COL0, dt_raw, ssm_prev, dt_bias, a_log, d_skip, g_out, None, o_attn)
    mixed = mixed.reshape(b * t, MOBA_W + D_INNER)
    x2 = yield _ws(mixed, w_out, e, res=x2, k=MOBA_W + D_INNER, n=d, tk=MOBA_W)
    heads = lambda a: a.reshape(b, t, MOBA_HEADS, HEAD_DIM)
    return x2.reshape(b, t, d), heads(kn), heads(v), conv_state, ssm


def _odd_layer(x, sample, q_start, caches, e, page_table, win_k, win_v,
               g_norm, w_in, w_out, qk_g, pe, w1, b1, w2, b2):
    b, t, d = x.shape
    x2 = x.reshape(b * t, d)
    h = rmsnorm_rows(x2, g_norm)
    proj, gl = yield from _proj(h, w_in, e, ODD_MAIN)
    proj = proj.reshape(b, t, ODD_MAIN)
    gl = gl[:, :3 * NSA_KV * NSA_HPG].reshape(b, t, NSA_KV, NSA_HPG, 3)
    col = lambda i: NSA_Q + i * KV_W
    qn = head_norm(proj, 0, NSA_KV * NSA_HPG, qk_g[0], BF16)
    skn = head_norm(proj, col(2), NSA_KV, qk_g[2], F32)
    wkn = head_norm(proj, col(4), NSA_KV, qk_g[3], F32)
    ck, cv, sv, wv = (proj[..., col(i):col(i + 1)] for i in (0, 1, 3, 5))
    if sample:
        cache_ck, cache_cv, cache_sk, cache_sv = caches
        n_pool = cache_ck.shape[1]
        flat = lambda c: c[e].reshape(n_pool * (PAGE // CMP_STRIDE), CHUNK_W)
        rows_k, rows_v, pt = flat(cache_ck), flat(cache_cv), page_table
        n_rows = page_table.shape[1] * PAGE + t
    else:
        flat = lambda a: a.reshape(b * t // CMP_STRIDE, CHUNK_W)
        rows_k, rows_v = flat(ck), flat(cv)
        npg = t // PAGE
        pt = (jnp.arange(b, dtype=jnp.int32)[:, None] * npg + jnp.arange(npg, dtype=jnp.int32)[None, :])
        n_rows = t
    kc = compress_tokens(rows_k, pt, pe[0], w1[0], b1[0], w2[0], b2[0], qk_g[1])
    vc = compress_tokens(rows_v, pt, pe[1], w1[1], b1[1], w2[1], b2[1], None)
    n_sel = -(-n_rows // SEL_BLOCK)
    ocmp, selm = nsa_compressed(qn, kc, vc, q_start, n_sel)
    if sample:
        rows_view = lambda c: c.reshape(c.shape[0], c.shape[1], c.shape[2] * NSA_KV, HEAD_DIM)
        o = nsa_sample(qn, skn, sv, wkn, wv, rows_view(cache_sk), rows_view(cache_sv), rows_view(win_k),
                       rows_view(win_v), e, page_table, selm, ocmp, gl).astype(BF16)
        wk_all = jnp.concatenate([win_k[e].reshape(b, -1, KV_W), wkn], axis=1)
        wv_all = jnp.concatenate([win_v[e].reshape(b, -1, KV_W), wv], axis=1)
    else:
        o = nsa_prompt(qn, skn, wkn, proj, col(3), col(5), selm, ocmp, gl.transpose(4, 0, 2, 1, 3))
        wk_all, wv_all = wkn, wv
    x2 = yield _ws(o.reshape(b * t, NSA_Q), w_out, e, res=x2, k=NSA_Q, n=d)
    keep = min(WINDOW, wk_all.shape[1])
    kvh = lambda a: a.reshape(b, a.shape[1], NSA_KV, HEAD_DIM)
    return (x2.reshape(b, t, d), kvh(ck), kvh(cv), kvh(skn), kvh(sv),
            kvh(wk_all[:, wk_all.shape[1] - keep:]), kvh(wv_all[:, wv_all.shape[1] - keep:]))


def _conv_ffn(x, conv_prev, li, g_norm, w_up, conv_w, conv_b, w_down):
    b, t, d = x.shape
    x2 = x.reshape(b * t, d)
    u = (yield _ws(rmsnorm_rows(x2, g_norm), w_up, li, k=d, n=2 * D_FF)).reshape(b, t, 2 * D_FF)
    hist = _history8(conv_prev, b, 2 * D_FF)
    act = dwconv(u, 0, D_FF, hist, conv_w, conv_b, True, BF16)
    state = jnp.concatenate([hist, u], axis=1)[:, -(FFN_CONV - 1):]
    x2 = yield _ws(act.reshape(b * t, D_FF), w_down, li, res=x2, k=D_FF, n=d)
    return x2.reshape(b, t, d), state


def kernel(x_prompt, x_sample, cache_moba_k, cache_moba_v, state_ssd, state_ssd_conv, cache_nsa_cmp_k, cache_nsa_cmp_v, cache_nsa_sel_k, cache_nsa_sel_v, state_nsa_win_k, state_nsa_win_v, state_ffn_conv, page_table, norm_mix, norm_ffn, even_w_in, even_w_out, moba_qk_norm, ssd_conv_w, ssd_conv_b, ssd_dt_bias, ssd_a_log, ssd_d, ssd_norm, odd_w_in, odd_w_out, nsa_qk_norm, cmp_pe, cmp_w1, cmp_b1, cmp_w2, cmp_b2, ffn_w_up, ffn_conv_w, ffn_conv_b, ffn_w_down):
    depth = norm_mix.shape[0]
    past_len = page_table.shape[1] * PAGE
    state_keys = ("moba_k", "moba_v", "ssd", "ssd_conv", "nsa_cmp_k", "nsa_cmp_v",
                  "nsa_sel_k", "nsa_sel_v", "nsa_win_k", "nsa_win_v", "ffn_conv")
    moba_pool = lambda c: c.reshape(c.shape[0], c.shape[1], PAGE * MOBA_HEADS, HEAD_DIM)
    cache_mk, cache_mv = moba_pool(cache_moba_k), moba_pool(cache_moba_v)

    def trunk(x, q_start, sample):
        b = x.shape[0]
        new = {name: [] for name in state_keys}
        for li in range(depth):
            e = li // 2
            if li % 2 == 0:
                cp = state_ssd_conv[e] if sample else None
                sp = state_ssd[e] if sample else jnp.zeros((b, SSD_HEADS, SSD_HEAD_DIM, SSD_STATE), F32)
                x, k, v, cs, ss = yield from _even_layer(x, sample, q_start, cache_mk, cache_mv, e, page_table, cp, sp,
                                                         norm_mix[li], even_w_in, even_w_out, moba_qk_norm[e],
                                                         ssd_conv_w[e], ssd_conv_b[e], ssd_dt_bias[e], ssd_a_log[e],
                                                         ssd_d[e], ssd_norm[e])
                for name, val in zip(("moba_k", "moba_v", "ssd_conv", "ssd"), (k, v, cs, ss)):
                    new[name].append(val)
            else:
                caches = (cache_nsa_cmp_k, cache_nsa_cmp_v, cache_nsa_sel_k, cache_nsa_sel_v)
                outs = yield from _odd_layer(x, sample, q_start, caches, e, page_table, state_nsa_win_k,
                                             state_nsa_win_v, norm_mix[li], odd_w_in, odd_w_out, nsa_qk_norm[e],
                                             cmp_pe[e], cmp_w1[e], cmp_b1[e], cmp_w2[e], cmp_b2[e])
                x = outs[0]
                for name, val in zip(("nsa_cmp_k", "nsa_cmp_v", "nsa_sel_k", "nsa_sel_v", "nsa_win_k", "nsa_win_v"), outs[1:]):
                    new[name].append(val)
            fp = state_ffn_conv[li] if sample else None
            x, fs = yield from _conv_ffn(x, fp, li, norm_ffn[li], ffn_w_up, ffn_conv_w[li], ffn_conv_b[li], ffn_w_down)
            new["ffn_conv"].append(fs)
        return x, {name: jnp.stack(rows) for name, rows in new.items()}

    (y_prompt, sp), (y_sample, ss) = _run_pair(trunk(x_prompt, 0, False), trunk(x_sample, past_len, True))
    out = [y_prompt, y_sample]
    for name in state_keys:
        out += [sp[name], ss[name]]
    return tuple(out)
```

```python
import functools

import numpy as np
import jax
import jax.numpy as jnp
from jax import lax
from jax.experimental import pallas as pl
from jax.experimental.pallas import tpu as pltpu

F32 = jnp.float32
BF16 = jnp.bfloat16

LANE = 128
VMEM_LIMIT_BYTES = 56 * 1024 * 1024

EPS = 1e-6
HEAD_DIM = 128
SCALE = HEAD_DIM ** -0.5
EXP2_SCALE = SCALE * float(np.log2(np.e))
NEG = -1e30

PAGE = 128
MOBA_HEADS = 16
MOBA_BLOCK = 256
MOBA_TOPK = 3
SSD_HEADS = 64
SSD_HEAD_DIM = 64
SSD_GROUPS = 8
SSD_STATE = 128
SSD_CONV = 4
SSD_CHUNK = 128
D_INNER = 4096
GROUP_W = D_INNER // SSD_GROUPS
HEADS_PER_GROUP = SSD_HEADS // SSD_GROUPS
NSA_KV = 4
NSA_HPG = 8
KV_W = NSA_KV * HEAD_DIM
CMP_LEN = 32
CMP_STRIDE = 16
SEL_BLOCK = 64
SEL_TOPN = 16
WINDOW = 512
FFN_CONV = 3


def _params(*sem):
    return pltpu.CompilerParams(dimension_semantics=sem, vmem_limit_bytes=VMEM_LIMIT_BYTES)


def _dot(a, b):
    return jnp.dot(a, b, preferred_element_type=F32)


def _dot_nt(a, b):
    return lax.dot_general(a, b, (((1,), (1,)), ((), ())), preferred_element_type=F32)


def _dot_tn(a, b):
    return lax.dot_general(a, b, (((0,), (0,)), ((), ())), preferred_element_type=F32)


def _split3(x):
    hi = x.astype(BF16)
    r1 = x - hi.astype(F32)
    mid = r1.astype(BF16)
    lo = (r1 - mid.astype(F32)).astype(BF16)
    return hi, mid, lo


def _sigmoid(x):
    return 1.0 / (1.0 + jnp.exp(-x))


def _silu(x):
    return x * _sigmoid(x)


def _softplus(x):
    return jnp.maximum(x, 0.0) + jnp.log1p(jnp.exp(-jnp.abs(x)))


def _gelu_tanh(x):
    return 0.5 * x * (1.0 + jnp.tanh(np.sqrt(2.0 / np.pi).astype(np.float32) * (x + 0.044715 * (x * x * x))))


def _rms(x, g):
    return x * lax.rsqrt(jnp.mean(x * x, axis=-1, keepdims=True) + EPS) * g


def _iota(shape, axis):
    return lax.broadcasted_iota(jnp.int32, shape, axis)


def _rank_before(score, n_cols):
    col = _iota(score.shape, 1)
    rank = jnp.zeros(score.shape, F32)
    for c in range(n_cols):
        sc = score[:, c:c + 1]
        rank = rank + jnp.where((sc > score) | ((sc == score) & (c < col)), 1.0, 0.0)
    return rank


def _pick_col(x, c):
    return jnp.sum(jnp.where(_iota(x.shape, 1) == c, x, 0.0), axis=1, keepdims=True)


def _online_update(carry, s, msk, v_bf16):
    m, l, acc = carry
    s = jnp.where(msk, s, NEG)
    m_new = jnp.maximum(m, jnp.max(s, axis=1, keepdims=True))
    p = jnp.where(msk, jnp.exp(s - m_new), 0.0)
    alpha = jnp.exp(m - m_new)
    l = alpha * l + jnp.sum(p, axis=1, keepdims=True)
    acc = alpha * acc + _dot(p.astype(BF16), v_bf16)
    return m_new, l, acc


def _flash_init(rows, dv):
    return (jnp.full((rows, 1), NEG, F32), jnp.zeros((rows, 1), F32), jnp.zeros((rows, dv), F32))


def _flash_out(carry):
    _, l, acc = carry
    return acc / jnp.where(l > 0.0, l, 1.0)


def _rmsnorm_kernel(x_ref, g_ref, o_ref):
    o_ref[...] = _rms(x_ref[...], g_ref[...]).astype(o_ref.dtype)


def rmsnorm_rows(x2d, g):
    m, d = x2d.shape
    tm = min(m, 256)
    return pl.pallas_call(
        _rmsnorm_kernel,
        grid=(m // tm,),
        in_specs=[pl.BlockSpec((tm, d), lambda i: (i, 0)), pl.BlockSpec((1, d), lambda i: (0, 0))],
        out_specs=pl.BlockSpec((tm, d), lambda i: (i, 0)),
        out_shape=jax.ShapeDtypeStruct((m, d), BF16),
        compiler_params=_params("parallel"),
        name="rmsnorm_rows",
    )(x2d, g.reshape(1, d))


def _matmul_kernel(*refs, nk, has_res, b_t):
    if has_res:
        a_ref, b_ref, r_ref, o_ref = refs
    else:
        a_ref, b_ref, o_ref = refs
    part = (_dot_nt if b_t else _dot)(a_ref[...].astype(BF16), b_ref[...].astype(BF16))
    if nk == 1:
        o_ref[...] = part + r_ref[...] if has_res else part
        return
    k = pl.program_id(2)

    @pl.when(k == 0)
    def _():
        o_ref[...] = part + r_ref[...] if has_res else part

    @pl.when(k > 0)
    def _():
        o_ref[...] += part


def matmul(a, b, res=None, b_t=False, tm=1024, tn=512, tk=4096):
    m, k = a.shape
    n = b.shape[0] if b_t else b.shape[1]
    tm, tn, tk = min(tm, m), min(tn, n), min(tk, k)
    assert m % tm == 0 and n % tn == 0 and k % tk == 0, (a.shape, b.shape)
    nk = k // tk
    b_spec = pl.BlockSpec((tn, tk), lambda i, j, kk: (j, kk)) if b_t else pl.BlockSpec((tk, tn), lambda i, j, kk: (kk, j))
    in_specs = [pl.BlockSpec((tm, tk), lambda i, j, kk: (i, kk)), b_spec]
    args = [a, b]
    if res is not None:
        in_specs.append(pl.BlockSpec((tm, tn), lambda i, j, kk: (i, j)))
        args.append(res)
    return pl.pallas_call(
        functools.partial(_matmul_kernel, nk=nk, has_res=res is not None, b_t=b_t),
        grid=(m // tm, n // tn, nk),
        in_specs=in_specs,
        out_specs=pl.BlockSpec((tm, tn), lambda i, j, kk: (i, j)),
        out_shape=jax.ShapeDtypeStruct((m, n), F32),
        compiler_params=_params("parallel", "parallel", "arbitrary"),
        name="matmul",
    )(*args)


def _matmul_ws_kernel(*refs, nk, has_res, has_small, w_t):
    refs = list(refs)
    a_ref, w_ref = refs.pop(0), refs.pop(0)
    r_ref = refs.pop(0) if has_res else None
    a2_ref = refs.pop(0) if has_small else None
    r2_ref = refs.pop(0) if has_small and has_res else None
    o_ref = refs.pop(0)
    o2_ref = refs.pop(0) if has_small else None
    wbf_ref, = refs
    i = pl.program_id(1)
    kk = pl.program_id(2)
    mm = _dot_nt if w_t else _dot

    def accumulate(out_ref, part, res_ref):
        if nk == 1:
            out_ref[...] = part + res_ref[...] if has_res else part
            return

        @pl.when(kk == 0)
        def _():
            out_ref[...] = part + res_ref[...] if has_res else part

        @pl.when(kk > 0)
        def _():
            out_ref[...] += part

    @pl.when(i == 0)
    def _():
        wbf_ref[kk] = w_ref[...].astype(BF16)
        if has_small:
            accumulate(o2_ref, mm(a2_ref[...], wbf_ref[kk]), r2_ref)

    accumulate(o_ref, mm(a_ref[...], wbf_ref[kk]), r_ref)


def matmul_ws(a, w, layer, *, k, n, res=None, small=None, small_res=None, out_cols=None, w_t=False,
              tm=1024, tn=512, tk=4096):
    m = a.shape[0]
    tm, tn, tk = min(tm, m), min(tn, n), min(tk, k)
    assert m % tm == 0 and n % tn == 0 and k % tk == 0, (a.shape, w.shape, k, n)
    assert (res is None) == (small_res is None) or small is None
    nk = k // tk
    k_idx = (lambda i, kk: 0) if nk == 1 else (lambda i, kk: jnp.where(i == 0, kk, nk - 1))
    if w_t:
        w_spec = pl.BlockSpec((None, tn, tk), lambda j, i, kk: (layer, j, k_idx(i, kk)))
    else:
        w_spec = pl.BlockSpec((None, tk, tn), lambda j, i, kk: (layer, k_idx(i, kk), j))
    col = (lambda j: j) if out_cols is None else out_cols
    o_spec = pl.BlockSpec((tm, tn), lambda j, i, kk: (i, col(j)))
    in_specs = [pl.BlockSpec((tm, tk), lambda j, i, kk: (i, kk)), w_spec]
    args = [a, w]
    out_specs, out_shape = [o_spec], [jax.ShapeDtypeStruct((m, n), F32)]
    if res is not None:
        in_specs.append(o_spec)
        args.append(res)
    if small is not None:
        m2 = small.shape[0]
        o2_spec = pl.BlockSpec((m2, tn), lambda j, i, kk: (0, col(j)))
        in_specs.append(pl.BlockSpec((m2, tk), lambda j, i, kk: (0, k_idx(i, kk))))
        args.append(small)
        if res is not None:
            in_specs.append(o2_spec)
            args.append(small_res)
        out_specs.append(o2_spec)
        out_shape.append(jax.ShapeDtypeStruct((m2, n), F32))
    outs = pl.pallas_call(
        functools.partial(_matmul_ws_kernel, nk=nk, has_res=res is not None, has_small=small is not None, w_t=w_t),
        grid=(n // tn, m // tm, nk),
        in_specs=in_specs,
        out_specs=out_specs,
        out_shape=out_shape,
        scratch_shapes=[pltpu.VMEM((nk, tn, tk) if w_t else (nk, tk, tn), BF16)],
        compiler_params=_params("arbitrary", "arbitrary", "arbitrary"),
        name="matmul_ws",
    )(*args)
    return outs[0] if small is None else tuple(outs)


CONV_ROWS = 256


def _ffn_up_kernel(a_ref, wa_ref, wg_ref, cwa_ref, cwg_ref, cba_ref, cbg_ref, a2_ref, h2a_ref, h2g_ref,
                   act_ref, sta_ref, stg_ref, act2_ref, st2a_ref, st2g_ref,
                   wbf_ref, xpa_ref, xpg_ref, xp2_ref, *, tiles_per_seq, width, t2):
    i = pl.program_id(1)
    tm = a_ref.shape[0]
    halves = ((0, cwa_ref, cba_ref), (1, cwg_ref, cbg_ref))

    def conv(xp, rows, r0, cw_ref, cb_ref):
        acc = cb_ref[...]
        for k in range(width):
            acc = acc + cw_ref[k:k + 1, :] * xp[pl.ds(8 - (width - 1) + k + r0, rows), :]
        return acc

    @pl.when(i == 0)
    def _():
        wbf_ref[0] = wa_ref[...].astype(BF16)
        wbf_ref[1] = wg_ref[...].astype(BF16)
        nb2 = a2_ref.shape[0] // t2
        for h, h2_ref, st2_ref in ((0, h2a_ref, st2a_ref), (1, h2g_ref, st2g_ref)):
            u2 = _dot(a2_ref[...], wbf_ref[h])
            for b in range(nb2):
                xp2_ref[h, b, 0:8, :] = h2_ref[b]
                xp2_ref[h, b, 8:8 + t2, :] = u2[b * t2:(b + 1) * t2, :]
                st2_ref[b] = xp2_ref[h, b, t2:t2 + 8, :]
        for b in range(nb2):
            ca, cg = (conv(xp2_ref.at[h, b], t2, 0, cw_ref, cb_ref) for h, cw_ref, cb_ref in halves)
            act2_ref[b * t2:(b + 1) * t2, :] = (ca * _silu(cg)).astype(act2_ref.dtype)

    first = i % tiles_per_seq == 0
    for (h, _, _), xp_ref, st_ref in zip(halves, (xpa_ref, xpg_ref), (sta_ref, stg_ref)):
        @pl.when(first)
        def _(xp_ref=xp_ref):
            xp_ref[0:8, :] = jnp.zeros((8, xp_ref.shape[1]), F32)

        @pl.when(jnp.logical_not(first))
        def _(xp_ref=xp_ref):
            xp_ref[0:8, :] = xp_ref[tm:tm + 8, :]

        xp_ref[8:8 + tm, :] = _dot(a_ref[...], wbf_ref[h])
        st_ref[...] = xp_ref[tm:tm + 8, :]
    rows = min(tm, CONV_ROWS)
    for c in range(tm // rows):
        ca, cg = (conv(xp_ref, rows, c * rows, cw_ref, cb_ref)
                  for (h, cw_ref, cb_ref), xp_ref in zip(halves, (xpa_ref, xpg_ref)))
        act_ref[c * rows:(c + 1) * rows, :] = (ca * _silu(cg)).astype(act_ref.dtype)


def ffn_up_glu(a, seq_len, a2, seq_len2, hist2, w, layer, conv_w, conv_b, tn=256):
    m, k = a.shape
    m2 = a2.shape[0]
    f = w.shape[2] // 2
    width = conv_w.shape[0]
    tm = min(1024, seq_len)
    assert seq_len % tm == 0 and m % tm == 0 and f % tn == 0 and m2 % seq_len2 == 0 and seq_len2 % 8 == 0
    nj, nb, nb2 = f // tn, m // seq_len, m2 // seq_len2
    tiles_per_seq = seq_len // tm
    cb = conv_b.reshape(1, -1)
    half = lambda shape, idx: [pl.BlockSpec(shape, lambda j, i, g=g: idx(j + g * nj, i)) for g in range(2)]
    in_specs = ([pl.BlockSpec((tm, k), lambda j, i: (i, 0))]
                + half((None, k, tn), lambda c, i: (layer, 0, c))
                + half((width, tn), lambda c, i: (0, c)) + half((1, tn), lambda c, i: (0, c))
                + [pl.BlockSpec((m2, k), lambda j, i: (0, 0))] + half((nb2, 8, tn), lambda c, i: (0, 0, c)))
    out_specs = ([pl.BlockSpec((tm, tn), lambda j, i: (i, j))]
                 + [pl.BlockSpec((None, 8, tn), lambda j, i: (i // tiles_per_seq, 0, j))] * 2
                 + [pl.BlockSpec((m2, tn), lambda j, i: (0, j))] + [pl.BlockSpec((nb2, 8, tn), lambda j, i: (0, 0, j))] * 2)
    out_shape = ([jax.ShapeDtypeStruct((m, f), BF16)] + [jax.ShapeDtypeStruct((nb, 8, f), F32)] * 2
                 + [jax.ShapeDtypeStruct((m2, f), BF16)] + [jax.ShapeDtypeStruct((nb2, 8, f), F32)] * 2)
    act, sta, stg, act2, st2a, st2g = pl.pallas_call(
        functools.partial(_ffn_up_kernel, tiles_per_seq=tiles_per_seq, width=width, t2=seq_len2),
        grid=(nj, m // tm),
        in_specs=in_specs,
        out_specs=out_specs,
        out_shape=out_shape,
        scratch_shapes=[pltpu.VMEM((2, k, tn), BF16), pltpu.VMEM((8 + tm, tn), F32), pltpu.VMEM((8 + tm, tn), F32),
                        pltpu.VMEM((2, nb2, 8 + seq_len2, tn), F32)],
        compiler_params=_params("arbitrary", "arbitrary"),
        name="ffn_up_glu",
    )(a, w, w, conv_w, conv_w, cb, cb, a2, hist2, hist2)
    return (act, jnp.concatenate([sta, stg], axis=-1)), (act2, jnp.concatenate([st2a, st2g], axis=-1))


def _head_norm_kernel(x_ref, g_ref, o_ref, *, heads):
    g = g_ref[...]
    for h in range(heads):
        sl = slice(h * HEAD_DIM, (h + 1) * HEAD_DIM)
        o_ref[:, sl] = _rms(x_ref[:, sl], g).astype(o_ref.dtype)


def head_norm(x, col0, n_heads, g, out_dtype):
    b, t, _ = x.shape
    hb = min(n_heads, 4)
    w = hb * HEAD_DIM
    assert col0 % w == 0 and n_heads % hb == 0
    tq = min(t, 512)
    return pl.pallas_call(
        functools.partial(_head_norm_kernel, heads=hb),
        grid=(b, t // tq, n_heads // hb),
        in_specs=[pl.BlockSpec((None, tq, w), lambda bi, i, j: (bi, i, col0 // w + j)),
                  pl.BlockSpec((1, HEAD_DIM), lambda bi, i, j: (0, 0))],
        out_specs=pl.BlockSpec((None, tq, w), lambda bi, i, j: (bi, i, j)),
        out_shape=jax.ShapeDtypeStruct((b, t, n_heads * HEAD_DIM), out_dtype),
        compiler_params=_params("parallel", "parallel", "parallel"),
        name="head_norm",
    )(x, g.reshape(1, HEAD_DIM))


def _dwconv_kernel(x_ref, p_ref, w_ref, b_ref, o_ref, s_ref, *, width, t):
    s_ref[0:8, :] = p_ref[...]
    s_ref[8:8 + t, :] = x_ref[...]
    tc = min(t, CONV_ROWS)
    for c in range(t // tc):
        acc = b_ref[...]
        for i in range(width):
            acc = acc + w_ref[i:i + 1, :] * s_ref[pl.ds(8 - (width - 1) + i + c * tc, tc), :]
        o_ref[c * tc:(c + 1) * tc, :] = _silu(acc).astype(o_ref.dtype)


def dwconv_silu(x, col0, c_out, prev8, w, bias):
    b, t, _ = x.shape
    width = w.shape[0]
    tc = min(256 if t > 64 else 2048, c_out)
    assert col0 % tc == 0 and c_out % tc == 0
    return pl.pallas_call(
        functools.partial(_dwconv_kernel, width=width, t=t),
        grid=(b, c_out // tc),
        in_specs=[pl.BlockSpec((None, t, tc), lambda bi, j: (bi, 0, col0 // tc + j)),
                  pl.BlockSpec((None, 8, tc), lambda bi, j: (bi, 0, j)),
                  pl.BlockSpec((width, tc), lambda bi, j: (0, j)),
                  pl.BlockSpec((1, tc), lambda bi, j: (0, j))],
        out_specs=pl.BlockSpec((None, t, tc), lambda bi, j: (bi, 0, j)),
        out_shape=jax.ShapeDtypeStruct((b, t, c_out), F32),
        scratch_shapes=[pltpu.VMEM((8 + t, tc), F32)],
        compiler_params=_params("parallel", "parallel"),
        name="dwconv_silu",
    )(x, prev8, w, bias.reshape(1, -1))


def _history8(state, b, c):
    if state is None:
        return jnp.zeros((b, 8, c), F32)
    return jnp.concatenate([jnp.zeros((b, 8 - state.shape[1], c), F32), state], axis=1)


def _moba_tile(i, nb, q_ref, o_ref, kmean_ref, kbf_ref, vbf_ref):
    blk, half = MOBA_BLOCK, MOBA_BLOCK // 2
    fold = lambda a, op: functools.reduce(op, [a[:, c * LANE:(c + 1) * LANE] for c in range(a.shape[1] // LANE)])
    k_blk = lambda n: kbf_ref[n * blk:(n + 1) * blk, :]
    v_blk = lambda n: vbf_ref[n * blk:(n + 1) * blk, :]
    spread = jnp.where(_iota((LANE, max(i, 1) * LANE), 0) == _iota((LANE, max(i, 1) * LANE), 1) // LANE, 1.0, 0.0).astype(BF16)
    for hq in range(2):
        q = q_ref[hq * half:(hq + 1) * half, :]
        masks = []
        if i > 0:
            gate = _dot_nt(q, kmean_ref[...].astype(BF16))
            valid = _iota(gate.shape, 1) < i
            gate = jnp.where(valid, gate, -jnp.inf)
            sel = jnp.where(valid & (_rank_before(gate, i) < min(MOBA_TOPK, nb)), 1.0, 0.0).astype(BF16)
            selb = _dot(sel, spread)
            masks = [jnp.concatenate([selb[:, n * LANE:(n + 1) * LANE]] * (blk // LANE), axis=1) > 0.5 for n in range(i)]
        causal = _iota((half, blk), 1) <= _iota((half, blk), 0) + hq * half
        masks.append(causal)
        scores = lambda n: jnp.where(masks[n], _dot_nt(q, k_blk(n)), NEG)
        part = functools.reduce(jnp.maximum, [fold(scores(n), jnp.maximum) for n in range(i + 1)])
        m = jnp.max(part, axis=1, keepdims=True)
        l_part, acc = None, None
        for n in range(i + 1):
            p = jnp.exp2((scores(n) - m) * EXP2_SCALE)
            pv = _dot(p.astype(BF16), v_blk(n))
            l_part = fold(p, jnp.add) if n == 0 else l_part + fold(p, jnp.add)
            acc = pv if n == 0 else acc + pv
        l = jnp.sum(l_part, axis=1, keepdims=True)
        o_ref[hq * half:(hq + 1) * half, :] = (acc / l).astype(o_ref.dtype)


def _moba_tiles_kernel(q_ref, k_ref, v_ref, o_ref, kmean_ref, kbf_ref, vbf_ref, *, nb):
    blk = MOBA_BLOCK
    i = pl.program_id(2)

    @pl.when(i == 0)
    def _():
        kmean_ref[...] = jnp.zeros_like(kmean_ref)
        for n in range(nb):
            kmean_ref[n:n + 1, :] = jnp.mean(k_ref[n * blk:(n + 1) * blk, :], axis=0, keepdims=True)
        kbf_ref[...] = k_ref[...].astype(BF16)
        vbf_ref[...] = v_ref[...].astype(BF16)

    for c in range(nb):
        @pl.when(i == c)
        def _(c=c):
            _moba_tile(c, nb, q_ref, o_ref, kmean_ref, kbf_ref, vbf_ref)


def moba_prompt(qn, kn, proj, v_col0):
    b, t, w = qn.shape
    nh = w // HEAD_DIM
    nb = t // MOBA_BLOCK
    assert t % MOBA_BLOCK == 0 and nb <= LANE
    return pl.pallas_call(
        functools.partial(_moba_tiles_kernel, nb=nb),
        grid=(b, nh, nb),
        in_specs=[pl.BlockSpec((None, MOBA_BLOCK, HEAD_DIM), lambda bi, h, i: (bi, i, h)),
                  pl.BlockSpec((None, t, HEAD_DIM), lambda bi, h, i: (bi, 0, h)),
                  pl.BlockSpec((None, t, HEAD_DIM), lambda bi, h, i: (bi, 0, v_col0 // HEAD_DIM + h))],
        out_specs=pl.BlockSpec((None, MOBA_BLOCK, HEAD_DIM), lambda bi, h, i: (bi, i, h)),
        out_shape=jax.ShapeDtypeStruct((b, t, w), BF16),
        scratch_shapes=[pltpu.VMEM((LANE, HEAD_DIM), F32), pltpu.VMEM((t, HEAD_DIM), BF16), pltpu.VMEM((t, HEAD_DIM), BF16)],
        compiler_params=_params("parallel", "parallel", "arbitrary"),
        name="moba_prompt",
    )(qn, kn, proj)


def _moba_dec_block_kernel(pt_ref, q_ref, ka_ref, kb_ref, va_ref, vb_ref,
                           acc_ref, g_ref, m_ref, l_ref, *, nh, tq):
    n = pl.program_id(1)
    q = q_ref[...]
    kb = jnp.concatenate([ka_ref[...], kb_ref[...]], axis=0).astype(BF16)
    vb = jnp.concatenate([va_ref[...], vb_ref[...]], axis=0).astype(BF16)
    s_raw = _dot_nt(q, kb)
    same_head = _iota(s_raw.shape, 1) % nh == _iota(s_raw.shape, 0) // tq
    gate = jnp.sum(jnp.where(same_head, s_raw, 0.0), axis=1, keepdims=True) * (1.0 / MOBA_BLOCK)
    s = jnp.where(same_head, s_raw * SCALE, NEG)
    m = jnp.max(s, axis=1, keepdims=True)
    p = jnp.exp(s - m)
    l = jnp.sum(p, axis=1, keepdims=True)
    acc_ref[...] = _dot(p.astype(BF16), vb)

    @pl.when(n == 0)
    def _():
        g_ref[...] = jnp.zeros_like(g_ref)
        m_ref[...] = jnp.zeros_like(m_ref)
        l_ref[...] = jnp.zeros_like(l_ref)

    lane = _iota(g_ref.shape, 1)
    g_ref[...] = jnp.where(lane == n, gate, g_ref[...])
    m_ref[...] = jnp.where(lane == n, m, m_ref[...])
    l_ref[...] = jnp.where(lane == n, l, l_ref[...])


def _moba_dec_combine_kernel(g_ref, m_ref, l_ref, acc_ref, q_ref, kn_ref, vn_ref, o_ref, *, nbk, nh, tq):
    gate = g_ref[...]
    lane = _iota(gate.shape, 1)
    row = _iota(gate.shape, 0)
    valid = lane < nbk
    gate = jnp.where(valid, gate, -jnp.inf)
    sel = valid & (_rank_before(gate, nbk) < min(MOBA_TOPK, nbk + 1))
    q = q_ref[...]
    s_own = _dot_nt(q, kn_ref[...].astype(BF16)) * SCALE
    own = (lane % nh == row // tq) & (lane // nh <= row % tq)
    m_blk = m_ref[...]
    m_tot = jnp.maximum(jnp.max(jnp.where(sel, m_blk, NEG), axis=1, keepdims=True),
                        jnp.max(jnp.where(own, s_own, NEG), axis=1, keepdims=True))
    wgt = jnp.where(sel, jnp.exp(m_blk - m_tot), 0.0)
    p_own = jnp.where(own, jnp.exp(s_own - m_tot), 0.0)
    l_tot = jnp.sum(wgt * l_ref[...], axis=1, keepdims=True) + jnp.sum(p_own, axis=1, keepdims=True)
    acc = _dot(p_own.astype(BF16), vn_ref[...].astype(BF16))
    for n in range(nbk):
        acc = acc + wgt[:, n:n + 1] * acc_ref[n]
    o_ref[...] = acc / jnp.where(l_tot > 0.0, l_tot, 1.0)


def moba_sample(qn, kn_new, v_new, cache_k, cache_v, layer, page_table):
    b, tq, w = qn.shape
    nh = w // HEAD_DIM
    rows = nh * tq
    npages = page_table.shape[1]
    ppb = MOBA_BLOCK // PAGE
    nbk = npages // ppb
    assert rows == LANE and ppb == 2 and nbk <= LANE
    qrows = qn.reshape(b, tq, nh, HEAD_DIM).transpose(0, 2, 1, 3).reshape(b, rows, HEAD_DIM)
    kn_rows = kn_new.reshape(b, tq * nh, HEAD_DIM)
    vn_rows = v_new.reshape(b, tq * nh, HEAD_DIM)

    page = lambda j: pl.BlockSpec((None, None, PAGE * nh, HEAD_DIM), lambda bi, n, pt: (layer, pt[bi, ppb * n + j], 0, 0))
    stat = pl.BlockSpec((None, rows, LANE), lambda bi, n, pt: (bi, 0, 0))
    acc, g, m, l = pl.pallas_call(
        functools.partial(_moba_dec_block_kernel, nh=nh, tq=tq),
        grid_spec=pltpu.PrefetchScalarGridSpec(
            num_scalar_prefetch=1,
            grid=(b, nbk),
            in_specs=[pl.BlockSpec((None, rows, HEAD_DIM), lambda bi, n, pt: (bi, 0, 0)),
                      page(0), page(1), page(0), page(1)],
            out_specs=[pl.BlockSpec((None, None, rows, HEAD_DIM), lambda bi, n, pt: (bi, n, 0, 0)), stat, stat, stat],
        ),
        out_shape=[jax.ShapeDtypeStruct((b, nbk, rows, HEAD_DIM), F32)] + [jax.ShapeDtypeStruct((b, rows, LANE), F32)] * 3,
        compiler_params=_params("parallel", "arbitrary"),
        name="moba_sample_blocks",
    )(page_table, qrows, cache_k, cache_k, cache_v, cache_v)

    per_b = lambda *shape: pl.BlockSpec((None,) + shape, lambda bi: (bi,) + (0,) * len(shape))
    o = pl.pallas_call(
        functools.partial(_moba_dec_combine_kernel, nbk=nbk, nh=nh, tq=tq),
        grid=(b,),
        in_specs=[per_b(rows, LANE), per_b(rows, LANE), per_b(rows, LANE), per_b(nbk, rows, HEAD_DIM),
                  per_b(rows, HEAD_DIM), per_b(tq * nh, HEAD_DIM), per_b(tq * nh, HEAD_DIM)],
        out_specs=per_b(rows, HEAD_DIM),
        out_shape=jax.ShapeDtypeStruct((b, rows, HEAD_DIM), F32),
        compiler_params=_params("parallel"),
        name="moba_sample_combine",
    )(g, m, l, acc, qrows, kn_rows, vn_rows)
    return o.reshape(b, nh, tq, HEAD_DIM).transpose(0, 2, 1, 3).reshape(b, tq, w)


def _ssd_kernel(x_ref, bm_ref, cm_ref, z_ref, dtc_ref, dtr_ref, bias_c_ref, bias_r_ref, alog_c_ref, alog_r_ref,
                dskip_ref, gout_ref, s0_ref, pre_ref, y_ref, sfin_ref, state_ref, *, q, t_valid):
    c = pl.program_id(1)
    nc = pl.num_programs(1)
    pre_w = pre_ref.shape[1]
    y_ref[:, :pre_w] = pre_ref[...]

    @pl.when(c == 0)
    def _():
        state_ref[...] = s0_ref[...]

    dtc = _softplus(dtc_ref[...] + bias_c_ref[...])
    dtr = _softplus(dtr_ref[...] + bias_r_ref[...])
    if t_valid is not None:
        dtc = jnp.where(c * q + _iota(dtc.shape, 0) < t_valid, dtc, 0.0)
        dtr = jnp.where(c * q + _iota(dtr.shape, 1) < t_valid, dtr, 0.0)
    da_c = dtc * -jnp.exp(alog_c_ref[...])
    da_r = dtr * -jnp.exp(alog_r_ref[...])
    tri = _iota((q, q), 0) >= _iota((q, q), 1)
    tri_bf = jnp.where(tri, 1.0, 0.0).astype(BF16)
    cum_c = sum(_dot(tri_bf, part) for part in _split3(da_c))
    cum_r = sum(_dot_nt(part, tri_bf) for part in _split3(da_r))
    lane = _iota((q, LANE), 1)
    lo_half = lane < SSD_HEAD_DIM
    row_lo = _iota((LANE, 1), 0) < SSD_HEAD_DIM
    pairs = HEADS_PER_GROUP // 2

    for g in range(SSD_GROUPS):
        xg = x_ref[:, g * GROUP_W:(g + 1) * GROUP_W]
        bm = bm_ref[:, g * SSD_STATE:(g + 1) * SSD_STATE].astype(BF16)
        cm = cm_ref[:, g * SSD_STATE:(g + 1) * SSD_STATE].astype(BF16)
        cb = _dot_nt(cm, bm)
        y_parts = []
        for pr in range(pairs):
            x_pair = xg[:, pr * LANE:(pr + 1) * LANE]
            x_bf = x_pair.astype(BF16)
            ys, cols, lasts = [], [], []
            for h in (g * HEADS_PER_GROUP + 2 * pr, g * HEADS_PER_GROUP + 2 * pr + 1):
                col = cum_c[:, h:h + 1]
                seg = col - cum_r[h:h + 1, :]
                decay = jnp.where(tri, jnp.exp(jnp.where(tri, seg, 0.0)), 0.0)
                wgt = cb * decay * dtr[h:h + 1, :]
                ys.append(_dot(wgt.astype(BF16), x_bf))
                cols.append(col)
                lasts.append(cum_c[q - 1:q, h:h + 1])
            sidx = g * pairs + pr
            state = state_ref[sidx]
            y_pair = jnp.where(lo_half, ys[0], ys[1])
            carry_in = _dot_nt(cm, state.astype(BF16))
            y_pair = y_pair + carry_in * jnp.exp(jnp.where(lo_half, cols[0], cols[1]))
            y_parts.append(y_pair)
            tail = jnp.where(lo_half,
                             jnp.exp(lasts[0] - cols[0]) * dtc[:, 2 * sidx:2 * sidx + 1],
                             jnp.exp(lasts[1] - cols[1]) * dtc[:, 2 * sidx + 1:2 * sidx + 2])
            upd = _dot_tn((x_pair * tail).astype(BF16), bm)
            state_ref[sidx] = state * jnp.where(row_lo, jnp.exp(lasts[0]), jnp.exp(lasts[1])) + upd
        sl = slice(g * GROUP_W, (g + 1) * GROUP_W)
        yg = jnp.concatenate(y_parts, axis=1) + xg * dskip_ref[:, sl]
        yg = yg * _silu(z_ref[:, sl])
        y_ref[:, pre_w + g * GROUP_W:pre_w + (g + 1) * GROUP_W] = _rms(yg, gout_ref[:, sl]).astype(y_ref.dtype)

    @pl.when(c == nc - 1)
    def _():
        sfin_ref[...] = state_ref[...]


def ssd_mixer(conv, z_src, z_col0, dt_raw, s0, dt_bias, a_log, d_skip, g_out, t_valid, prefix):
    b, t, _ = conv.shape
    pre_w = prefix.shape[-1]
    q = SSD_CHUNK
    assert t % q == 0
    nc = t // q
    dtr = dt_raw.transpose(0, 2, 1)
    pad_h = LANE - SSD_HEADS
    bias_c = jnp.pad(dt_bias, (0, pad_h)).reshape(1, LANE)
    bias_r = bias_c.reshape(LANE, 1)
    alog_c = jnp.pad(a_log, (0, pad_h)).reshape(1, LANE)
    alog_r = alog_c.reshape(LANE, 1)
    dskip = jnp.repeat(d_skip, SSD_HEAD_DIM).reshape(1, D_INNER)
    npair = SSD_HEADS // 2
    s0p = s0.reshape(b, npair, 2 * SSD_HEAD_DIM, SSD_STATE)
    nbc = D_INNER // (SSD_GROUPS * SSD_STATE)
    const = lambda shape: pl.BlockSpec(shape, lambda bi, c: (0,) * len(shape))
    y, sfin = pl.pallas_call(
        functools.partial(_ssd_kernel, q=q, t_valid=t_valid),
        grid=(b, nc),
        in_specs=[pl.BlockSpec((None, q, D_INNER), lambda bi, c: (bi, c, 0)),
                  pl.BlockSpec((None, q, SSD_GROUPS * SSD_STATE), lambda bi, c: (bi, c, nbc)),
                  pl.BlockSpec((None, q, SSD_GROUPS * SSD_STATE), lambda bi, c: (bi, c, nbc + 1)),
                  pl.BlockSpec((None, q, D_INNER), lambda bi, c: (bi, c, z_col0 // D_INNER)),
                  pl.BlockSpec((None, q, LANE), lambda bi, c: (bi, c, 0)),
                  pl.BlockSpec((None, LANE, q), lambda bi, c: (bi, 0, c)),
                  const((1, LANE)), const((LANE, 1)), const((1, LANE)), const((LANE, 1)),
                  const((1, D_INNER)), const((1, D_INNER)),
                  pl.BlockSpec((None, npair, LANE, SSD_STATE), lambda bi, c: (bi, 0, 0, 0)),
                  pl.BlockSpec((None, q, pre_w), lambda bi, c: (bi, c, 0))],
        out_specs=[pl.BlockSpec((None, q, pre_w + D_INNER), lambda bi, c: (bi, c, 0)),
                   pl.BlockSpec((None, npair, LANE, SSD_STATE), lambda bi, c: (bi, 0, 0, 0))],
        out_shape=[jax.ShapeDtypeStruct((b, t, pre_w + D_INNER), BF16),
                   jax.ShapeDtypeStruct((b, npair, LANE, SSD_STATE), F32)],
        scratch_shapes=[pltpu.VMEM((npair, LANE, SSD_STATE), F32)],
        compiler_params=_params("parallel", "arbitrary"),
        name="ssd_scan",
    )(conv, conv, conv, z_src, dt_raw, dtr, bias_c, bias_r, alog_c, alog_r, dskip, g_out.reshape(1, D_INNER), s0p,
      prefix)
    return y, sfin.reshape(b, SSD_HEADS, SSD_HEAD_DIM, SSD_STATE)


def _compress_kernel(pt_ref, a_ref, b_ref, cpe_a_ref, cpe_b_ref, b1_ref, w2_ref, b2_ref, g_ref, o_ref,
                     ha_ref, hb_ref, *, npg, norm):
    bi = pl.program_id(0)
    ncp = npg * 8

    def gather(p, _):
        src = pl.ds(pl.multiple_of(pt_ref[bi, p] * 8, 8), 8)
        dst = pl.ds(pl.multiple_of(p * 8, 8), 8)
        ha_ref[dst, :] = a_ref[src, :]
        hb_ref[dst, :] = b_ref[src, :]
        return 0

    lax.fori_loop(0, npg, gather, 0)
    hb_ref[ncp:ncp + 8, :] = jnp.zeros((8, HEAD_DIM), F32)
    hid = ha_ref[...] + hb_ref[pl.ds(1, ncp), :] + (cpe_a_ref[0:1, :] + cpe_b_ref[1:2, :] + b1_ref[...])
    out = _dot(_gelu_tanh(hid).astype(BF16), w2_ref[...]) + b2_ref[...]
    if norm:
        out = _rms(out, g_ref[...])
    o_ref[...] = out


ROWS_PER_TILE = 8


def _chunk_products_kernel(x_ref, w_ref, o_ref):
    @pl.when(pl.program_id(1) == 0)
    def _():
        o_ref[...] = jnp.zeros_like(o_ref)

    for r in range(ROWS_PER_TILE):
        g = r % NSA_KV
        prod = _dot(x_ref[:, r, :].astype(BF16), w_ref[r // NSA_KV])
        o_ref[:, g * HEAD_DIM:(g + 1) * HEAD_DIM] += prod[:, :HEAD_DIM]
        o_ref[:, KV_W + g * HEAD_DIM:KV_W + (g + 1) * HEAD_DIM] += prod[:, HEAD_DIM:]


def chunk_products(x, layer, w_pair):
    nck = x.shape[1]
    blk = min(nck, 1024)
    assert nck % blk == 0 and x.shape[2] == CMP_STRIDE * NSA_KV
    per_step = ROWS_PER_TILE // NSA_KV
    return pl.pallas_call(
        _chunk_products_kernel,
        grid=(nck // blk, CMP_STRIDE // per_step),
        in_specs=[pl.BlockSpec((None, blk, ROWS_PER_TILE, HEAD_DIM), lambda i, lp: (layer, i, lp, 0)),
                  pl.BlockSpec((per_step, HEAD_DIM, 2 * HEAD_DIM), lambda i, lp: (lp, 0, 0))],
        out_specs=pl.BlockSpec((blk, 2 * KV_W), lambda i, lp: (i, 0)),
        out_shape=jax.ShapeDtypeStruct((nck, 2 * KV_W), F32),
        compiler_params=_params("parallel", "arbitrary"),
        name="chunk_products",
    )(x, w_pair)


def compress_tokens(rows, layer, page_table, pe, w1, b1, w2, b2, g_norm):
    nck = rows.shape[1]
    b, npg = page_table.shape
    ncp = npg * 8
    w_pair = w1.reshape(2, CMP_STRIDE, HEAD_DIM, HEAD_DIM).transpose(1, 2, 0, 3).reshape(CMP_STRIDE, HEAD_DIM, 2 * HEAD_DIM)
    w_pair = w_pair.astype(BF16)
    ab = chunk_products(rows, layer, w_pair)
    pe_rows = jnp.broadcast_to(pe.reshape(2, CMP_STRIDE, 1, HEAD_DIM), (2, CMP_STRIDE, NSA_KV, HEAD_DIM))
    pe_rows = jnp.pad(pe_rows.reshape(1, 2, CMP_STRIDE * NSA_KV, HEAD_DIM), ((0, 0), (0, 6), (0, 0), (0, 0)))
    cpe = chunk_products(pe_rows, 0, w_pair)
    col = lambda off: (lambda bi, g, pt: (0, off + g))
    return pl.pallas_call(
        functools.partial(_compress_kernel, npg=npg, norm=g_norm is not None),
        grid_spec=pltpu.PrefetchScalarGridSpec(
            num_scalar_prefetch=1,
            grid=(b, NSA_KV),
            in_specs=[pl.BlockSpec((nck, HEAD_DIM), col(0)), pl.BlockSpec((nck, HEAD_DIM), col(NSA_KV)),
                      pl.BlockSpec((8, HEAD_DIM), col(0)), pl.BlockSpec((8, HEAD_DIM), col(NSA_KV)),
                      pl.BlockSpec((1, HEAD_DIM), lambda bi, g, pt: (0, 0)),
                      pl.BlockSpec((HEAD_DIM, HEAD_DIM), lambda bi, g, pt: (0, 0)),
                      pl.BlockSpec((1, HEAD_DIM), lambda bi, g, pt: (0, 0)),
                      pl.BlockSpec((1, HEAD_DIM), lambda bi, g, pt: (0, 0))],
            out_specs=pl.BlockSpec((None, ncp, HEAD_DIM), lambda bi, g, pt: (bi, 0, g)),
            scratch_shapes=[pltpu.VMEM((ncp, HEAD_DIM), F32), pltpu.VMEM((ncp + 8, HEAD_DIM), F32)],
        ),
        out_shape=jax.ShapeDtypeStruct((b, ncp, KV_W), F32),
        compiler_params=_params("parallel", "parallel"),
        name="nsa_compress",
    )(page_table, ab, ab, cpe, cpe, b1.reshape(1, -1), w2.astype(BF16), b2.reshape(1, -1),
      (g_norm if g_norm is not None else jnp.ones((HEAD_DIM,), F32)).reshape(1, -1))


def _nsa_cmp_kernel(q_ref, kc_ref, vc_ref, o_ref, sel_ref, *, tq, q_start, n_sel, nsp):
    i = pl.program_id(2)
    kc = kc_ref[...].astype(BF16)
    vc = vc_ref[...].astype(BF16)
    ncp = kc.shape[0]
    qpos = q_start + i * tq + _iota((tq, 1), 0)
    end = _iota((tq, ncp), 1) * CMP_STRIDE + (CMP_LEN - 1)
    msk = end <= qpos
    imp = jnp.zeros((tq, ncp), F32)
    for r in range(NSA_HPG):
        sl = slice(r * HEAD_DIM, (r + 1) * HEAD_DIM)
        s = jnp.where(msk, _dot_nt(q_ref[:, sl], kc) * SCALE, NEG)
        p = jnp.where(msk, jnp.exp(s - jnp.max(s, axis=1, keepdims=True)), 0.0)
        d = jnp.sum(p, axis=1, keepdims=True)
        p = p / jnp.where(d > 0.0, d, 1.0)
        o_ref[:, sl] = _dot(p.astype(BF16), vc)
        imp = imp + p
    cn = _iota((ncp, nsp), 0)
    sj = _iota((ncp, nsp), 1)
    overlap = jnp.where((cn >= 4 * sj - 1) & (cn <= 4 * sj + 3), 1.0, 0.0).astype(BF16)
    score = sum(_dot(part, overlap) for part in _split3(imp))
    j = _iota((tq, nsp), 1)
    qblk = qpos // SEL_BLOCK
    valid = j <= qblk
    forced = (j == 0) | (j == qblk) | (j == qblk - 1)
    score = jnp.where(valid, jnp.where(forced, jnp.inf, score), -jnp.inf)
    sel_ref[...] = jnp.where(valid & (_rank_before(score, n_sel) < min(SEL_TOPN, n_sel)), 1.0, 0.0)


def nsa_compressed(qn, kc, vc, q_start, n_sel):
    b, t, w = qn.shape
    ncp = kc.shape[1]
    gw = NSA_HPG * HEAD_DIM
    tq = min(t, 256)
    nsp = -(-n_sel // LANE) * LANE
    return pl.pallas_call(
        functools.partial(_nsa_cmp_kernel, tq=tq, q_start=q_start, n_sel=n_sel, nsp=nsp),
        grid=(b, NSA_KV, t // tq),
        in_specs=[pl.BlockSpec((None, tq, gw), lambda bi, g, i: (bi, i, g)),
                  pl.BlockSpec((None, ncp, HEAD_DIM), lambda bi, g, i: (bi, 0, g)),
                  pl.BlockSpec((None, ncp, HEAD_DIM), lambda bi, g, i: (bi, 0, g))],
        out_specs=[pl.BlockSpec((None, tq, gw), lambda bi, g, i: (bi, i, g)),
                   pl.BlockSpec((None, None, tq, nsp), lambda bi, g, i: (bi, g, i, 0))],
        out_shape=[jax.ShapeDtypeStruct((b, t, w), F32), jax.ShapeDtypeStruct((b, NSA_KV, t, nsp), F32)],
        compiler_params=_params("parallel", "parallel", "parallel"),
        name="nsa_compressed",
    )(qn, kc, vc)


def _nsa_prompt_kernel(q_ref, sk_ref, sv_ref, wk_ref, wv_ref, sel_ref, ocmp_ref, gl_ref, o_ref,
                       m_ref, part_ref, acc_ref, s_ref, *, tq, tk):
    i = pl.program_id(2)
    q0 = i * tq
    tpos = q0 + _iota((tq, tk), 0)
    sel = sel_ref[...].astype(BF16)
    nsp = sel.shape[1]

    def sel_mask(k0):
        kpos = k0 + _iota((tq, tk), 1)
        expand = jnp.where(_iota((nsp, tk), 0) == (k0 + _iota((nsp, tk), 1)) // SEL_BLOCK, 1.0, 0.0).astype(BF16)
        return (_dot(sel, expand) > 0.5) & (kpos <= tpos)

    def win_mask(k0):
        kpos = k0 + _iota((tq, tk), 1)
        return (kpos <= tpos) & (kpos > tpos - WINDOW)

    n_hi = (q0 + tq - 1) // tk + 1
    branches = ((sk_ref, sv_ref, sel_mask, 0), (wk_ref, wv_ref, win_mask, jnp.maximum(q0 - (WINDOW - 1), 0) // tk))
    fold = lambda a, op: functools.reduce(op, [a[:, c * LANE:(c + 1) * LANE] for c in range(tk // LANE)])
    for br, (k_ref, v_ref, mask_fn, n_lo) in enumerate(branches):
        base = br * NSA_HPG
        for r in range(NSA_HPG):
            part_ref[base + r] = jnp.full((tq, LANE), NEG, F32)
            acc_ref[base + r] = jnp.zeros((tq, HEAD_DIM), F32)

        def max_sweep(n, _, base=base, k_ref=k_ref, mask_fn=mask_fn, n_lo=n_lo):
            k0 = pl.multiple_of(n * tk, tk)
            kb, msk = k_ref[pl.ds(k0, tk), :].astype(BF16), mask_fn(k0)
            for r in range(NSA_HPG):
                s = jnp.where(msk, _dot_nt(q_ref[:, r * HEAD_DIM:(r + 1) * HEAD_DIM], kb), NEG)
                s_ref[n - n_lo, r] = s
                part_ref[base + r] = jnp.maximum(part_ref[base + r], fold(s, jnp.maximum))
            return 0

        lax.fori_loop(n_lo, n_hi, max_sweep, 0)
        for r in range(NSA_HPG):
            m_ref[base + r] = jnp.max(part_ref[base + r], axis=1, keepdims=True)
            part_ref[base + r] = jnp.zeros((tq, LANE), F32)

        def exp_sweep(n, _, base=base, v_ref=v_ref, n_lo=n_lo):
            vb = v_ref[pl.ds(pl.multiple_of(n * tk, tk), tk), :].astype(BF16)
            for r in range(NSA_HPG):
                p = jnp.exp2((s_ref[n - n_lo, r] - m_ref[base + r]) * EXP2_SCALE)
                part_ref[base + r] += fold(p, jnp.add)
                acc_ref[base + r] += _dot(p.astype(BF16), vb)
            return 0

        lax.fori_loop(n_lo, n_hi, exp_sweep, 0)

    gates = [_sigmoid(gl_ref[c]) for c in range(3)]
    for r in range(NSA_HPG):
        sl = slice(r * HEAD_DIM, (r + 1) * HEAD_DIM)
        o_sel = acc_ref[r] / jnp.sum(part_ref[r], axis=1, keepdims=True)
        o_win = acc_ref[NSA_HPG + r] / jnp.sum(part_ref[NSA_HPG + r], axis=1, keepdims=True)
        o = gates[0][:, r:r + 1] * ocmp_ref[:, sl] + gates[1][:, r:r + 1] * o_sel + gates[2][:, r:r + 1] * o_win
        o_ref[:, sl] = o.astype(o_ref.dtype)


def nsa_prompt(qn, skn, wkn, proj, sv_col0, wv_col0, selm, ocmp, gate_logits, tq=256, tk=256):
    b, t, w = qn.shape
    gw = NSA_HPG * HEAD_DIM
    tq, tk = min(t, tq), min(t, tk)
    nsp = selm.shape[-1]
    kv = lambda off: pl.BlockSpec((None, t, HEAD_DIM), lambda bi, g, i: (bi, 0, off + g))
    return pl.pallas_call(
        functools.partial(_nsa_prompt_kernel, tq=tq, tk=tk),
        grid=(b, NSA_KV, t // tq),
        in_specs=[pl.BlockSpec((None, tq, gw), lambda bi, g, i: (bi, i, g)),
                  kv(0), kv(sv_col0 // HEAD_DIM), kv(0), kv(wv_col0 // HEAD_DIM),
                  pl.BlockSpec((None, None, tq, nsp), lambda bi, g, i: (bi, g, i, 0)),
                  pl.BlockSpec((None, tq, gw), lambda bi, g, i: (bi, i, g)),
                  pl.BlockSpec((3, None, None, tq, NSA_HPG), lambda bi, g, i: (0, bi, g, i, 0))],
        out_specs=pl.BlockSpec((None, tq, gw), lambda bi, g, i: (bi, i, g)),
        out_shape=jax.ShapeDtypeStruct((b, t, w), BF16),
        scratch_shapes=[pltpu.VMEM((2 * NSA_HPG, tq, 1), F32), pltpu.VMEM((2 * NSA_HPG, tq, LANE), F32),
                        pltpu.VMEM((2 * NSA_HPG, tq, HEAD_DIM), F32), pltpu.VMEM((t // tk, NSA_HPG, tq, tk), F32)],
        compiler_params=_params("parallel", "parallel", "arbitrary"),
        name="nsa_prompt",
    )(qn, skn, proj, wkn, proj, selm, ocmp, gate_logits)


def _nsa_dec_kernel(pt_ref, q_ref, sel_ref, ocmp_ref, gl_ref, kp_ref, vp_ref, skn_ref, svn_ref,
                    wkb_ref, wvb_ref, wkn_ref, wvn_ref, o_ref, m_ref, l_ref, acc_ref, *, tq, past_len):
    p = pl.program_id(1)
    npages = pl.num_programs(1)
    rows = NSA_KV * NSA_HPG * tq
    per_g = NSA_HPG * tq
    q = q_ref[...]
    sel = sel_ref[...]

    @pl.when(p == 0)
    def _():
        m_ref[...] = jnp.full(m_ref.shape, NEG, F32)
        l_ref[...] = jnp.zeros_like(l_ref)
        acc_ref[...] = jnp.zeros_like(acc_ref)

    def grid_of(n_keys):
        shape = (rows, n_keys * NSA_KV)
        lane, row = _iota(shape, 1), _iota(shape, 0)
        return lane // NSA_KV, lane % NSA_KV == row // per_g, row % tq

    def update(carry, k_rows, v_rows, msk):
        s = _dot_nt(q, k_rows.astype(BF16)) * SCALE
        return _online_update(carry, s, msk, v_rows.astype(BF16))

    key, same_g, _ = grid_of(PAGE)
    blocks_per_page = PAGE // SEL_BLOCK
    picked = jnp.where(key < SEL_BLOCK, _pick_col(sel, blocks_per_page * p), _pick_col(sel, blocks_per_page * p + 1))
    carry = update((m_ref[...], l_ref[...], acc_ref[...]), kp_ref[...], vp_ref[...], same_g & (picked > 0.5))
    m_ref[...], l_ref[...], acc_ref[...] = carry

    @pl.when(p == npages - 1)
    def _():
        n_new = skn_ref.shape[0] // NSA_KV
        key, same_g, t_row = grid_of(n_new)
        own = same_g & (key <= t_row) & (key < tq)
        new_blk = _pick_col(sel, past_len // SEL_BLOCK) > 0.5
        o_sel = _flash_out(update(carry, skn_ref[...], svn_ref[...], own & new_blk))
        wkey, wsame, wt = grid_of(WINDOW)
        w_carry = update(_flash_init(rows, HEAD_DIM), wkb_ref[...], wvb_ref[...], wsame & (wkey > wt))
        o_win = _flash_out(update(w_carry, wkn_ref[...], wvn_ref[...], own))
        gates = _sigmoid(gl_ref[...])
        o_ref[...] = gates[:, 0:1] * ocmp_ref[...] + gates[:, 1:2] * o_sel + gates[:, 2:3] * o_win


def nsa_sample(qn, skn_new, sv_new, wkn_new, wv_new, cache_sk, cache_sv, win_k, win_v, layer,
               page_table, selm, ocmp, gate_logits):
    b, tq, w = qn.shape
    rows = NSA_KV * NSA_HPG * tq
    npages = page_table.shape[1]
    past_len = npages * PAGE
    nsp = selm.shape[-1]
    assert win_k.shape[2] == WINDOW * NSA_KV and past_len % SEL_BLOCK == 0 and tq * NSA_KV <= LANE
    selrows = jnp.broadcast_to(selm[:, :, None], (b, NSA_KV, NSA_HPG, tq, nsp)).reshape(b, rows, nsp)
    to_rows = lambda a: a.reshape(b, tq, NSA_KV, NSA_HPG, -1).transpose(0, 2, 3, 1, 4).reshape(b, rows, -1)
    new_rows = lambda a: jnp.pad(a.reshape(b, tq * NSA_KV, HEAD_DIM), ((0, 0), (0, LANE - tq * NSA_KV), (0, 0)))
    per_b = lambda *shape: pl.BlockSpec((None,) + shape, lambda bi, p, pt: (bi,) + (0,) * len(shape))
    page = pl.BlockSpec((None, None, PAGE * NSA_KV, HEAD_DIM), lambda bi, p, pt: (layer, pt[bi, p], 0, 0))
    wbuf = pl.BlockSpec((None, None, WINDOW * NSA_KV, HEAD_DIM), lambda bi, p, pt: (layer, bi, 0, 0))
    o = pl.pallas_call(
        functools.partial(_nsa_dec_kernel, tq=tq, past_len=past_len),
        grid_spec=pltpu.PrefetchScalarGridSpec(
            num_scalar_prefetch=1,
            grid=(b, npages),
            in_specs=[per_b(rows, HEAD_DIM), per_b(rows, nsp), per_b(rows, HEAD_DIM), per_b(rows, 3),
                      page, page, per_b(LANE, HEAD_DIM), per_b(LANE, HEAD_DIM), wbuf, wbuf,
                      per_b(LANE, HEAD_DIM), per_b(LANE, HEAD_DIM)],
            out_specs=per_b(rows, HEAD_DIM),
            scratch_shapes=[pltpu.VMEM((rows, 1), F32), pltpu.VMEM((rows, 1), F32), pltpu.VMEM((rows, HEAD_DIM), F32)],
        ),
        out_shape=jax.ShapeDtypeStruct((b, rows, HEAD_DIM), F32),
        compiler_params=_params("parallel", "arbitrary"),
        name="nsa_sample",
    )(page_table, to_rows(qn), selrows, to_rows(ocmp), to_rows(gate_logits), cache_sk, cache_sv,
      new_rows(skn_new), new_rows(sv_new), win_k, win_v, new_rows(wkn_new), new_rows(wv_new))
    return o.reshape(b, NSA_KV, NSA_HPG, tq, HEAD_DIM).transpose(0, 3, 1, 2, 4).reshape(b, tq, w)


MOBA_W = MOBA_HEADS * HEAD_DIM
CONV_DIM = D_INNER + 2 * SSD_GROUPS * SSD_STATE
EVEN_MAIN = 3 * MOBA_W + D_INNER + CONV_DIM
Z_COL0 = 0
XBC_COL0 = D_INNER
Q_COL0 = XBC_COL0 + CONV_DIM
K_COL0 = Q_COL0 + MOBA_W
V_COL0 = K_COL0 + MOBA_W
NSA_Q = NSA_KV * NSA_HPG * HEAD_DIM
ODD_MAIN = NSA_Q + 6 * KV_W
D_FF = 8192


PROJ_TN = 512


def _ws(a, w, layer, res=None, **kw):
    return "ws", a, w, layer, res, kw


def _ffn_up(a, seq_len, hist8, w, layer, conv_w, conv_b):
    return "ffn", a, seq_len, hist8, w, layer, conv_w, conv_b


def _run_pair(big, small):
    req_big, req_small = next(big), next(small)
    while True:
        assert req_big[0] == req_small[0]
        if req_big[0] == "ws":
            _, a, w, layer, res, kw = req_big
            out_big, out_small = matmul_ws(a, w, layer, res=res, small=req_small[1], small_res=req_small[4], **kw)
        else:
            _, a, seq_len, _, w, layer, conv_w, conv_b = req_big
            out_big, out_small = ffn_up_glu(a, seq_len, req_small[1], req_small[2], req_small[3], w, layer, conv_w, conv_b)
        try:
            req_big = big.send(out_big)
        except StopIteration as done_big:
            try:
                small.send(out_small)
            except StopIteration as done_small:
                return done_big.value, done_small.value
            raise AssertionError("the two groups must issue the same matmul sequence")
        req_small = small.send(out_small)


def _proj(h2d, w, layer, main, out_cols=None):
    wt = jnp.swapaxes(w, 1, 2)
    tail = jnp.pad(wt[layer, main:, :], ((0, LANE - (w.shape[2] - main)), (0, 0)))
    main_out = yield _ws(h2d, wt, layer, k=h2d.shape[1], n=main, out_cols=out_cols, w_t=True, tn=PROJ_TN)
    return main_out, matmul(h2d, tail, b_t=True)


def _even_layer(x, sample, q_start, cache_k, cache_v, e, page_table, conv_prev, ssm_prev,
                g_norm, w_in, w_out, qk_g, conv_w, conv_b, dt_bias, a_log, d_skip, g_out):
    b, t, d = x.shape
    x2 = x.reshape(b * t, d)
    h = rmsnorm_rows(x2, g_norm)
    n_tiles = EVEN_MAIN // PROJ_TN
    proj, dt_raw = yield from _proj(h, w_in, e, EVEN_MAIN, out_cols=lambda j: (j + Q_COL0 // PROJ_TN) % n_tiles)
    proj = proj.reshape(b, t, EVEN_MAIN)
    dt_raw = dt_raw.reshape(b, t, LANE)
    qn = head_norm(proj, Q_COL0, MOBA_HEADS, qk_g[0], BF16)
    kn = head_norm(proj, K_COL0, MOBA_HEADS, qk_g[1], F32)
    v = proj[..., V_COL0:]
    if sample:
        o_attn = moba_sample(qn, kn, v, cache_k, cache_v, e, page_table).astype(BF16)
    else:
        o_attn = moba_prompt(qn, kn, proj, V_COL0)
    conv = dwconv_silu(proj, XBC_COL0, CONV_DIM, _history8(conv_prev, b, CONV_DIM), conv_w, conv_b)
    xbc = proj[..., XBC_COL0:Q_COL0]
    conv_state = jnp.concatenate([_history8(conv_prev, b, CONV_DIM), xbc], axis=1)[:, -(SSD_CONV - 1):]
    if t % SSD_CHUNK:
        tp = -(-t // SSD_CHUNK) * SSD_CHUNK
        padt = lambda a: jnp.pad(a, ((0, 0), (0, tp - t), (0, 0)))
        mixed, ssm = ssd_mixer(padt(conv), padt(proj[..., Z_COL0:XBC_COL0]), 0, padt(dt_raw), ssm_prev,
                               dt_bias, a_log, d_skip, g_out, t, padt(o_attn))
        mixed = mixed[:, :t]
    else:
        mixed, ssm = ssd_mixer(conv, proj, Z_COL0, dt_raw, ssm_prev, dt_bias, a_log, d_skip, g_out, None, o_attn)
    mixed = mixed.reshape(b * t, MOBA_W + D_INNER)
    x2 = yield _ws(mixed, w_out, e, res=x2, k=MOBA_W + D_INNER, n=d, tk=MOBA_W)
    heads = lambda a: a.reshape(b, t, MOBA_HEADS, HEAD_DIM)
    return x2.reshape(b, t, d), heads(kn), heads(v), conv_state, ssm


def _odd_layer(x, sample, q_start, caches, e, page_table, win_k, win_v,
               g_norm, w_in, w_out, qk_g, pe, w1, b1, w2, b2):
    b, t, d = x.shape
    x2 = x.reshape(b * t, d)
    h = rmsnorm_rows(x2, g_norm)
    proj, gl = yield from _proj(h, w_in, e, ODD_MAIN)
    proj = proj.reshape(b, t, ODD_MAIN)
    gl = gl[:, :3 * NSA_KV * NSA_HPG].reshape(b, t, NSA_KV, NSA_HPG, 3)
    col = lambda i: NSA_Q + i * KV_W
    qn = head_norm(proj, 0, NSA_KV * NSA_HPG, qk_g[0], BF16)
    skn = head_norm(proj, col(2), NSA_KV, qk_g[2], F32)
    wkn = head_norm(proj, col(4), NSA_KV, qk_g[3], F32)
    ck, cv, sv, wv = (proj[..., col(i):col(i + 1)] for i in (0, 1, 3, 5))
    if sample:
        cache_ck, cache_cv, cache_sk, cache_sv = caches
        assert t < CMP_STRIDE
        chunks = lambda c: c.reshape(c.shape[0], -1, CMP_STRIDE * NSA_KV, HEAD_DIM)
        rows_k, rows_v, pt, src = chunks(cache_ck), chunks(cache_cv), page_table, e
        n_rows = page_table.shape[1] * PAGE + t
    else:
        chunks = lambda a: a.reshape(1, b * t // CMP_STRIDE, CMP_STRIDE * NSA_KV, HEAD_DIM)
        rows_k, rows_v, src = chunks(ck), chunks(cv), 0
        npg = t // PAGE
        pt = (jnp.arange(b, dtype=jnp.int32)[:, None] * npg + jnp.arange(npg, dtype=jnp.int32)[None, :])
        n_rows = t
    kc = compress_tokens(rows_k, src, pt, pe[0], w1[0], b1[0], w2[0], b2[0], qk_g[1])
    vc = compress_tokens(rows_v, src, pt, pe[1], w1[1], b1[1], w2[1], b2[1], None)
    n_sel = -(-n_rows // SEL_BLOCK)
    ocmp, selm = nsa_compressed(qn, kc, vc, q_start, n_sel)
    if sample:
        rows_view = lambda c: c.reshape(c.shape[0], c.shape[1], c.shape[2] * NSA_KV, HEAD_DIM)
        o = nsa_sample(qn, skn, sv, wkn, wv, rows_view(cache_sk), rows_view(cache_sv), rows_view(win_k),
                       rows_view(win_v), e, page_table, selm, ocmp, gl).astype(BF16)
        wk_all = jnp.concatenate([win_k[e].reshape(b, -1, KV_W), wkn], axis=1)
        wv_all = jnp.concatenate([win_v[e].reshape(b, -1, KV_W), wv], axis=1)
    else:
        o = nsa_prompt(qn, skn, wkn, proj, col(3), col(5), selm, ocmp, gl.transpose(4, 0, 2, 1, 3))
        wk_all, wv_all = wkn, wv
    x2 = yield _ws(o.reshape(b * t, NSA_Q), w_out, e, res=x2, k=NSA_Q, n=d)
    keep = min(WINDOW, wk_all.shape[1])
    kvh = lambda a: a.reshape(b, a.shape[1], NSA_KV, HEAD_DIM)
    return (x2.reshape(b, t, d), kvh(ck), kvh(cv), kvh(skn), kvh(sv),
            kvh(wk_all[:, wk_all.shape[1] - keep:]), kvh(wv_all[:, wv_all.shape[1] - keep:]))


def _conv_ffn(x, conv_prev, li, g_norm, w_up, conv_w, conv_b, w_down):
    b, t, d = x.shape
    x2 = x.reshape(b * t, d)
    hist = None if conv_prev is None else _history8(conv_prev, b, 2 * D_FF)
    act, last8 = yield _ffn_up(rmsnorm_rows(x2, g_norm), t, hist, w_up, li, conv_w, conv_b)
    x2 = yield _ws(act, w_down, li, res=x2, k=D_FF, n=d)
    return x2.reshape(b, t, d), last8[:, -(FFN_CONV - 1):]


def kernel(x_prompt, x_sample, cache_moba_k, cache_moba_v, state_ssd, state_ssd_conv, cache_nsa_cmp_k, cache_nsa_cmp_v, cache_nsa_sel_k, cache_nsa_sel_v, state_nsa_win_k, state_nsa_win_v, state_ffn_conv, page_table, norm_mix, norm_ffn, even_w_in, even_w_out, moba_qk_norm, ssd_conv_w, ssd_conv_b, ssd_dt_bias, ssd_a_log, ssd_d, ssd_norm, odd_w_in, odd_w_out, nsa_qk_norm, cmp_pe, cmp_w1, cmp_b1, cmp_w2, cmp_b2, ffn_w_up, ffn_conv_w, ffn_conv_b, ffn_w_down):
    depth = norm_mix.shape[0]
    past_len = page_table.shape[1] * PAGE
    state_keys = ("moba_k", "moba_v", "ssd", "ssd_conv", "nsa_cmp_k", "nsa_cmp_v",
                  "nsa_sel_k", "nsa_sel_v", "nsa_win_k", "nsa_win_v", "ffn_conv")
    moba_pool = lambda c: c.reshape(c.shape[0], c.shape[1], PAGE * MOBA_HEADS, HEAD_DIM)
    cache_mk, cache_mv = moba_pool(cache_moba_k), moba_pool(cache_moba_v)

    def trunk(x, q_start, sample):
        b = x.shape[0]
        new = {name: [] for name in state_keys}
        for li in range(depth):
            e = li // 2
            if li % 2 == 0:
                cp = state_ssd_conv[e] if sample else None
                sp = state_ssd[e] if sample else jnp.zeros((b, SSD_HEADS, SSD_HEAD_DIM, SSD_STATE), F32)
                x, k, v, cs, ss = yield from _even_layer(x, sample, q_start, cache_mk, cache_mv, e, page_table, cp, sp,
                                                         norm_mix[li], even_w_in, even_w_out, moba_qk_norm[e],
                                                         ssd_conv_w[e], ssd_conv_b[e], ssd_dt_bias[e], ssd_a_log[e],
                                                         ssd_d[e], ssd_norm[e])
                for name, val in zip(("moba_k", "moba_v", "ssd_conv", "ssd"), (k, v, cs, ss)):
                    new[name].append(val)
            else:
                caches = (cache_nsa_cmp_k, cache_nsa_cmp_v, cache_nsa_sel_k, cache_nsa_sel_v)
                outs = yield from _odd_layer(x, sample, q_start, caches, e, page_table, state_nsa_win_k,
                                             state_nsa_win_v, norm_mix[li], odd_w_in, odd_w_out, nsa_qk_norm[e],
                                             cmp_pe[e], cmp_w1[e], cmp_b1[e], cmp_w2[e], cmp_b2[e])
                x = outs[0]
                for name, val in zip(("nsa_cmp_k", "nsa_cmp_v", "nsa_sel_k", "nsa_sel_v", "nsa_win_k", "nsa_win_v"), outs[1:]):
                    new[name].append(val)
            fp = state_ffn_conv[li] if sample else None
            x, fs = yield from _conv_ffn(x, fp, li, norm_ffn[li], ffn_w_up, ffn_conv_w[li], ffn_conv_b[li], ffn_w_down)
            new["ffn_conv"].append(fs)
        return x, {name: jnp.stack(rows) for name, rows in new.items()}

    (y_prompt, sp), (y_sample, ss) = _run_pair(trunk(x_prompt, 0, False), trunk(x_sample, past_len, True))
    out = [y_prompt, y_sample]
    for name in state_keys:
        out += [sp[name], ss[name]]
    return tuple(out)
```

```python
import functools

import numpy as np
import jax
import jax.numpy as jnp
from jax import lax
from jax.experimental import pallas as pl
from jax.experimental.pallas import tpu as pltpu

F32 = jnp.float32
BF16 = jnp.bfloat16

LANE = 128
VMEM_LIMIT_BYTES = 56 * 1024 * 1024

EPS = 1e-6
HEAD_DIM = 128
SCALE = HEAD_DIM ** -0.5
EXP2_SCALE = SCALE * float(np.log2(np.e))
NEG = -1e30

PAGE = 128
MOBA_HEADS = 16
MOBA_BLOCK = 256
MOBA_TOPK = 3
SSD_HEADS = 64
SSD_HEAD_DIM = 64
SSD_GROUPS = 8
SSD_STATE = 128
SSD_CONV = 4
SSD_CHUNK = 128
D_INNER = 4096
GROUP_W = D_INNER // SSD_GROUPS
HEADS_PER_GROUP = SSD_HEADS // SSD_GROUPS
NSA_KV = 4
NSA_HPG = 8
KV_W = NSA_KV * HEAD_DIM
CMP_LEN = 32
CMP_STRIDE = 16
SEL_BLOCK = 64
SEL_TOPN = 16
WINDOW = 512
FFN_CONV = 3


def _params(*sem):
    return pltpu.CompilerParams(dimension_semantics=sem, vmem_limit_bytes=VMEM_LIMIT_BYTES)


def _dot(a, b):
    return jnp.dot(a, b, preferred_element_type=F32)


def _dot_nt(a, b):
    return lax.dot_general(a, b, (((1,), (1,)), ((), ())), preferred_element_type=F32)


def _dot_tn(a, b):
    return lax.dot_general(a, b, (((0,), (0,)), ((), ())), preferred_element_type=F32)


def _split3(x):
    hi = x.astype(BF16)
    r1 = x - hi.astype(F32)
    mid = r1.astype(BF16)
    lo = (r1 - mid.astype(F32)).astype(BF16)
    return hi, mid, lo


def _sigmoid(x):
    return 1.0 / (1.0 + jnp.exp(-x))


def _silu(x):
    return x * _sigmoid(x)


def _softplus(x):
    return jnp.maximum(x, 0.0) + jnp.log1p(jnp.exp(-jnp.abs(x)))


def _gelu_tanh(x):
    return 0.5 * x * (1.0 + jnp.tanh(np.sqrt(2.0 / np.pi).astype(np.float32) * (x + 0.044715 * (x * x * x))))


def _rms(x, g):
    return x * lax.rsqrt(jnp.mean(x * x, axis=-1, keepdims=True) + EPS) * g


def _iota(shape, axis):
    return lax.broadcasted_iota(jnp.int32, shape, axis)


def _rank_before(score, n_cols):
    col = _iota(score.shape, 1)
    rank = jnp.zeros(score.shape, F32)
    for c in range(n_cols):
        sc = score[:, c:c + 1]
        rank = rank + jnp.where((sc > score) | ((sc == score) & (c < col)), 1.0, 0.0)
    return rank


def _pick_col(x, c):
    return jnp.sum(jnp.where(_iota(x.shape, 1) == c, x, 0.0), axis=1, keepdims=True)


def _online_update(carry, s, msk, v_bf16):
    m, l, acc = carry
    s = jnp.where(msk, s, NEG)
    m_new = jnp.maximum(m, jnp.max(s, axis=1, keepdims=True))
    p = jnp.where(msk, jnp.exp(s - m_new), 0.0)
    alpha = jnp.exp(m - m_new)
    l = alpha * l + jnp.sum(p, axis=1, keepdims=True)
    acc = alpha * acc + _dot(p.astype(BF16), v_bf16)
    return m_new, l, acc


def _flash_init(rows, dv):
    return (jnp.full((rows, 1), NEG, F32), jnp.zeros((rows, 1), F32), jnp.zeros((rows, dv), F32))


def _flash_out(carry):
    _, l, acc = carry
    return acc / jnp.where(l > 0.0, l, 1.0)


def _rmsnorm_kernel(x_ref, g_ref, o_ref):
    o_ref[...] = _rms(x_ref[...], g_ref[...]).astype(o_ref.dtype)


def rmsnorm_rows(x2d, g):
    m, d = x2d.shape
    tm = min(m, 256)
    return pl.pallas_call(
        _rmsnorm_kernel,
        grid=(m // tm,),
        in_specs=[pl.BlockSpec((tm, d), lambda i: (i, 0)), pl.BlockSpec((1, d), lambda i: (0, 0))],
        out_specs=pl.BlockSpec((tm, d), lambda i: (i, 0)),
        out_shape=jax.ShapeDtypeStruct((m, d), BF16),
        compiler_params=_params("parallel"),
        name="rmsnorm_rows",
    )(x2d, g.reshape(1, d))


def _matmul_kernel(*refs, nk, has_res, b_t):
    if has_res:
        a_ref, b_ref, r_ref, o_ref = refs
    else:
        a_ref, b_ref, o_ref = refs
    part = (_dot_nt if b_t else _dot)(a_ref[...].astype(BF16), b_ref[...].astype(BF16))
    if nk == 1:
        o_ref[...] = part + r_ref[...] if has_res else part
        return
    k = pl.program_id(2)

    @pl.when(k == 0)
    def _():
        o_ref[...] = part + r_ref[...] if has_res else part

    @pl.when(k > 0)
    def _():
        o_ref[...] += part


def matmul(a, b, res=None, b_t=False, tm=1024, tn=512, tk=4096):
    m, k = a.shape
    n = b.shape[0] if b_t else b.shape[1]
    tm, tn, tk = min(tm, m), min(tn, n), min(tk, k)
    assert m % tm == 0 and n % tn == 0 and k % tk == 0, (a.shape, b.shape)
    nk = k // tk
    b_spec = pl.BlockSpec((tn, tk), lambda i, j, kk: (j, kk)) if b_t else pl.BlockSpec((tk, tn), lambda i, j, kk: (kk, j))
    in_specs = [pl.BlockSpec((tm, tk), lambda i, j, kk: (i, kk)), b_spec]
    args = [a, b]
    if res is not None:
        in_specs.append(pl.BlockSpec((tm, tn), lambda i, j, kk: (i, j)))
        args.append(res)
    return pl.pallas_call(
        functools.partial(_matmul_kernel, nk=nk, has_res=res is not None, b_t=b_t),
        grid=(m // tm, n // tn, nk),
        in_specs=in_specs,
        out_specs=pl.BlockSpec((tm, tn), lambda i, j, kk: (i, j)),
        out_shape=jax.ShapeDtypeStruct((m, n), F32),
        compiler_params=_params("parallel", "parallel", "arbitrary"),
        name="matmul",
    )(*args)


def _matmul_ws_kernel(*refs, nk, has_res, has_small, w_t):
    refs = list(refs)
    a_ref, w_ref = refs.pop(0), refs.pop(0)
    r_ref = refs.pop(0) if has_res else None
    a2_ref = refs.pop(0) if has_small else None
    r2_ref = refs.pop(0) if has_small and has_res else None
    o_ref = refs.pop(0)
    o2_ref = refs.pop(0) if has_small else None
    wbf_ref, = refs
    i = pl.program_id(1)
    kk = pl.program_id(2)
    mm = _dot_nt if w_t else _dot

    def accumulate(out_ref, part, res_ref):
        if nk == 1:
            out_ref[...] = part + res_ref[...] if has_res else part
            return

        @pl.when(kk == 0)
        def _():
            out_ref[...] = part + res_ref[...] if has_res else part

        @pl.when(kk > 0)
        def _():
            out_ref[...] += part

    @pl.when(i == 0)
    def _():
        wbf_ref[kk] = w_ref[...].astype(BF16)
        if has_small:
            accumulate(o2_ref, mm(a2_ref[...], wbf_ref[kk]), r2_ref)

    accumulate(o_ref, mm(a_ref[...], wbf_ref[kk]), r_ref)


def matmul_ws(a, w, layer, *, k, n, res=None, small=None, small_res=None, out_cols=None, w_t=False,
              tm=1024, tn=512, tk=4096):
    m = a.shape[0]
    tm, tn, tk = min(tm, m), min(tn, n), min(tk, k)
    assert m % tm == 0 and n % tn == 0 and k % tk == 0, (a.shape, w.shape, k, n)
    assert (res is None) == (small_res is None) or small is None
    nk = k // tk
    k_idx = (lambda i, kk: 0) if nk == 1 else (lambda i, kk: jnp.where(i == 0, kk, nk - 1))
    if w_t:
        w_spec = pl.BlockSpec((None, tn, tk), lambda j, i, kk: (layer, j, k_idx(i, kk)))
    else:
        w_spec = pl.BlockSpec((None, tk, tn), lambda j, i, kk: (layer, k_idx(i, kk), j))
    col = (lambda j: j) if out_cols is None else out_cols
    o_spec = pl.BlockSpec((tm, tn), lambda j, i, kk: (i, col(j)))
    in_specs = [pl.BlockSpec((tm, tk), lambda j, i, kk: (i, kk)), w_spec]
    args = [a, w]
    out_specs, out_shape = [o_spec], [jax.ShapeDtypeStruct((m, n), F32)]
    if res is not None:
        in_specs.append(o_spec)
        args.append(res)
    if small is not None:
        m2 = small.shape[0]
        o2_spec = pl.BlockSpec((m2, tn), lambda j, i, kk: (0, col(j)))
        in_specs.append(pl.BlockSpec((m2, tk), lambda j, i, kk: (0, k_idx(i, kk))))
        args.append(small)
        if res is not None:
            in_specs.append(o2_spec)
            args.append(small_res)
        out_specs.append(o2_spec)
        out_shape.append(jax.ShapeDtypeStruct((m2, n), F32))
    outs = pl.pallas_call(
        functools.partial(_matmul_ws_kernel, nk=nk, has_res=res is not None, has_small=small is not None, w_t=w_t),
        grid=(n // tn, m // tm, nk),
        in_specs=in_specs,
        out_specs=out_specs,
        out_shape=out_shape,
        scratch_shapes=[pltpu.VMEM((nk, tn, tk) if w_t else (nk, tk, tn), BF16)],
        compiler_params=_params("arbitrary", "arbitrary", "arbitrary"),
        name="matmul_ws",
    )(*args)
    return outs[0] if small is None else tuple(outs)


def _head_norm_kernel(x_ref, g_ref, o_ref, *, heads):
    g = g_ref[...]
    for h in range(heads):
        sl = slice(h * HEAD_DIM, (h + 1) * HEAD_DIM)
        o_ref[:, sl] = _rms(x_ref[:, sl], g).astype(o_ref.dtype)


def head_norm(x, col0, n_heads, g, out_dtype):
    b, t, _ = x.shape
    hb = min(n_heads, 4)
    w = hb * HEAD_DIM
    assert col0 % w == 0 and n_heads % hb == 0
    tq = min(t, 512)
    return pl.pallas_call(
        functools.partial(_head_norm_kernel, heads=hb),
        grid=(b, t // tq, n_heads // hb),
        in_specs=[pl.BlockSpec((None, tq, w), lambda bi, i, j: (bi, i, col0 // w + j)),
                  pl.BlockSpec((1, HEAD_DIM), lambda bi, i, j: (0, 0))],
        out_specs=pl.BlockSpec((None, tq, w), lambda bi, i, j: (bi, i, j)),
        out_shape=jax.ShapeDtypeStruct((b, t, n_heads * HEAD_DIM), out_dtype),
        compiler_params=_params("parallel", "parallel", "parallel"),
        name="head_norm",
    )(x, g.reshape(1, HEAD_DIM))


CONV_ROWS = 256


def _dwconv_kernel(*refs, width, t, glu):
    if glu:
        xa_ref, xg_ref, pa_ref, pg_ref, wa_ref, wg_ref, ba_ref, bg_ref, o_ref, sa_ref, sg_ref = refs
        pairs = ((xa_ref, pa_ref, sa_ref), (xg_ref, pg_ref, sg_ref))
    else:
        xa_ref, pa_ref, wa_ref, ba_ref, o_ref, sa_ref = refs
        pairs = ((xa_ref, pa_ref, sa_ref),)
    for x_ref, p_ref, s_ref in pairs:
        s_ref[0:8, :] = p_ref[...]
        s_ref[8:8 + t, :] = x_ref[...]
    tc = min(t, CONV_ROWS)

    def conv(s_ref, w_ref, b_ref, r0):
        acc = b_ref[...]
        for i in range(width):
            acc = acc + w_ref[i:i + 1, :] * s_ref[pl.ds(8 - (width - 1) + i + r0, tc), :]
        return acc

    for c in range(t // tc):
        r0 = c * tc
        a = conv(sa_ref, wa_ref, ba_ref, r0)
        if glu:
            o_ref[r0:r0 + tc, :] = (a * _silu(conv(sg_ref, wg_ref, bg_ref, r0))).astype(o_ref.dtype)
        else:
            o_ref[r0:r0 + tc, :] = _silu(a).astype(o_ref.dtype)


def dwconv(x, col0, c_out, prev8, w, bias, glu, out_dtype):
    b, t, _ = x.shape
    width = w.shape[0]
    tc = min(256 if t > 64 else 2048, c_out)
    assert col0 % tc == 0 and c_out % tc == 0
    nj = c_out // tc
    x_spec = lambda off: pl.BlockSpec((None, t, tc), lambda bi, j: (bi, 0, off + j))
    p_spec = lambda off: pl.BlockSpec((None, 8, tc), lambda bi, j: (bi, 0, off + j))
    w_spec = lambda off: pl.BlockSpec((width, tc), lambda bi, j: (0, off + j))
    b_spec = lambda off: pl.BlockSpec((1, tc), lambda bi, j: (0, off + j))
    bias2 = bias.reshape(1, -1)
    if glu:
        in_specs = [x_spec(col0 // tc), x_spec(col0 // tc + nj), p_spec(0), p_spec(nj),
                    w_spec(0), w_spec(nj), b_spec(0), b_spec(nj)]
        args = (x, x, prev8, prev8, w, w, bias2, bias2)
        scratch = [pltpu.VMEM((8 + t, tc), F32), pltpu.VMEM((8 + t, tc), F32)]
    else:
        in_specs = [x_spec(col0 // tc), p_spec(0), w_spec(0), b_spec(0)]
        args = (x, prev8, w, bias2)
        scratch = [pltpu.VMEM((8 + t, tc), F32)]
    return pl.pallas_call(
        functools.partial(_dwconv_kernel, width=width, t=t, glu=glu),
        grid=(b, nj),
        in_specs=in_specs,
        out_specs=pl.BlockSpec((None, t, tc), lambda bi, j: (bi, 0, j)),
        out_shape=jax.ShapeDtypeStruct((b, t, c_out), out_dtype),
        scratch_shapes=scratch,
        compiler_params=_params("parallel", "parallel"),
        name="dwconv_glu" if glu else "dwconv_silu",
    )(*args)


def _history8(state, b, c):
    if state is None:
        return jnp.zeros((b, 8, c), F32)
    return jnp.concatenate([jnp.zeros((b, 8 - state.shape[1], c), F32), state], axis=1)


def _moba_tile(i, nb, q_ref, o_ref, kmean_ref, kbf_ref, vbf_ref):
    blk, half = MOBA_BLOCK, MOBA_BLOCK // 2
    fold = lambda a, op: functools.reduce(op, [a[:, c * LANE:(c + 1) * LANE] for c in range(a.shape[1] // LANE)])
    k_blk = lambda n: kbf_ref[n * blk:(n + 1) * blk, :]
    v_blk = lambda n: vbf_ref[n * blk:(n + 1) * blk, :]
    spread = jnp.where(_iota((LANE, max(i, 1) * LANE), 0) == _iota((LANE, max(i, 1) * LANE), 1) // LANE, 1.0, 0.0).astype(BF16)
    for hq in range(2):
        q = q_ref[hq * half:(hq + 1) * half, :]
        masks = []
        if i > 0:
            gate = _dot_nt(q, kmean_ref[...].astype(BF16))
            valid = _iota(gate.shape, 1) < i
            gate = jnp.where(valid, gate, -jnp.inf)
            sel = jnp.where(valid & (_rank_before(gate, i) < min(MOBA_TOPK, nb)), 1.0, 0.0).astype(BF16)
            selb = _dot(sel, spread)
            masks = [jnp.concatenate([selb[:, n * LANE:(n + 1) * LANE]] * (blk // LANE), axis=1) > 0.5 for n in range(i)]
        causal = _iota((half, blk), 1) <= _iota((half, blk), 0) + hq * half
        masks.append(causal)
        scores = lambda n: jnp.where(masks[n], _dot_nt(q, k_blk(n)), NEG)
        part = functools.reduce(jnp.maximum, [fold(scores(n), jnp.maximum) for n in range(i + 1)])
        m = jnp.max(part, axis=1, keepdims=True)
        l_part, acc = None, None
        for n in range(i + 1):
            p = jnp.exp2((scores(n) - m) * EXP2_SCALE)
            pv = _dot(p.astype(BF16), v_blk(n))
            l_part = fold(p, jnp.add) if n == 0 else l_part + fold(p, jnp.add)
            acc = pv if n == 0 else acc + pv
        l = jnp.sum(l_part, axis=1, keepdims=True)
        o_ref[hq * half:(hq + 1) * half, :] = (acc / l).astype(o_ref.dtype)


def _moba_tiles_kernel(q_ref, k_ref, v_ref, o_ref, kmean_ref, kbf_ref, vbf_ref, *, nb):
    blk = MOBA_BLOCK
    i = pl.program_id(2)

    @pl.when(i == 0)
    def _():
        kmean_ref[...] = jnp.zeros_like(kmean_ref)
        for n in range(nb):
            kmean_ref[n:n + 1, :] = jnp.mean(k_ref[n * blk:(n + 1) * blk, :], axis=0, keepdims=True)
        kbf_ref[...] = k_ref[...].astype(BF16)
        vbf_ref[...] = v_ref[...].astype(BF16)

    for c in range(nb):
        @pl.when(i == c)
        def _(c=c):
            _moba_tile(c, nb, q_ref, o_ref, kmean_ref, kbf_ref, vbf_ref)


def moba_prompt(qn, kn, proj, v_col0):
    b, t, w = qn.shape
    nh = w // HEAD_DIM
    nb = t // MOBA_BLOCK
    assert t % MOBA_BLOCK == 0 and nb <= LANE
    return pl.pallas_call(
        functools.partial(_moba_tiles_kernel, nb=nb),
        grid=(b, nh, nb),
        in_specs=[pl.BlockSpec((None, MOBA_BLOCK, HEAD_DIM), lambda bi, h, i: (bi, i, h)),
                  pl.BlockSpec((None, t, HEAD_DIM), lambda bi, h, i: (bi, 0, h)),
                  pl.BlockSpec((None, t, HEAD_DIM), lambda bi, h, i: (bi, 0, v_col0 // HEAD_DIM + h))],
        out_specs=pl.BlockSpec((None, MOBA_BLOCK, HEAD_DIM), lambda bi, h, i: (bi, i, h)),
        out_shape=jax.ShapeDtypeStruct((b, t, w), BF16),
        scratch_shapes=[pltpu.VMEM((LANE, HEAD_DIM), F32), pltpu.VMEM((t, HEAD_DIM), BF16), pltpu.VMEM((t, HEAD_DIM), BF16)],
        compiler_params=_params("parallel", "parallel", "arbitrary"),
        name="moba_prompt",
    )(qn, kn, proj)


def _moba_dec_block_kernel(pt_ref, q_ref, ka_ref, kb_ref, va_ref, vb_ref,
                           acc_ref, g_ref, m_ref, l_ref, *, nh, tq):
    n = pl.program_id(1)
    q = q_ref[...]
    kb = jnp.concatenate([ka_ref[...], kb_ref[...]], axis=0).astype(BF16)
    vb = jnp.concatenate([va_ref[...], vb_ref[...]], axis=0).astype(BF16)
    s_raw = _dot_nt(q, kb)
    same_head = _iota(s_raw.shape, 1) % nh == _iota(s_raw.shape, 0) // tq
    gate = jnp.sum(jnp.where(same_head, s_raw, 0.0), axis=1, keepdims=True) * (1.0 / MOBA_BLOCK)
    s = jnp.where(same_head, s_raw * SCALE, NEG)
    m = jnp.max(s, axis=1, keepdims=True)
    p = jnp.exp(s - m)
    l = jnp.sum(p, axis=1, keepdims=True)
    acc_ref[...] = _dot(p.astype(BF16), vb)

    @pl.when(n == 0)
    def _():
        g_ref[...] = jnp.zeros_like(g_ref)
        m_ref[...] = jnp.zeros_like(m_ref)
        l_ref[...] = jnp.zeros_like(l_ref)

    lane = _iota(g_ref.shape, 1)
    g_ref[...] = jnp.where(lane == n, gate, g_ref[...])
    m_ref[...] = jnp.where(lane == n, m, m_ref[...])
    l_ref[...] = jnp.where(lane == n, l, l_ref[...])


def _moba_dec_combine_kernel(g_ref, m_ref, l_ref, acc_ref, q_ref, kn_ref, vn_ref, o_ref, *, nbk, nh, tq):
    gate = g_ref[...]
    lane = _iota(gate.shape, 1)
    row = _iota(gate.shape, 0)
    valid = lane < nbk
    gate = jnp.where(valid, gate, -jnp.inf)
    sel = valid & (_rank_before(gate, nbk) < min(MOBA_TOPK, nbk + 1))
    q = q_ref[...]
    s_own = _dot_nt(q, kn_ref[...].astype(BF16)) * SCALE
    own = (lane % nh == row // tq) & (lane // nh <= row % tq)
    m_blk = m_ref[...]
    m_tot = jnp.maximum(jnp.max(jnp.where(sel, m_blk, NEG), axis=1, keepdims=True),
                        jnp.max(jnp.where(own, s_own, NEG), axis=1, keepdims=True))
    wgt = jnp.where(sel, jnp.exp(m_blk - m_tot), 0.0)
    p_own = jnp.where(own, jnp.exp(s_own - m_tot), 0.0)
    l_tot = jnp.sum(wgt * l_ref[...], axis=1, keepdims=True) + jnp.sum(p_own, axis=1, keepdims=True)
    acc = _dot(p_own.astype(BF16), vn_ref[...].astype(BF16))
    for n in range(nbk):
        acc = acc + wgt[:, n:n + 1] * acc_ref[n]
    o_ref[...] = acc / jnp.where(l_tot > 0.0, l_tot, 1.0)


def moba_sample(qn, kn_new, v_new, cache_k, cache_v, layer, page_table):
    b, tq, w = qn.shape
    nh = w // HEAD_DIM
    rows = nh * tq
    npages = page_table.shape[1]
    ppb = MOBA_BLOCK // PAGE
    nbk = npages // ppb
    assert rows == LANE and ppb == 2 and nbk <= LANE
    qrows = qn.reshape(b, tq, nh, HEAD_DIM).transpose(0, 2, 1, 3).reshape(b, rows, HEAD_DIM)
    kn_rows = kn_new.reshape(b, tq * nh, HEAD_DIM)
    vn_rows = v_new.reshape(b, tq * nh, HEAD_DIM)

    page = lambda j: pl.BlockSpec((None, None, PAGE * nh, HEAD_DIM), lambda bi, n, pt: (layer, pt[bi, ppb * n + j], 0, 0))
    stat = pl.BlockSpec((None, rows, LANE), lambda bi, n, pt: (bi, 0, 0))
    acc, g, m, l = pl.pallas_call(
        functools.partial(_moba_dec_block_kernel, nh=nh, tq=tq),
        grid_spec=pltpu.PrefetchScalarGridSpec(
            num_scalar_prefetch=1,
            grid=(b, nbk),
            in_specs=[pl.BlockSpec((None, rows, HEAD_DIM), lambda bi, n, pt: (bi, 0, 0)),
                      page(0), page(1), page(0), page(1)],
            out_specs=[pl.BlockSpec((None, None, rows, HEAD_DIM), lambda bi, n, pt: (bi, n, 0, 0)), stat, stat, stat],
        ),
        out_shape=[jax.ShapeDtypeStruct((b, nbk, rows, HEAD_DIM), F32)] + [jax.ShapeDtypeStruct((b, rows, LANE), F32)] * 3,
        compiler_params=_params("parallel", "arbitrary"),
        name="moba_sample_blocks",
    )(page_table, qrows, cache_k, cache_k, cache_v, cache_v)

    per_b = lambda *shape: pl.BlockSpec((None,) + shape, lambda bi: (bi,) + (0,) * len(shape))
    o = pl.pallas_call(
        functools.partial(_moba_dec_combine_kernel, nbk=nbk, nh=nh, tq=tq),
        grid=(b,),
        in_specs=[per_b(rows, LANE), per_b(rows, LANE), per_b(rows, LANE), per_b(nbk, rows, HEAD_DIM),
                  per_b(rows, HEAD_DIM), per_b(tq * nh, HEAD_DIM), per_b(tq * nh, HEAD_DIM)],
        out_specs=per_b(rows, HEAD_DIM),
        out_shape=jax.ShapeDtypeStruct((b, rows, HEAD_DIM), F32),
        compiler_params=_params("parallel"),
        name="moba_sample_combine",
    )(g, m, l, acc, qrows, kn_rows, vn_rows)
    return o.reshape(b, nh, tq, HEAD_DIM).transpose(0, 2, 1, 3).reshape(b, tq, w)


def _ssd_kernel(x_ref, bm_ref, cm_ref, z_ref, dtc_ref, dtr_ref, bias_c_ref, bias_r_ref, alog_c_ref, alog_r_ref,
                dskip_ref, gout_ref, s0_ref, pre_ref, y_ref, sfin_ref, state_ref, *, q, t_valid):
    c = pl.program_id(1)
    nc = pl.num_programs(1)
    pre_w = pre_ref.shape[1]
    y_ref[:, :pre_w] = pre_ref[...]

    @pl.when(c == 0)
    def _():
        state_ref[...] = s0_ref[...]

    dtc = _softplus(dtc_ref[...] + bias_c_ref[...])
    dtr = _softplus(dtr_ref[...] + bias_r_ref[...])
    if t_valid is not None:
        dtc = jnp.where(c * q + _iota(dtc.shape, 0) < t_valid, dtc, 0.0)
        dtr = jnp.where(c * q + _iota(dtr.shape, 1) < t_valid, dtr, 0.0)
    da_c = dtc * -jnp.exp(alog_c_ref[...])
    da_r = dtr * -jnp.exp(alog_r_ref[...])
    tri = _iota((q, q), 0) >= _iota((q, q), 1)
    tri_bf = jnp.where(tri, 1.0, 0.0).astype(BF16)
    cum_c = sum(_dot(tri_bf, part) for part in _split3(da_c))
    cum_r = sum(_dot_nt(part, tri_bf) for part in _split3(da_r))
    lane = _iota((q, LANE), 1)
    lo_half = lane < SSD_HEAD_DIM
    row_lo = _iota((LANE, 1), 0) < SSD_HEAD_DIM
    pairs = HEADS_PER_GROUP // 2

    for g in range(SSD_GROUPS):
        xg = x_ref[:, g * GROUP_W:(g + 1) * GROUP_W]
        bm = bm_ref[:, g * SSD_STATE:(g + 1) * SSD_STATE].astype(BF16)
        cm = cm_ref[:, g * SSD_STATE:(g + 1) * SSD_STATE].astype(BF16)
        cb = _dot_nt(cm, bm)
        y_parts = []
        for pr in range(pairs):
            x_pair = xg[:, pr * LANE:(pr + 1) * LANE]
            x_bf = x_pair.astype(BF16)
            ys, cols, lasts = [], [], []
            for h in (g * HEADS_PER_GROUP + 2 * pr, g * HEADS_PER_GROUP + 2 * pr + 1):
                col = cum_c[:, h:h + 1]
                seg = col - cum_r[h:h + 1, :]
                decay = jnp.where(tri, jnp.exp(jnp.where(tri, seg, 0.0)), 0.0)
                wgt = cb * decay * dtr[h:h + 1, :]
                ys.append(_dot(wgt.astype(BF16), x_bf))
                cols.append(col)
                lasts.append(cum_c[q - 1:q, h:h + 1])
            sidx = g * pairs + pr
            state = state_ref[sidx]
            y_pair = jnp.where(lo_half, ys[0], ys[1])
            carry_in = _dot_nt(cm, state.astype(BF16))
            y_pair = y_pair + carry_in * jnp.exp(jnp.where(lo_half, cols[0], cols[1]))
            y_parts.append(y_pair)
            tail = jnp.where(lo_half,
                             jnp.exp(lasts[0] - cols[0]) * dtc[:, 2 * sidx:2 * sidx + 1],
                             jnp.exp(lasts[1] - cols[1]) * dtc[:, 2 * sidx + 1:2 * sidx + 2])
            upd = _dot_tn((x_pair * tail).astype(BF16), bm)
            state_ref[sidx] = state * jnp.where(row_lo, jnp.exp(lasts[0]), jnp.exp(lasts[1])) + upd
        sl = slice(g * GROUP_W, (g + 1) * GROUP_W)
        yg = jnp.concatenate(y_parts, axis=1) + xg * dskip_ref[:, sl]
        yg = yg * _silu(z_ref[:, sl])
        y_ref[:, pre_w + g * GROUP_W:pre_w + (g + 1) * GROUP_W] = _rms(yg, gout_ref[:, sl]).astype(y_ref.dtype)

    @pl.when(c == nc - 1)
    def _():
        sfin_ref[...] = state_ref[...]


def ssd_mixer(conv, z_src, z_col0, dt_raw, s0, dt_bias, a_log, d_skip, g_out, t_valid, prefix):
    b, t, _ = conv.shape
    pre_w = prefix.shape[-1]
    q = SSD_CHUNK
    assert t % q == 0
    nc = t // q
    dtr = dt_raw.transpose(0, 2, 1)
    pad_h = LANE - SSD_HEADS
    bias_c = jnp.pad(dt_bias, (0, pad_h)).reshape(1, LANE)
    bias_r = bias_c.reshape(LANE, 1)
    alog_c = jnp.pad(a_log, (0, pad_h)).reshape(1, LANE)
    alog_r = alog_c.reshape(LANE, 1)
    dskip = jnp.repeat(d_skip, SSD_HEAD_DIM).reshape(1, D_INNER)
    npair = SSD_HEADS // 2
    s0p = s0.reshape(b, npair, 2 * SSD_HEAD_DIM, SSD_STATE)
    nbc = D_INNER // (SSD_GROUPS * SSD_STATE)
    const = lambda shape: pl.BlockSpec(shape, lambda bi, c: (0,) * len(shape))
    y, sfin = pl.pallas_call(
        functools.partial(_ssd_kernel, q=q, t_valid=t_valid),
        grid=(b, nc),
        in_specs=[pl.BlockSpec((None, q, D_INNER), lambda bi, c: (bi, c, 0)),
                  pl.BlockSpec((None, q, SSD_GROUPS * SSD_STATE), lambda bi, c: (bi, c, nbc)),
                  pl.BlockSpec((None, q, SSD_GROUPS * SSD_STATE), lambda bi, c: (bi, c, nbc + 1)),
                  pl.BlockSpec((None, q, D_INNER), lambda bi, c: (bi, c, z_col0 // D_INNER)),
                  pl.BlockSpec((None, q, LANE), lambda bi, c: (bi, c, 0)),
                  pl.BlockSpec((None, LANE, q), lambda bi, c: (bi, 0, c)),
                  const((1, LANE)), const((LANE, 1)), const((1, LANE)), const((LANE, 1)),
                  const((1, D_INNER)), const((1, D_INNER)),
                  pl.BlockSpec((None, npair, LANE, SSD_STATE), lambda bi, c: (bi, 0, 0, 0)),
                  pl.BlockSpec((None, q, pre_w), lambda bi, c: (bi, c, 0))],
        out_specs=[pl.BlockSpec((None, q, pre_w + D_INNER), lambda bi, c: (bi, c, 0)),
                   pl.BlockSpec((None, npair, LANE, SSD_STATE), lambda bi, c: (bi, 0, 0, 0))],
        out_shape=[jax.ShapeDtypeStruct((b, t, pre_w + D_INNER), BF16),
                   jax.ShapeDtypeStruct((b, npair, LANE, SSD_STATE), F32)],
        scratch_shapes=[pltpu.VMEM((npair, LANE, SSD_STATE), F32)],
        compiler_params=_params("parallel", "arbitrary"),
        name="ssd_scan",
    )(conv, conv, conv, z_src, dt_raw, dtr, bias_c, bias_r, alog_c, alog_r, dskip, g_out.reshape(1, D_INNER), s0p,
      prefix)
    return y, sfin.reshape(b, SSD_HEADS, SSD_HEAD_DIM, SSD_STATE)


def _compress_kernel(pt_ref, a_ref, b_ref, cpe_a_ref, cpe_b_ref, b1_ref, w2_ref, b2_ref, g_ref, o_ref,
                     ha_ref, hb_ref, *, npg, norm):
    bi = pl.program_id(0)
    ncp = npg * 8

    def gather(p, _):
        src = pl.ds(pl.multiple_of(pt_ref[bi, p] * 8, 8), 8)
        dst = pl.ds(pl.multiple_of(p * 8, 8), 8)
        ha_ref[dst, :] = a_ref[src, :]
        hb_ref[dst, :] = b_ref[src, :]
        return 0

    lax.fori_loop(0, npg, gather, 0)
    hb_ref[ncp:ncp + 8, :] = jnp.zeros((8, HEAD_DIM), F32)
    hid = ha_ref[...] + hb_ref[pl.ds(1, ncp), :] + (cpe_a_ref[0:1, :] + cpe_b_ref[1:2, :] + b1_ref[...])
    out = _dot(_gelu_tanh(hid).astype(BF16), w2_ref[...]) + b2_ref[...]
    if norm:
        out = _rms(out, g_ref[...])
    o_ref[...] = out


ROWS_PER_TILE = 8


def _chunk_products_kernel(x_ref, w_ref, o_ref):
    @pl.when(pl.program_id(1) == 0)
    def _():
        o_ref[...] = jnp.zeros_like(o_ref)

    for r in range(ROWS_PER_TILE):
        g = r % NSA_KV
        prod = _dot(x_ref[:, r, :].astype(BF16), w_ref[r // NSA_KV])
        o_ref[:, g * HEAD_DIM:(g + 1) * HEAD_DIM] += prod[:, :HEAD_DIM]
        o_ref[:, KV_W + g * HEAD_DIM:KV_W + (g + 1) * HEAD_DIM] += prod[:, HEAD_DIM:]


def chunk_products(x, layer, w_pair):
    nck = x.shape[1]
    blk = min(nck, 1024)
    assert nck % blk == 0 and x.shape[2] == CMP_STRIDE * NSA_KV
    per_step = ROWS_PER_TILE // NSA_KV
    return pl.pallas_call(
        _chunk_products_kernel,
        grid=(nck // blk, CMP_STRIDE // per_step),
        in_specs=[pl.BlockSpec((None, blk, ROWS_PER_TILE, HEAD_DIM), lambda i, lp: (layer, i, lp, 0)),
                  pl.BlockSpec((per_step, HEAD_DIM, 2 * HEAD_DIM), lambda i, lp: (lp, 0, 0))],
        out_specs=pl.BlockSpec((blk, 2 * KV_W), lambda i, lp: (i, 0)),
        out_shape=jax.ShapeDtypeStruct((nck, 2 * KV_W), F32),
        compiler_params=_params("parallel", "arbitrary"),
        name="chunk_products",
    )(x, w_pair)


def compress_tokens(rows, layer, page_table, pe, w1, b1, w2, b2, g_norm):
    nck = rows.shape[1]
    b, npg = page_table.shape
    ncp = npg * 8
    w_pair = w1.reshape(2, CMP_STRIDE, HEAD_DIM, HEAD_DIM).transpose(1, 2, 0, 3).reshape(CMP_STRIDE, HEAD_DIM, 2 * HEAD_DIM)
    w_pair = w_pair.astype(BF16)
    ab = chunk_products(rows, layer, w_pair)
    pe_rows = jnp.broadcast_to(pe.reshape(2, CMP_STRIDE, 1, HEAD_DIM), (2, CMP_STRIDE, NSA_KV, HEAD_DIM))
    pe_rows = jnp.pad(pe_rows.reshape(1, 2, CMP_STRIDE * NSA_KV, HEAD_DIM), ((0, 0), (0, 6), (0, 0), (0, 0)))
    cpe = chunk_products(pe_rows, 0, w_pair)
    col = lambda off: (lambda bi, g, pt: (0, off + g))
    return pl.pallas_call(
        functools.partial(_compress_kernel, npg=npg, norm=g_norm is not None),
        grid_spec=pltpu.PrefetchScalarGridSpec(
            num_scalar_prefetch=1,
            grid=(b, NSA_KV),
            in_specs=[pl.BlockSpec((nck, HEAD_DIM), col(0)), pl.BlockSpec((nck, HEAD_DIM), col(NSA_KV)),
                      pl.BlockSpec((8, HEAD_DIM), col(0)), pl.BlockSpec((8, HEAD_DIM), col(NSA_KV)),
                      pl.BlockSpec((1, HEAD_DIM), lambda bi, g, pt: (0, 0)),
                      pl.BlockSpec((HEAD_DIM, HEAD_DIM), lambda bi, g, pt: (0, 0)),
                      pl.BlockSpec((1, HEAD_DIM), lambda bi, g, pt: (0, 0)),
                      pl.BlockSpec((1, HEAD_DIM), lambda bi, g, pt: (0, 0))],
            out_specs=pl.BlockSpec((None, ncp, HEAD_DIM), lambda bi, g, pt: (bi, 0, g)),
            scratch_shapes=[pltpu.VMEM((ncp, HEAD_DIM), F32), pltpu.VMEM((ncp + 8, HEAD_DIM), F32)],
        ),
        out_shape=jax.ShapeDtypeStruct((b, ncp, KV_W), F32),
        compiler_params=_params("parallel", "parallel"),
        name="nsa_compress",
    )(page_table, ab, ab, cpe, cpe, b1.reshape(1, -1), w2.astype(BF16), b2.reshape(1, -1),
      (g_norm if g_norm is not None else jnp.ones((HEAD_DIM,), F32)).reshape(1, -1))


def _nsa_cmp_kernel(q_ref, kc_ref, vc_ref, o_ref, sel_ref, *, tq, q_start, n_sel, nsp):
    i = pl.program_id(2)
    kc = kc_ref[...].astype(BF16)
    vc = vc_ref[...].astype(BF16)
    ncp = kc.shape[0]
    qpos = q_start + i * tq + _iota((tq, 1), 0)
    end = _iota((tq, ncp), 1) * CMP_STRIDE + (CMP_LEN - 1)
    msk = end <= qpos
    imp = jnp.zeros((tq, ncp), F32)
    for r in range(NSA_HPG):
        sl = slice(r * HEAD_DIM, (r + 1) * HEAD_DIM)
        s = jnp.where(msk, _dot_nt(q_ref[:, sl], kc) * SCALE, NEG)
        p = jnp.where(msk, jnp.exp(s - jnp.max(s, axis=1, keepdims=True)), 0.0)
        d = jnp.sum(p, axis=1, keepdims=True)
        p = p / jnp.where(d > 0.0, d, 1.0)
        o_ref[:, sl] = _dot(p.astype(BF16), vc)
        imp = imp + p
    cn = _iota((ncp, nsp), 0)
    sj = _iota((ncp, nsp), 1)
    overlap = jnp.where((cn >= 4 * sj - 1) & (cn <= 4 * sj + 3), 1.0, 0.0).astype(BF16)
    score = sum(_dot(part, overlap) for part in _split3(imp))
    j = _iota((tq, nsp), 1)
    qblk = qpos // SEL_BLOCK
    valid = j <= qblk
    forced = (j == 0) | (j == qblk) | (j == qblk - 1)
    score = jnp.where(valid, jnp.where(forced, jnp.inf, score), -jnp.inf)
    sel_ref[...] = jnp.where(valid & (_rank_before(score, n_sel) < min(SEL_TOPN, n_sel)), 1.0, 0.0)


def nsa_compressed(qn, kc, vc, q_start, n_sel):
    b, t, w = qn.shape
    ncp = kc.shape[1]
    gw = NSA_HPG * HEAD_DIM
    tq = min(t, 256)
    nsp = -(-n_sel // LANE) * LANE
    return pl.pallas_call(
        functools.partial(_nsa_cmp_kernel, tq=tq, q_start=q_start, n_sel=n_sel, nsp=nsp),
        grid=(b, NSA_KV, t // tq),
        in_specs=[pl.BlockSpec((None, tq, gw), lambda bi, g, i: (bi, i, g)),
                  pl.BlockSpec((None, ncp, HEAD_DIM), lambda bi, g, i: (bi, 0, g)),
                  pl.BlockSpec((None, ncp, HEAD_DIM), lambda bi, g, i: (bi, 0, g))],
        out_specs=[pl.BlockSpec((None, tq, gw), lambda bi, g, i: (bi, i, g)),
                   pl.BlockSpec((None, None, tq, nsp), lambda bi, g, i: (bi, g, i, 0))],
        out_shape=[jax.ShapeDtypeStruct((b, t, w), F32), jax.ShapeDtypeStruct((b, NSA_KV, t, nsp), F32)],
        compiler_params=_params("parallel", "parallel", "parallel"),
        name="nsa_compressed",
    )(qn, kc, vc)


def _nsa_prompt_kernel(q_ref, sk_ref, sv_ref, wk_ref, wv_ref, sel_ref, ocmp_ref, gl_ref, o_ref,
                       m_ref, part_ref, acc_ref, s_ref, *, tq, tk):
    i = pl.program_id(2)
    q0 = i * tq
    tpos = q0 + _iota((tq, tk), 0)
    sel = sel_ref[...].astype(BF16)
    nsp = sel.shape[1]

    def sel_mask(k0):
        kpos = k0 + _iota((tq, tk), 1)
        expand = jnp.where(_iota((nsp, tk), 0) == (k0 + _iota((nsp, tk), 1)) // SEL_BLOCK, 1.0, 0.0).astype(BF16)
        return (_dot(sel, expand) > 0.5) & (kpos <= tpos)

    def win_mask(k0):
        kpos = k0 + _iota((tq, tk), 1)
        return (kpos <= tpos) & (kpos > tpos - WINDOW)

    n_hi = (q0 + tq - 1) // tk + 1
    branches = ((sk_ref, sv_ref, sel_mask, 0), (wk_ref, wv_ref, win_mask, jnp.maximum(q0 - (WINDOW - 1), 0) // tk))
    fold = lambda a, op: functools.reduce(op, [a[:, c * LANE:(c + 1) * LANE] for c in range(tk // LANE)])
    for br, (k_ref, v_ref, mask_fn, n_lo) in enumerate(branches):
        base = br * NSA_HPG
        for r in range(NSA_HPG):
            part_ref[base + r] = jnp.full((tq, LANE), NEG, F32)
            acc_ref[base + r] = jnp.zeros((tq, HEAD_DIM), F32)

        def max_sweep(n, _, base=base, k_ref=k_ref, mask_fn=mask_fn, n_lo=n_lo):
            k0 = pl.multiple_of(n * tk, tk)
            kb, msk = k_ref[pl.ds(k0, tk), :].astype(BF16), mask_fn(k0)
            for r in range(NSA_HPG):
                s = jnp.where(msk, _dot_nt(q_ref[:, r * HEAD_DIM:(r + 1) * HEAD_DIM], kb), NEG)
                s_ref[n - n_lo, r] = s
                part_ref[base + r] = jnp.maximum(part_ref[base + r], fold(s, jnp.maximum))
            return 0

        lax.fori_loop(n_lo, n_hi, max_sweep, 0)
        for r in range(NSA_HPG):
            m_ref[base + r] = jnp.max(part_ref[base + r], axis=1, keepdims=True)
            part_ref[base + r] = jnp.zeros((tq, LANE), F32)

        def exp_sweep(n, _, base=base, v_ref=v_ref, n_lo=n_lo):
            vb = v_ref[pl.ds(pl.multiple_of(n * tk, tk), tk), :].astype(BF16)
            for r in range(NSA_HPG):
                p = jnp.exp2((s_ref[n - n_lo, r] - m_ref[base + r]) * EXP2_SCALE)
                part_ref[base + r] += fold(p, jnp.add)
                acc_ref[base + r] += _dot(p.astype(BF16), vb)
            return 0

        lax.fori_loop(n_lo, n_hi, exp_sweep, 0)

    gates = [_sigmoid(gl_ref[c]) for c in range(3)]
    for r in range(NSA_HPG):
        sl = slice(r * HEAD_DIM, (r + 1) * HEAD_DIM)
        o_sel = acc_ref[r] / jnp.sum(part_ref[r], axis=1, keepdims=True)
        o_win = acc_ref[NSA_HPG + r] / jnp.sum(part_ref[NSA_HPG + r], axis=1, keepdims=True)
        o = gates[0][:, r:r + 1] * ocmp_ref[:, sl] + gates[1][:, r:r + 1] * o_sel + gates[2][:, r:r + 1] * o_win
        o_ref[:, sl] = o.astype(o_ref.dtype)


def nsa_prompt(qn, skn, wkn, proj, sv_col0, wv_col0, selm, ocmp, gate_logits, tq=256, tk=256):
    b, t, w = qn.shape
    gw = NSA_HPG * HEAD_DIM
    tq, tk = min(t, tq), min(t, tk)
    nsp = selm.shape[-1]
    kv = lambda off: pl.BlockSpec((None, t, HEAD_DIM), lambda bi, g, i: (bi, 0, off + g))
    return pl.pallas_call(
        functools.partial(_nsa_prompt_kernel, tq=tq, tk=tk),
        grid=(b, NSA_KV, t // tq),
        in_specs=[pl.BlockSpec((None, tq, gw), lambda bi, g, i: (bi, i, g)),
                  kv(0), kv(sv_col0 // HEAD_DIM), kv(0), kv(wv_col0 // HEAD_DIM),
                  pl.BlockSpec((None, None, tq, nsp), lambda bi, g, i: (bi, g, i, 0)),
                  pl.BlockSpec((None, tq, gw), lambda bi, g, i: (bi, i, g)),
                  pl.BlockSpec((3, None, None, tq, NSA_HPG), lambda bi, g, i: (0, bi, g, i, 0))],
        out_specs=pl.BlockSpec((None, tq, gw), lambda bi, g, i: (bi, i, g)),
        out_shape=jax.ShapeDtypeStruct((b, t, w), BF16),
        scratch_shapes=[pltpu.VMEM((2 * NSA_HPG, tq, 1), F32), pltpu.VMEM((2 * NSA_HPG, tq, LANE), F32),
                        pltpu.VMEM((2 * NSA_HPG, tq, HEAD_DIM), F32), pltpu.VMEM((t // tk, NSA_HPG, tq, tk), F32)],
        compiler_params=_params("parallel", "parallel", "arbitrary"),
        name="nsa_prompt",
    )(qn, skn, proj, wkn, proj, selm, ocmp, gate_logits)


def _nsa_dec_kernel(pt_ref, q_ref, sel_ref, ocmp_ref, gl_ref, kp_ref, vp_ref, skn_ref, svn_ref,
                    wkb_ref, wvb_ref, wkn_ref, wvn_ref, o_ref, m_ref, l_ref, acc_ref, *, tq, past_len):
    p = pl.program_id(1)
    npages = pl.num_programs(1)
    rows = NSA_KV * NSA_HPG * tq
    per_g = NSA_HPG * tq
    q = q_ref[...]
    sel = sel_ref[...]

    @pl.when(p == 0)
    def _():
        m_ref[...] = jnp.full(m_ref.shape, NEG, F32)
        l_ref[...] = jnp.zeros_like(l_ref)
        acc_ref[...] = jnp.zeros_like(acc_ref)

    def grid_of(n_keys):
        shape = (rows, n_keys * NSA_KV)
        lane, row = _iota(shape, 1), _iota(shape, 0)
        return lane // NSA_KV, lane % NSA_KV == row // per_g, row % tq

    def update(carry, k_rows, v_rows, msk):
        s = _dot_nt(q, k_rows.astype(BF16)) * SCALE
        return _online_update(carry, s, msk, v_rows.astype(BF16))

    key, same_g, _ = grid_of(PAGE)
    blocks_per_page = PAGE // SEL_BLOCK
    picked = jnp.where(key < SEL_BLOCK, _pick_col(sel, blocks_per_page * p), _pick_col(sel, blocks_per_page * p + 1))
    carry = update((m_ref[...], l_ref[...], acc_ref[...]), kp_ref[...], vp_ref[...], same_g & (picked > 0.5))
    m_ref[...], l_ref[...], acc_ref[...] = carry

    @pl.when(p == npages - 1)
    def _():
        n_new = skn_ref.shape[0] // NSA_KV
        key, same_g, t_row = grid_of(n_new)
        own = same_g & (key <= t_row) & (key < tq)
        new_blk = _pick_col(sel, past_len // SEL_BLOCK) > 0.5
        o_sel = _flash_out(update(carry, skn_ref[...], svn_ref[...], own & new_blk))
        wkey, wsame, wt = grid_of(WINDOW)
        w_carry = update(_flash_init(rows, HEAD_DIM), wkb_ref[...], wvb_ref[...], wsame & (wkey > wt))
        o_win = _flash_out(update(w_carry, wkn_ref[...], wvn_ref[...], own))
        gates = _sigmoid(gl_ref[...])
        o_ref[...] = gates[:, 0:1] * ocmp_ref[...] + gates[:, 1:2] * o_sel + gates[:, 2:3] * o_win


def nsa_sample(qn, skn_new, sv_new, wkn_new, wv_new, cache_sk, cache_sv, win_k, win_v, layer,
               page_table, selm, ocmp, gate_logits):
    b, tq, w = qn.shape
    rows = NSA_KV * NSA_HPG * tq
    npages = page_table.shape[1]
    past_len = npages * PAGE
    nsp = selm.shape[-1]
    assert win_k.shape[2] == WINDOW * NSA_KV and past_len % SEL_BLOCK == 0 and tq * NSA_KV <= LANE
    selrows = jnp.broadcast_to(selm[:, :, None], (b, NSA_KV, NSA_HPG, tq, nsp)).reshape(b, rows, nsp)
    to_rows = lambda a: a.reshape(b, tq, NSA_KV, NSA_HPG, -1).transpose(0, 2, 3, 1, 4).reshape(b, rows, -1)
    new_rows = lambda a: jnp.pad(a.reshape(b, tq * NSA_KV, HEAD_DIM), ((0, 0), (0, LANE - tq * NSA_KV), (0, 0)))
    per_b = lambda *shape: pl.BlockSpec((None,) + shape, lambda bi, p, pt: (bi,) + (0,) * len(shape))
    page = pl.BlockSpec((None, None, PAGE * NSA_KV, HEAD_DIM), lambda bi, p, pt: (layer, pt[bi, p], 0, 0))
    wbuf = pl.BlockSpec((None, None, WINDOW * NSA_KV, HEAD_DIM), lambda bi, p, pt: (layer, bi, 0, 0))
    o = pl.pallas_call(
        functools.partial(_nsa_dec_kernel, tq=tq, past_len=past_len),
        grid_spec=pltpu.PrefetchScalarGridSpec(
            num_scalar_prefetch=1,
            grid=(b, npages),
            in_specs=[per_b(rows, HEAD_DIM), per_b(rows, nsp), per_b(rows, HEAD_DIM), per_b(rows, 3),
                      page, page, per_b(LANE, HEAD_DIM), per_b(LANE, HEAD_DIM), wbuf, wbuf,
                      per_b(LANE, HEAD_DIM), per_b(LANE, HEAD_DIM)],
            out_specs=per_b(rows, HEAD_DIM),
            scratch_shapes=[pltpu.VMEM((rows, 1), F32), pltpu.VMEM((rows, 1), F32), pltpu.VMEM((rows, HEAD_DIM), F32)],
        ),
        out_shape=jax.ShapeDtypeStruct((b, rows, HEAD_DIM), F32),
        compiler_params=_params("parallel", "arbitrary"),
        name="nsa_sample",
    )(page_table, to_rows(qn), selrows, to_rows(ocmp), to_rows(gate_logits), cache_sk, cache_sv,
      new_rows(skn_new), new_rows(sv_new), win_k, win_v, new_rows(wkn_new), new_rows(wv_new))
    return o.reshape(b, NSA_KV, NSA_HPG, tq, HEAD_DIM).transpose(0, 3, 1, 2, 4).reshape(b, tq, w)


MOBA_W = MOBA_HEADS * HEAD_DIM
CONV_DIM = D_INNER + 2 * SSD_GROUPS * SSD_STATE
EVEN_MAIN = 3 * MOBA_W + D_INNER + CONV_DIM
Z_COL0 = 0
XBC_COL0 = D_INNER
Q_COL0 = XBC_COL0 + CONV_DIM
K_COL0 = Q_COL0 + MOBA_W
V_COL0 = K_COL0 + MOBA_W
NSA_Q = NSA_KV * NSA_HPG * HEAD_DIM
ODD_MAIN = NSA_Q + 6 * KV_W
D_FF = 8192


PROJ_TN = 512


def _ws(a, w, layer, res=None, **kw):
    return a, w, layer, res, kw


def _run_pair(big, small):
    req_big, req_small = next(big), next(small)
    while True:
        a, w, layer, res, kw = req_big
        out_big, out_small = matmul_ws(a, w, layer, res=res, small=req_small[0], small_res=req_small[3], **kw)
        try:
            req_big = big.send(out_big)
        except StopIteration as done_big:
            try:
                small.send(out_small)
            except StopIteration as done_small:
                return done_big.value, done_small.value
            raise AssertionError("the two groups must issue the same matmul sequence")
        req_small = small.send(out_small)


def _proj(h2d, w, layer, main, out_cols=None):
    wt = jnp.swapaxes(w, 1, 2)
    tail = jnp.pad(wt[layer, main:, :], ((0, LANE - (w.shape[2] - main)), (0, 0)))
    main_out = yield _ws(h2d, wt, layer, k=h2d.shape[1], n=main, out_cols=out_cols, w_t=True, tn=PROJ_TN)
    return main_out, matmul(h2d, tail, b_t=True)


def _even_layer(x, sample, q_start, cache_k, cache_v, e, page_table, conv_prev, ssm_prev,
                g_norm, w_in, w_out, qk_g, conv_w, conv_b, dt_bias, a_log, d_skip, g_out):
    b, t, d = x.shape
    x2 = x.reshape(b * t, d)
    h = rmsnorm_rows(x2, g_norm)
    n_tiles = EVEN_MAIN // PROJ_TN
    proj, dt_raw = yield from _proj(h, w_in, e, EVEN_MAIN, out_cols=lambda j: (j + Q_COL0 // PROJ_TN) % n_tiles)
    proj = proj.reshape(b, t, EVEN_MAIN)
    dt_raw = dt_raw.reshape(b, t, LANE)
    qn = head_norm(proj, Q_COL0, MOBA_HEADS, qk_g[0], BF16)
    kn = head_norm(proj, K_COL0, MOBA_HEADS, qk_g[1], F32)
    v = proj[..., V_COL0:]
    if sample:
        o_attn = moba_sample(qn, kn, v, cache_k, cache_v, e, page_table).astype(BF16)
    else:
        o_attn = moba_prompt(qn, kn, proj, V_COL0)
    conv = dwconv(proj, XBC_COL0, CONV_DIM, _history8(conv_prev, b, CONV_DIM), conv_w, conv_b, False, F32)
    xbc = proj[..., XBC_COL0:Q_COL0]
    conv_state = jnp.concatenate([_history8(conv_prev, b, CONV_DIM), xbc], axis=1)[:, -(SSD_CONV - 1):]
    if t % SSD_CHUNK:
        tp = -(-t // SSD_CHUNK) * SSD_CHUNK
        padt = lambda a: jnp.pad(a, ((0, 0), (0, tp - t), (0, 0)))
        mixed, ssm = ssd_mixer(padt(conv), padt(proj[..., Z_COL0:XBC_COL0]), 0, padt(dt_raw), ssm_prev,
                               dt_bias, a_log, d_skip, g_out, t, padt(o_attn))
        mixed = mixed[:, :t]
    else:
        mixed, ssm = ssd_mixer(conv, proj, Z_COL0, dt_raw, ssm_prev, dt_bias, a_log, d_skip, g_out, None, o_attn)
    mixed = mixed.reshape(b * t, MOBA_W + D_INNER)
    x2 = yield _ws(mixed, w_out, e, res=x2, k=MOBA_W + D_INNER, n=d, tk=MOBA_W)
    heads = lambda a: a.reshape(b, t, MOBA_HEADS, HEAD_DIM)
    return x2.reshape(b, t, d), heads(kn), heads(v), conv_state, ssm


def _odd_layer(x, sample, q_start, caches, e, page_table, win_k, win_v,
               g_norm, w_in, w_out, qk_g, pe, w1, b1, w2, b2):
    b, t, d = x.shape
    x2 = x.reshape(b * t, d)
    h = rmsnorm_rows(x2, g_norm)
    proj, gl = yield from _proj(h, w_in, e, ODD_MAIN)
    proj = proj.reshape(b, t, ODD_MAIN)
    gl = gl[:, :3 * NSA_KV * NSA_HPG].reshape(b, t, NSA_KV, NSA_HPG, 3)
    col = lambda i: NSA_Q + i * KV_W
    qn = head_norm(proj, 0, NSA_KV * NSA_HPG, qk_g[0], BF16)
    skn = head_norm(proj, col(2), NSA_KV, qk_g[2], F32)
    wkn = head_norm(proj, col(4), NSA_KV, qk_g[3], F32)
    ck, cv, sv, wv = (proj[..., col(i):col(i + 1)] for i in (0, 1, 3, 5))
    if sample:
        cache_ck, cache_cv, cache_sk, cache_sv = caches
        assert t < CMP_STRIDE
        chunks = lambda c: c.reshape(c.shape[0], -1, CMP_STRIDE * NSA_KV, HEAD_DIM)
        rows_k, rows_v, pt, src = chunks(cache_ck), chunks(cache_cv), page_table, e
        n_rows = page_table.shape[1] * PAGE + t
    else:
        chunks = lambda a: a.reshape(1, b * t // CMP_STRIDE, CMP_STRIDE * NSA_KV, HEAD_DIM)
        rows_k, rows_v, src = chunks(ck), chunks(cv), 0
        npg = t // PAGE
        pt = (jnp.arange(b, dtype=jnp.int32)[:, None] * npg + jnp.arange(npg, dtype=jnp.int32)[None, :])
        n_rows = t
    kc = compress_tokens(rows_k, src, pt, pe[0], w1[0], b1[0], w2[0], b2[0], qk_g[1])
    vc = compress_tokens(rows_v, src, pt, pe[1], w1[1], b1[1], w2[1], b2[1], None)
    n_sel = -(-n_rows // SEL_BLOCK)
    ocmp, selm = nsa_compressed(qn, kc, vc, q_start, n_sel)
    if sample:
        rows_view = lambda c: c.reshape(c.shape[0], c.shape[1], c.shape[2] * NSA_KV, HEAD_DIM)
        o = nsa_sample(qn, skn, sv, wkn, wv, rows_view(cache_sk), rows_view(cache_sv), rows_view(win_k),
                       rows_view(win_v), e, page_table, selm, ocmp, gl).astype(BF16)
        wk_all = jnp.concatenate([win_k[e].reshape(b, -1, KV_W), wkn], axis=1)
        wv_all = jnp.concatenate([win_v[e].reshape(b, -1, KV_W), wv], axis=1)
    else:
        o = nsa_prompt(qn, skn, wkn, proj, col(3), col(5), selm, ocmp, gl.transpose(4, 0, 2, 1, 3))
        wk_all, wv_all = wkn, wv
    x2 = yield _ws(o.reshape(b * t, NSA_Q), w_out, e, res=x2, k=NSA_Q, n=d)
    keep = min(WINDOW, wk_all.shape[1])
    kvh = lambda a: a.reshape(b, a.shape[1], NSA_KV, HEAD_DIM)
    return (x2.reshape(b, t, d), kvh(ck), kvh(cv), kvh(skn), kvh(sv),
            kvh(wk_all[:, wk_all.shape[1] - keep:]), kvh(wv_all[:, wv_all.shape[1] - keep:]))


def _conv_ffn(x, conv_prev, li, g_norm, w_up, conv_w, conv_b, w_down):
    b, t, d = x.shape
    x2 = x.reshape(b * t, d)
    u = (yield _ws(rmsnorm_rows(x2, g_norm), w_up, li, k=d, n=2 * D_FF)).reshape(b, t, 2 * D_FF)
    hist = _history8(conv_prev, b, 2 * D_FF)
    act = dwconv(u, 0, D_FF, hist, conv_w, conv_b, True, BF16)
    state = jnp.concatenate([hist, u], axis=1)[:, -(FFN_CONV - 1):]
    x2 = yield _ws(act.reshape(b * t, D_FF), w_down, li, res=x2, k=D_FF, n=d)
    return x2.reshape(b, t, d), state


def kernel(x_prompt, x_sample, cache_moba_k, cache_moba_v, state_ssd, state_ssd_conv, cache_nsa_cmp_k, cache_nsa_cmp_v, cache_nsa_sel_k, cache_nsa_sel_v, state_nsa_win_k, state_nsa_win_v, state_ffn_conv, page_table, norm_mix, norm_ffn, even_w_in, even_w_out, moba_qk_norm, ssd_conv_w, ssd_conv_b, ssd_dt_bias, ssd_a_log, ssd_d, ssd_norm, odd_w_in, odd_w_out, nsa_qk_norm, cmp_pe, cmp_w1, cmp_b1, cmp_w2, cmp_b2, ffn_w_up, ffn_conv_w, ffn_conv_b, ffn_w_down):
    depth = norm_mix.shape[0]
    past_len = page_table.shape[1] * PAGE
    state_keys = ("moba_k", "moba_v", "ssd", "ssd_conv", "nsa_cmp_k", "nsa_cmp_v",
                  "nsa_sel_k", "nsa_sel_v", "nsa_win_k", "nsa_win_v", "ffn_conv")
    moba_pool = lambda c: c.reshape(c.shape[0], c.shape[1], PAGE * MOBA_HEADS, HEAD_DIM)
    cache_mk, cache_mv = moba_pool(cache_moba_k), moba_pool(cache_moba_v)

    def trunk(x, q_start, sample):
        b = x.shape[0]
        new = {name: [] for name in state_keys}
        for li in range(depth):
            e = li // 2
            if li % 2 == 0:
                cp = state_ssd_conv[e] if sample else None
                sp = state_ssd[e] if sample else jnp.zeros((b, SSD_HEADS, SSD_HEAD_DIM, SSD_STATE), F32)
                x, k, v, cs, ss = yield from _even_layer(x, sample, q_start, cache_mk, cache_mv, e, page_table, cp, sp,
                                                         norm_mix[li], even_w_in, even_w_out, moba_qk_norm[e],
                                                         ssd_conv_w[e], ssd_conv_b[e], ssd_dt_bias[e], ssd_a_log[e],
                                                         ssd_d[e], ssd_norm[e])
                for name, val in zip(("moba_k", "moba_v", "ssd_conv", "ssd"), (k, v, cs, ss)):
                    new[name].append(val)
            else:
                caches = (cache_nsa_cmp_k, cache_nsa_cmp_v, cache_nsa_sel_k, cache_nsa_sel_v)
                outs = yield from _odd_layer(x, sample, q_start, caches, e, page_table, state_nsa_win_k,
                                             state_nsa_win_v, norm_mix[li], odd_w_in, odd_w_out, nsa_qk_norm[e],
                                             cmp_pe[e], cmp_w1[e], cmp_b1[e], cmp_w2[e], cmp_b2[e])
                x = outs[0]
                for name, val in zip(("nsa_cmp_k", "nsa_cmp_v", "nsa_sel_k", "nsa_sel_v", "nsa_win_k", "nsa_win_v"), outs[1:]):
                    new[name].append(val)
            fp = state_ffn_conv[li] if sample else None
            x, fs = yield from _conv_ffn(x, fp, li, norm_ffn[li], ffn_w_up, ffn_conv_w[li], ffn_conv_b[li], ffn_w_down)
            new["ffn_conv"].append(fs)
        return x, {name: jnp.stack(rows) for name, rows in new.items()}

    (y_prompt, sp), (y_sample, ss) = _run_pair(trunk(x_prompt, 0, False), trunk(x_sample, past_len, True))
    out = [y_prompt, y_sample]
    for name in state_keys:
        out += [sp[name], ss[name]]
    return tuple(out)
```

```python
import functools

import numpy as np
import jax
import jax.numpy as jnp
from jax import lax
from jax.experimental import pallas as pl
from jax.experimental.pallas import tpu as pltpu

F32 = jnp.float32
BF16 = jnp.bfloat16

LANE = 128
VMEM_LIMIT_BYTES = 56 * 1024 * 1024

EPS = 1e-6
HEAD_DIM = 128
SCALE = HEAD_DIM ** -0.5
EXP2_SCALE = SCALE * float(np.log2(np.e))
NEG = -1e30

PAGE = 128
MOBA_HEADS = 16
MOBA_BLOCK = 256
MOBA_TOPK = 3
SSD_HEADS = 64
SSD_HEAD_DIM = 64
SSD_GROUPS = 8
SSD_STATE = 128
SSD_CONV = 4
SSD_CHUNK = 128
D_INNER = 4096
GROUP_W = D_INNER // SSD_GROUPS
HEADS_PER_GROUP = SSD_HEADS // SSD_GROUPS
NSA_KV = 4
NSA_HPG = 8
KV_W = NSA_KV * HEAD_DIM
CMP_LEN = 32
CMP_STRIDE = 16
SEL_BLOCK = 64
SEL_TOPN = 16
WINDOW = 512
FFN_CONV = 3


def _params(*sem):
    return pltpu.CompilerParams(dimension_semantics=sem, vmem_limit_bytes=VMEM_LIMIT_BYTES)


def _dot(a, b):
    return jnp.dot(a, b, preferred_element_type=F32)


def _dot_nt(a, b):
    return lax.dot_general(a, b, (((1,), (1,)), ((), ())), preferred_element_type=F32)


def _dot_tn(a, b):
    return lax.dot_general(a, b, (((0,), (0,)), ((), ())), preferred_element_type=F32)


def _split3(x):
    hi = x.astype(BF16)
    r1 = x - hi.astype(F32)
    mid = r1.astype(BF16)
    lo = (r1 - mid.astype(F32)).astype(BF16)
    return hi, mid, lo


def _sigmoid(x):
    return 1.0 / (1.0 + jnp.exp(-x))


def _silu(x):
    return x * _sigmoid(x)


def _softplus(x):
    return jnp.maximum(x, 0.0) + jnp.log1p(jnp.exp(-jnp.abs(x)))


def _gelu_tanh(x):
    return 0.5 * x * (1.0 + jnp.tanh(np.sqrt(2.0 / np.pi).astype(np.float32) * (x + 0.044715 * (x * x * x))))


def _rms(x, g):
    return x * lax.rsqrt(jnp.mean(x * x, axis=-1, keepdims=True) + EPS) * g


def _iota(shape, axis):
    return lax.broadcasted_iota(jnp.int32, shape, axis)


def _rank_before(score, n_cols):
    col = _iota(score.shape, 1)
    rank = jnp.zeros(score.shape, F32)
    for c in range(n_cols):
        sc = score[:, c:c + 1]
        rank = rank + jnp.where((sc > score) | ((sc == score) & (c < col)), 1.0, 0.0)
    return rank


def _pick_col(x, c):
    return jnp.sum(jnp.where(_iota(x.shape, 1) == c, x, 0.0), axis=1, keepdims=True)


def _online_update(carry, s, msk, v_bf16):
    m, l, acc = carry
    s = jnp.where(msk, s, NEG)
    m_new = jnp.maximum(m, jnp.max(s, axis=1, keepdims=True))
    p = jnp.where(msk, jnp.exp(s - m_new), 0.0)
    alpha = jnp.exp(m - m_new)
    l = alpha * l + jnp.sum(p, axis=1, keepdims=True)
    acc = alpha * acc + _dot(p.astype(BF16), v_bf16)
    return m_new, l, acc


def _flash_init(rows, dv):
    return (jnp.full((rows, 1), NEG, F32), jnp.zeros((rows, 1), F32), jnp.zeros((rows, dv), F32))


def _flash_out(carry):
    _, l, acc = carry
    return acc / jnp.where(l > 0.0, l, 1.0)


def _rmsnorm_kernel(x_ref, g_ref, o_ref):
    o_ref[...] = _rms(x_ref[...], g_ref[...]).astype(o_ref.dtype)


def rmsnorm_rows(x2d, g):
    m, d = x2d.shape
    tm = min(m, 256)
    return pl.pallas_call(
        _rmsnorm_kernel,
        grid=(m // tm,),
        in_specs=[pl.BlockSpec((tm, d), lambda i: (i, 0)), pl.BlockSpec((1, d), lambda i: (0, 0))],
        out_specs=pl.BlockSpec((tm, d), lambda i: (i, 0)),
        out_shape=jax.ShapeDtypeStruct((m, d), BF16),
        compiler_params=_params("parallel"),
        name="rmsnorm_rows",
    )(x2d, g.reshape(1, d))


def _matmul_kernel(*refs, nk, has_res, b_t):
    if has_res:
        a_ref, b_ref, r_ref, o_ref = refs
    else:
        a_ref, b_ref, o_ref = refs
    part = (_dot_nt if b_t else _dot)(a_ref[...].astype(BF16), b_ref[...].astype(BF16))
    if nk == 1:
        o_ref[...] = part + r_ref[...] if has_res else part
        return
    k = pl.program_id(2)

    @pl.when(k == 0)
    def _():
        o_ref[...] = part + r_ref[...] if has_res else part

    @pl.when(k > 0)
    def _():
        o_ref[...] += part


def matmul(a, b, res=None, b_t=False, tm=1024, tn=512, tk=4096):
    m, k = a.shape
    n = b.shape[0] if b_t else b.shape[1]
    tm, tn, tk = min(tm, m), min(tn, n), min(tk, k)
    assert m % tm == 0 and n % tn == 0 and k % tk == 0, (a.shape, b.shape)
    nk = k // tk
    b_spec = pl.BlockSpec((tn, tk), lambda i, j, kk: (j, kk)) if b_t else pl.BlockSpec((tk, tn), lambda i, j, kk: (kk, j))
    in_specs = [pl.BlockSpec((tm, tk), lambda i, j, kk: (i, kk)), b_spec]
    args = [a, b]
    if res is not None:
        in_specs.append(pl.BlockSpec((tm, tn), lambda i, j, kk: (i, j)))
        args.append(res)
    return pl.pallas_call(
        functools.partial(_matmul_kernel, nk=nk, has_res=res is not None, b_t=b_t),
        grid=(m // tm, n // tn, nk),
        in_specs=in_specs,
        out_specs=pl.BlockSpec((tm, tn), lambda i, j, kk: (i, j)),
        out_shape=jax.ShapeDtypeStruct((m, n), F32),
        compiler_params=_params("parallel", "parallel", "arbitrary"),
        name="matmul",
    )(*args)


def _matmul_ws_kernel(*refs, nk, has_res, has_small, w_t):
    refs = list(refs)
    a_ref, w_ref = refs.pop(0), refs.pop(0)
    r_ref = refs.pop(0) if has_res else None
    a2_ref = refs.pop(0) if has_small else None
    r2_ref = refs.pop(0) if has_small and has_res else None
    o_ref = refs.pop(0)
    o2_ref = refs.pop(0) if has_small else None
    wbf_ref, = refs
    i = pl.program_id(1)
    kk = pl.program_id(2)
    mm = _dot_nt if w_t else _dot

    def accumulate(out_ref, part, res_ref):
        if nk == 1:
            out_ref[...] = part + res_ref[...] if has_res else part
            return

        @pl.when(kk == 0)
        def _():
            out_ref[...] = part + res_ref[...] if has_res else part

        @pl.when(kk > 0)
        def _():
            out_ref[...] += part

    @pl.when(i == 0)
    def _():
        wbf_ref[kk] = w_ref[...].astype(BF16)
        if has_small:
            accumulate(o2_ref, mm(a2_ref[...], wbf_ref[kk]), r2_ref)

    accumulate(o_ref, mm(a_ref[...], wbf_ref[kk]), r_ref)


def matmul_ws(a, w, layer, *, k, n, res=None, small=None, small_res=None, out_cols=None, w_t=False,
              tm=1024, tn=512, tk=4096):
    m = a.shape[0]
    tm, tn, tk = min(tm, m), min(tn, n), min(tk, k)
    assert m % tm == 0 and n % tn == 0 and k % tk == 0, (a.shape, w.shape, k, n)
    assert (res is None) == (small_res is None) or small is None
    nk = k // tk
    k_idx = (lambda i, kk: 0) if nk == 1 else (lambda i, kk: jnp.where(i == 0, kk, nk - 1))
    if w_t:
        w_spec = pl.BlockSpec((None, tn, tk), lambda j, i, kk: (layer, j, k_idx(i, kk)))
    else:
        w_spec = pl.BlockSpec((None, tk, tn), lambda j, i, kk: (layer, k_idx(i, kk), j))
    col = (lambda j: j) if out_cols is None else out_cols
    o_spec = pl.BlockSpec((tm, tn), lambda j, i, kk: (i, col(j)))
    in_specs = [pl.BlockSpec((tm, tk), lambda j, i, kk: (i, kk)), w_spec]
    args = [a, w]
    out_specs, out_shape = [o_spec], [jax.ShapeDtypeStruct((m, n), F32)]
    if res is not None:
        in_specs.append(o_spec)
        args.append(res)
    if small is not None:
        m2 = small.shape[0]
        o2_spec = pl.BlockSpec((m2, tn), lambda j, i, kk: (0, col(j)))
        in_specs.append(pl.BlockSpec((m2, tk), lambda j, i, kk: (0, k_idx(i, kk))))
        args.append(small)
        if res is not None:
            in_specs.append(o2_spec)
            args.append(small_res)
        out_specs.append(o2_spec)
        out_shape.append(jax.ShapeDtypeStruct((m2, n), F32))
    outs = pl.pallas_call(
        functools.partial(_matmul_ws_kernel, nk=nk, has_res=res is not None, has_small=small is not None, w_t=w_t),
        grid=(n // tn, m // tm, nk),
        in_specs=in_specs,
        out_specs=out_specs,
        out_shape=out_shape,
        scratch_shapes=[pltpu.VMEM((nk, tn, tk) if w_t else (nk, tk, tn), BF16)],
        compiler_params=_params("arbitrary", "arbitrary", "arbitrary"),
        name="matmul_ws",
    )(*args)
    return outs[0] if small is None else tuple(outs)


def _head_norm_kernel(x_ref, g_ref, o_ref, *, heads):
    g = g_ref[...]
    for h in range(heads):
        sl = slice(h * HEAD_DIM, (h + 1) * HEAD_DIM)
        o_ref[:, sl] = _rms(x_ref[:, sl], g).astype(o_ref.dtype)


def head_norm(x, col0, n_heads, g, out_dtype):
    b, t, _ = x.shape
    hb = min(n_heads, 4)
    w = hb * HEAD_DIM
    assert col0 % w == 0 and n_heads % hb == 0
    tq = min(t, 512)
    return pl.pallas_call(
        functools.partial(_head_norm_kernel, heads=hb),
        grid=(b, t // tq, n_heads // hb),
        in_specs=[pl.BlockSpec((None, tq, w), lambda bi, i, j: (bi, i, col0 // w + j)),
                  pl.BlockSpec((1, HEAD_DIM), lambda bi, i, j: (0, 0))],
        out_specs=pl.BlockSpec((None, tq, w), lambda bi, i, j: (bi, i, j)),
        out_shape=jax.ShapeDtypeStruct((b, t, n_heads * HEAD_DIM), out_dtype),
        compiler_params=_params("parallel", "parallel", "parallel"),
        name="head_norm",
    )(x, g.reshape(1, HEAD_DIM))


CONV_ROWS = 256


def _dwconv_kernel(*refs, width, t, glu):
    if glu:
        xa_ref, xg_ref, pa_ref, pg_ref, wa_ref, wg_ref, ba_ref, bg_ref, o_ref, sa_ref, sg_ref = refs
        pairs = ((xa_ref, pa_ref, sa_ref), (xg_ref, pg_ref, sg_ref))
    else:
        xa_ref, pa_ref, wa_ref, ba_ref, o_ref, sa_ref = refs
        pairs = ((xa_ref, pa_ref, sa_ref),)
    for x_ref, p_ref, s_ref in pairs:
        s_ref[0:8, :] = p_ref[...]
        s_ref[8:8 + t, :] = x_ref[...]
    tc = min(t, CONV_ROWS)

    def conv(s_ref, w_ref, b_ref, r0):
        acc = b_ref[...]
        for i in range(width):
            acc = acc + w_ref[i:i + 1, :] * s_ref[pl.ds(8 - (width - 1) + i + r0, tc), :]
        return acc

    for c in range(t // tc):
        r0 = c * tc
        a = conv(sa_ref, wa_ref, ba_ref, r0)
        if glu:
            o_ref[r0:r0 + tc, :] = (a * _silu(conv(sg_ref, wg_ref, bg_ref, r0))).astype(o_ref.dtype)
        else:
            o_ref[r0:r0 + tc, :] = _silu(a).astype(o_ref.dtype)


def dwconv(x, col0, c_out, prev8, w, bias, glu, out_dtype):
    b, t, _ = x.shape
    width = w.shape[0]
    tc = min(256 if t > 64 else 2048, c_out)
    assert col0 % tc == 0 and c_out % tc == 0
    nj = c_out // tc
    x_spec = lambda off: pl.BlockSpec((None, t, tc), lambda bi, j: (bi, 0, off + j))
    p_spec = lambda off: pl.BlockSpec((None, 8, tc), lambda bi, j: (bi, 0, off + j))
    w_spec = lambda off: pl.BlockSpec((width, tc), lambda bi, j: (0, off + j))
    b_spec = lambda off: pl.BlockSpec((1, tc), lambda bi, j: (0, off + j))
    bias2 = bias.reshape(1, -1)
    if glu:
        in_specs = [x_spec(col0 // tc), x_spec(col0 // tc + nj), p_spec(0), p_spec(nj),
                    w_spec(0), w_spec(nj), b_spec(0), b_spec(nj)]
        args = (x, x, prev8, prev8, w, w, bias2, bias2)
        scratch = [pltpu.VMEM((8 + t, tc), F32), pltpu.VMEM((8 + t, tc), F32)]
    else:
        in_specs = [x_spec(col0 // tc), p_spec(0), w_spec(0), b_spec(0)]
        args = (x, prev8, w, bias2)
        scratch = [pltpu.VMEM((8 + t, tc), F32)]
    return pl.pallas_call(
        functools.partial(_dwconv_kernel, width=width, t=t, glu=glu),
        grid=(b, nj),
        in_specs=in_specs,
        out_specs=pl.BlockSpec((None, t, tc), lambda bi, j: (bi, 0, j)),
        out_shape=jax.ShapeDtypeStruct((b, t, c_out), out_dtype),
        scratch_shapes=scratch,
        compiler_params=_params("parallel", "parallel"),
        name="dwconv_glu" if glu else "dwconv_silu",
    )(*args)


def _history8(state, b, c):
    if state is None:
        return jnp.zeros((b, 8, c), F32)
    return jnp.concatenate([jnp.zeros((b, 8 - state.shape[1], c), F32), state], axis=1)


def _moba_tile(i, nb, q_ref, o_ref, kmean_ref, kbf_ref, vbf_ref, spread_ref):
    blk, half = MOBA_BLOCK, MOBA_BLOCK // 2
    fold = lambda a, op: functools.reduce(op, [a[:, c * LANE:(c + 1) * LANE] for c in range(a.shape[1] // LANE)])
    k_blk = lambda n: kbf_ref[n * blk:(n + 1) * blk, :]
    v_blk = lambda n: vbf_ref[n * blk:(n + 1) * blk, :]
    if i > 0:
        gate = _dot_nt(kmean_ref[...].astype(BF16), q_ref[...])[:-(-nb // 8) * 8, :]
        blk_id = _iota(gate.shape, 0)
        valid = blk_id < i
        gate = jnp.where(valid, gate, -jnp.inf)
        rank = jnp.zeros(gate.shape, F32)
        for c in range(i):
            gc = gate[c:c + 1, :]
            rank = rank + jnp.where((gc > gate) | ((gc == gate) & (c < blk_id)), 1.0, 0.0)
        sel_t = jnp.where(valid & (rank < min(MOBA_TOPK, nb)), 1.0, 0.0)
        sel_t = jnp.concatenate([sel_t, jnp.zeros((LANE - sel_t.shape[0], blk), F32)], axis=0).astype(BF16)
        selb_all = _dot_tn(sel_t, spread_ref[:, :i * LANE])
    for hq in range(2):
        q = q_ref[hq * half:(hq + 1) * half, :]
        masks = []
        if i > 0:
            selb = selb_all[hq * half:(hq + 1) * half, :]
            masks = [jnp.concatenate([selb[:, n * LANE:(n + 1) * LANE]] * (blk // LANE), axis=1) > 0.5 for n in range(i)]
        causal = _iota((half, blk), 1) <= _iota((half, blk), 0) + hq * half
        masks.append(causal)
        scores = lambda n: jnp.where(masks[n], _dot_nt(q, k_blk(n)), NEG)
        part = functools.reduce(jnp.maximum, [fold(scores(n), jnp.maximum) for n in range(i + 1)])
        m = jnp.max(part, axis=1, keepdims=True)
        l_part, acc = None, None
        for n in range(i + 1):
            p = jnp.exp2((scores(n) - m) * EXP2_SCALE)
            pv = _dot(p.astype(BF16), v_blk(n))
            l_part = fold(p, jnp.add) if n == 0 else l_part + fold(p, jnp.add)
            acc = pv if n == 0 else acc + pv
        l = jnp.sum(l_part, axis=1, keepdims=True)
        o_ref[hq * half:(hq + 1) * half, :] = (acc / l).astype(o_ref.dtype)


def _moba_tiles_kernel(q_ref, k_ref, v_ref, o_ref, kmean_ref, kbf_ref, vbf_ref, spread_ref, *, nb):
    blk = MOBA_BLOCK
    i = pl.program_id(2)

    @pl.when(i == 0)
    def _():
        kmean_ref[...] = jnp.zeros_like(kmean_ref)
        for n in range(nb):
            kmean_ref[n:n + 1, :] = jnp.mean(k_ref[n * blk:(n + 1) * blk, :], axis=0, keepdims=True)
        kbf_ref[...] = k_ref[...].astype(BF16)
        vbf_ref[...] = v_ref[...].astype(BF16)
        shape = spread_ref.shape
        spread_ref[...] = jnp.where(_iota(shape, 0) == _iota(shape, 1) // LANE, 1.0, 0.0).astype(BF16)

    for c in range(nb):
        @pl.when(i == c)
        def _(c=c):
            _moba_tile(c, nb, q_ref, o_ref, kmean_ref, kbf_ref, vbf_ref, spread_ref)


def moba_prompt(qn, kn, proj, v_col0):
    b, t, w = qn.shape
    nh = w // HEAD_DIM
    nb = t // MOBA_BLOCK
    assert t % MOBA_BLOCK == 0 and nb <= LANE
    return pl.pallas_call(
        functools.partial(_moba_tiles_kernel, nb=nb),
        grid=(b, nh, nb),
        in_specs=[pl.BlockSpec((None, MOBA_BLOCK, HEAD_DIM), lambda bi, h, i: (bi, i, h)),
                  pl.BlockSpec((None, t, HEAD_DIM), lambda bi, h, i: (bi, 0, h)),
                  pl.BlockSpec((None, t, HEAD_DIM), lambda bi, h, i: (bi, 0, v_col0 // HEAD_DIM + h))],
        out_specs=pl.BlockSpec((None, MOBA_BLOCK, HEAD_DIM), lambda bi, h, i: (bi, i, h)),
        out_shape=jax.ShapeDtypeStruct((b, t, w), BF16),
        scratch_shapes=[pltpu.VMEM((LANE, HEAD_DIM), F32), pltpu.VMEM((t, HEAD_DIM), BF16), pltpu.VMEM((t, HEAD_DIM), BF16),
                        pltpu.VMEM((LANE, nb * LANE), BF16)],
        compiler_params=_params("parallel", "parallel", "arbitrary"),
        name="moba_prompt",
    )(qn, kn, proj)


def _moba_dec_block_kernel(pt_ref, q_ref, ka_ref, kb_ref, va_ref, vb_ref,
                           acc_ref, g_ref, m_ref, l_ref, *, nh, tq):
    n = pl.program_id(1)
    q = q_ref[...]
    kb = jnp.concatenate([ka_ref[...], kb_ref[...]], axis=0).astype(BF16)
    vb = jnp.concatenate([va_ref[...], vb_ref[...]], axis=0).astype(BF16)
    s_raw = _dot_nt(q, kb)
    same_head = _iota(s_raw.shape, 1) % nh == _iota(s_raw.shape, 0) // tq
    gate = jnp.sum(jnp.where(same_head, s_raw, 0.0), axis=1, keepdims=True) * (1.0 / MOBA_BLOCK)
    s = jnp.where(same_head, s_raw * SCALE, NEG)
    m = jnp.max(s, axis=1, keepdims=True)
    p = jnp.exp(s - m)
    l = jnp.sum(p, axis=1, keepdims=True)
    acc_ref[...] = _dot(p.astype(BF16), vb)

    @pl.when(n == 0)
    def _():
        g_ref[...] = jnp.zeros_like(g_ref)
        m_ref[...] = jnp.zeros_like(m_ref)
        l_ref[...] = jnp.zeros_like(l_ref)

    lane = _iota(g_ref.shape, 1)
    g_ref[...] = jnp.where(lane == n, gate, g_ref[...])
    m_ref[...] = jnp.where(lane == n, m, m_ref[...])
    l_ref[...] = jnp.where(lane == n, l, l_ref[...])


def _moba_dec_combine_kernel(g_ref, m_ref, l_ref, acc_ref, q_ref, kn_ref, vn_ref, o_ref, *, nbk, nh, tq):
    gate = g_ref[...]
    lane = _iota(gate.shape, 1)
    row = _iota(gate.shape, 0)
    valid = lane < nbk
    gate = jnp.where(valid, gate, -jnp.inf)
    sel = valid & (_rank_before(gate, nbk) < min(MOBA_TOPK, nbk + 1))
    q = q_ref[...]
    s_own = _dot_nt(q, kn_ref[...].astype(BF16)) * SCALE
    own = (lane % nh == row // tq) & (lane // nh <= row % tq)
    m_blk = m_ref[...]
    m_tot = jnp.maximum(jnp.max(jnp.where(sel, m_blk, NEG), axis=1, keepdims=True),
                        jnp.max(jnp.where(own, s_own, NEG), axis=1, keepdims=True))
    wgt = jnp.where(sel, jnp.exp(m_blk - m_tot), 0.0)
    p_own = jnp.where(own, jnp.exp(s_own - m_tot), 0.0)
    l_tot = jnp.sum(wgt * l_ref[...], axis=1, keepdims=True) + jnp.sum(p_own, axis=1, keepdims=True)
    acc = _dot(p_own.astype(BF16), vn_ref[...].astype(BF16))
    for n in range(nbk):
        acc = acc + wgt[:, n:n + 1] * acc_ref[n]
    o_ref[...] = acc / jnp.where(l_tot > 0.0, l_tot, 1.0)


def moba_sample(qn, kn_new, v_new, cache_k, cache_v, layer, page_table):
    b, tq, w = qn.shape
    nh = w // HEAD_DIM
    rows = nh * tq
    npages = page_table.shape[1]
    ppb = MOBA_BLOCK // PAGE
    nbk = npages // ppb
    assert rows == LANE and ppb == 2 and nbk <= LANE
    qrows = qn.reshape(b, tq, nh, HEAD_DIM).transpose(0, 2, 1, 3).reshape(b, rows, HEAD_DIM)
    kn_rows = kn_new.reshape(b, tq * nh, HEAD_DIM)
    vn_rows = v_new.reshape(b, tq * nh, HEAD_DIM)

    page = lambda j: pl.BlockSpec((None, None, PAGE * nh, HEAD_DIM), lambda bi, n, pt: (layer, pt[bi, ppb * n + j], 0, 0))
    stat = pl.BlockSpec((None, rows, LANE), lambda bi, n, pt: (bi, 0, 0))
    acc, g, m, l = pl.pallas_call(
        functools.partial(_moba_dec_block_kernel, nh=nh, tq=tq),
        grid_spec=pltpu.PrefetchScalarGridSpec(
            num_scalar_prefetch=1,
            grid=(b, nbk),
            in_specs=[pl.BlockSpec((None, rows, HEAD_DIM), lambda bi, n, pt: (bi, 0, 0)),
                      page(0), page(1), page(0), page(1)],
            out_specs=[pl.BlockSpec((None, None, rows, HEAD_DIM), lambda bi, n, pt: (bi, n, 0, 0)), stat, stat, stat],
        ),
        out_shape=[jax.ShapeDtypeStruct((b, nbk, rows, HEAD_DIM), F32)] + [jax.ShapeDtypeStruct((b, rows, LANE), F32)] * 3,
        compiler_params=_params("parallel", "arbitrary"),
        name="moba_sample_blocks",
    )(page_table, qrows, cache_k, cache_k, cache_v, cache_v)

    per_b = lambda *shape: pl.BlockSpec((None,) + shape, lambda bi: (bi,) + (0,) * len(shape))
    o = pl.pallas_call(
        functools.partial(_moba_dec_combine_kernel, nbk=nbk, nh=nh, tq=tq),
        grid=(b,),
        in_specs=[per_b(rows, LANE), per_b(rows, LANE), per_b(rows, LANE), per_b(nbk, rows, HEAD_DIM),
                  per_b(rows, HEAD_DIM), per_b(tq * nh, HEAD_DIM), per_b(tq * nh, HEAD_DIM)],
        out_specs=per_b(rows, HEAD_DIM),
        out_shape=jax.ShapeDtypeStruct((b, rows, HEAD_DIM), F32),
        compiler_params=_params("parallel"),
        name="moba_sample_combine",
    )(g, m, l, acc, qrows, kn_rows, vn_rows)
    return o.reshape(b, nh, tq, HEAD_DIM).transpose(0, 2, 1, 3).reshape(b, tq, w)


def _ssd_kernel(x_ref, bm_ref, cm_ref, z_ref, dtc_ref, dtr_ref, bias_c_ref, bias_r_ref, alog_c_ref, alog_r_ref,
                dskip_ref, gout_ref, s0_ref, pre_ref, y_ref, sfin_ref, state_ref, *, q, t_valid):
    c = pl.program_id(1)
    nc = pl.num_programs(1)
    pre_w = pre_ref.shape[1]
    y_ref[:, :pre_w] = pre_ref[...]

    @pl.when(c == 0)
    def _():
        state_ref[...] = s0_ref[...]

    dtc = _softplus(dtc_ref[...] + bias_c_ref[...])
    dtr = _softplus(dtr_ref[...] + bias_r_ref[...])
    if t_valid is not None:
        dtc = jnp.where(c * q + _iota(dtc.shape, 0) < t_valid, dtc, 0.0)
        dtr = jnp.where(c * q + _iota(dtr.shape, 1) < t_valid, dtr, 0.0)
    da_c = dtc * -jnp.exp(alog_c_ref[...])
    da_r = dtr * -jnp.exp(alog_r_ref[...])
    tri = _iota((q, q), 0) >= _iota((q, q), 1)
    tri_bf = jnp.where(tri, 1.0, 0.0).astype(BF16)
    cum_c = sum(_dot(tri_bf, part) for part in _split3(da_c))
    cum_r = sum(_dot_nt(part, tri_bf) for part in _split3(da_r))
    lane = _iota((q, LANE), 1)
    lo_half = lane < SSD_HEAD_DIM
    row_lo = _iota((LANE, 1), 0) < SSD_HEAD_DIM
    pairs = HEADS_PER_GROUP // 2

    for g in range(SSD_GROUPS):
        xg = x_ref[:, g * GROUP_W:(g + 1) * GROUP_W]
        bm = bm_ref[:, g * SSD_STATE:(g + 1) * SSD_STATE].astype(BF16)
        cm = cm_ref[:, g * SSD_STATE:(g + 1) * SSD_STATE].astype(BF16)
        cb = _dot_nt(cm, bm)
        y_parts = []
        for pr in range(pairs):
            x_pair = xg[:, pr * LANE:(pr + 1) * LANE]
            x_bf = x_pair.astype(BF16)
            ys, cols, lasts = [], [], []
            for h in (g * HEADS_PER_GROUP + 2 * pr, g * HEADS_PER_GROUP + 2 * pr + 1):
                col = cum_c[:, h:h + 1]
                seg = col - cum_r[h:h + 1, :]
                decay = jnp.where(tri, jnp.exp(jnp.where(tri, seg, 0.0)), 0.0)
                wgt = cb * decay * dtr[h:h + 1, :]
                ys.append(_dot(wgt.astype(BF16), x_bf))
                cols.append(col)
                lasts.append(cum_c[q - 1:q, h:h + 1])
            sidx = g * pairs + pr
            state = state_ref[sidx]
            y_pair = jnp.where(lo_half, ys[0], ys[1])
            carry_in = _dot_nt(cm, state.astype(BF16))
            y_pair = y_pair + carry_in * jnp.exp(jnp.where(lo_half, cols[0], cols[1]))
            y_parts.append(y_pair)
            tail = jnp.where(lo_half,
                             jnp.exp(lasts[0] - cols[0]) * dtc[:, 2 * sidx:2 * sidx + 1],
                             jnp.exp(lasts[1] - cols[1]) * dtc[:, 2 * sidx + 1:2 * sidx + 2])
            upd = _dot_tn((x_pair * tail).astype(BF16), bm)
            state_ref[sidx] = state * jnp.where(row_lo, jnp.exp(lasts[0]), jnp.exp(lasts[1])) + upd
        sl = slice(g * GROUP_W, (g + 1) * GROUP_W)
        yg = jnp.concatenate(y_parts, axis=1) + xg * dskip_ref[:, sl]
        yg = yg * _silu(z_ref[:, sl])
        y_ref[:, pre_w + g * GROUP_W:pre_w + (g + 1) * GROUP_W] = _rms(yg, gout_ref[:, sl]).astype(y_ref.dtype)

    @pl.when(c == nc - 1)
    def _():
        sfin_ref[...] = state_ref[...]


def ssd_mixer(conv, z_src, z_col0, dt_raw, s0, dt_bias, a_log, d_skip, g_out, t_valid, prefix):
    b, t, _ = conv.shape
    pre_w = prefix.shape[-1]
    q = SSD_CHUNK
    assert t % q == 0
    nc = t // q
    dtr = dt_raw.transpose(0, 2, 1)
    pad_h = LANE - SSD_HEADS
    bias_c = jnp.pad(dt_bias, (0, pad_h)).reshape(1, LANE)
    bias_r = bias_c.reshape(LANE, 1)
    alog_c = jnp.pad(a_log, (0, pad_h)).reshape(1, LANE)
    alog_r = alog_c.reshape(LANE, 1)
    dskip = jnp.repeat(d_skip, SSD_HEAD_DIM).reshape(1, D_INNER)
    npair = SSD_HEADS // 2
    s0p = s0.reshape(b, npair, 2 * SSD_HEAD_DIM, SSD_STATE)
    nbc = D_INNER // (SSD_GROUPS * SSD_STATE)
    const = lambda shape: pl.BlockSpec(shape, lambda bi, c: (0,) * len(shape))
    y, sfin = pl.pallas_call(
        functools.partial(_ssd_kernel, q=q, t_valid=t_valid),
        grid=(b, nc),
        in_specs=[pl.BlockSpec((None, q, D_INNER), lambda bi, c: (bi, c, 0)),
                  pl.BlockSpec((None, q, SSD_GROUPS * SSD_STATE), lambda bi, c: (bi, c, nbc)),
                  pl.BlockSpec((None, q, SSD_GROUPS * SSD_STATE), lambda bi, c: (bi, c, nbc + 1)),
                  pl.BlockSpec((None, q, D_INNER), lambda bi, c: (bi, c, z_col0 // D_INNER)),
                  pl.BlockSpec((None, q, LANE), lambda bi, c: (bi, c, 0)),
                  pl.BlockSpec((None, LANE, q), lambda bi, c: (bi, 0, c)),
                  const((1, LANE)), const((LANE, 1)), const((1, LANE)), const((LANE, 1)),
                  const((1, D_INNER)), const((1, D_INNER)),
                  pl.BlockSpec((None, npair, LANE, SSD_STATE), lambda bi, c: (bi, 0, 0, 0)),
                  pl.BlockSpec((None, q, pre_w), lambda bi, c: (bi, c, 0))],
        out_specs=[pl.BlockSpec((None, q, pre_w + D_INNER), lambda bi, c: (bi, c, 0)),
                   pl.BlockSpec((None, npair, LANE, SSD_STATE), lambda bi, c: (bi, 0, 0, 0))],
        out_shape=[jax.ShapeDtypeStruct((b, t, pre_w + D_INNER), BF16),
                   jax.ShapeDtypeStruct((b, npair, LANE, SSD_STATE), F32)],
        scratch_shapes=[pltpu.VMEM((npair, LANE, SSD_STATE), F32)],
        compiler_params=_params("parallel", "arbitrary"),
        name="ssd_scan",
    )(conv, conv, conv, z_src, dt_raw, dtr, bias_c, bias_r, alog_c, alog_r, dskip, g_out.reshape(1, D_INNER), s0p,
      prefix)
    return y, sfin.reshape(b, SSD_HEADS, SSD_HEAD_DIM, SSD_STATE)


def _compress_kernel(pt_ref, a_ref, b_ref, cpe_a_ref, cpe_b_ref, b1_ref, w2_ref, b2_ref, g_ref, o_ref,
                     ha_ref, hb_ref, *, npg, norm):
    bi = pl.program_id(0)
    ncp = npg * 8

    def gather(p, _):
        src = pl.ds(pl.multiple_of(pt_ref[bi, p] * 8, 8), 8)
        dst = pl.ds(pl.multiple_of(p * 8, 8), 8)
        ha_ref[dst, :] = a_ref[src, :]
        hb_ref[dst, :] = b_ref[src, :]
        return 0

    lax.fori_loop(0, npg, gather, 0)
    hb_ref[ncp:ncp + 8, :] = jnp.zeros((8, HEAD_DIM), F32)
    hid = ha_ref[...] + hb_ref[pl.ds(1, ncp), :] + (cpe_a_ref[0:1, :] + cpe_b_ref[1:2, :] + b1_ref[...])
    out = _dot(_gelu_tanh(hid).astype(BF16), w2_ref[...]) + b2_ref[...]
    if norm:
        out = _rms(out, g_ref[...])
    o_ref[...] = out


ROWS_PER_TILE = 8


def _chunk_products_kernel(x_ref, w_ref, o_ref):
    @pl.when(pl.program_id(1) == 0)
    def _():
        o_ref[...] = jnp.zeros_like(o_ref)

    for r in range(ROWS_PER_TILE):
        g = r % NSA_KV
        prod = _dot(x_ref[:, r, :].astype(BF16), w_ref[r // NSA_KV])
        o_ref[:, g * HEAD_DIM:(g + 1) * HEAD_DIM] += prod[:, :HEAD_DIM]
        o_ref[:, KV_W + g * HEAD_DIM:KV_W + (g + 1) * HEAD_DIM] += prod[:, HEAD_DIM:]


def chunk_products(x, layer, w_pair):
    nck = x.shape[1]
    blk = min(nck, 1024)
    assert nck % blk == 0 and x.shape[2] == CMP_STRIDE * NSA_KV
    per_step = ROWS_PER_TILE // NSA_KV
    return pl.pallas_call(
        _chunk_products_kernel,
        grid=(nck // blk, CMP_STRIDE // per_step),
        in_specs=[pl.BlockSpec((None, blk, ROWS_PER_TILE, HEAD_DIM), lambda i, lp: (layer, i, lp, 0)),
                  pl.BlockSpec((per_step, HEAD_DIM, 2 * HEAD_DIM), lambda i, lp: (lp, 0, 0))],
        out_specs=pl.BlockSpec((blk, 2 * KV_W), lambda i, lp: (i, 0)),
        out_shape=jax.ShapeDtypeStruct((nck, 2 * KV_W), F32),
        compiler_params=_params("parallel", "arbitrary"),
        name="chunk_products",
    )(x, w_pair)


def compress_tokens(rows, layer, page_table, pe, w1, b1, w2, b2, g_norm):
    nck = rows.shape[1]
    b, npg = page_table.shape
    ncp = npg * 8
    w_pair = w1.reshape(2, CMP_STRIDE, HEAD_DIM, HEAD_DIM).transpose(1, 2, 0, 3).reshape(CMP_STRIDE, HEAD_DIM, 2 * HEAD_DIM)
    w_pair = w_pair.astype(BF16)
    ab = chunk_products(rows, layer, w_pair)
    pe_rows = jnp.broadcast_to(pe.reshape(2, CMP_STRIDE, 1, HEAD_DIM), (2, CMP_STRIDE, NSA_KV, HEAD_DIM))
    pe_rows = jnp.pad(pe_rows.reshape(1, 2, CMP_STRIDE * NSA_KV, HEAD_DIM), ((0, 0), (0, 6), (0, 0), (0, 0)))
    cpe = chunk_products(pe_rows, 0, w_pair)
    col = lambda off: (lambda bi, g, pt: (0, off + g))
    return pl.pallas_call(
        functools.partial(_compress_kernel, npg=npg, norm=g_norm is not None),
        grid_spec=pltpu.PrefetchScalarGridSpec(
            num_scalar_prefetch=1,
            grid=(b, NSA_KV),
            in_specs=[pl.BlockSpec((nck, HEAD_DIM), col(0)), pl.BlockSpec((nck, HEAD_DIM), col(NSA_KV)),
                      pl.BlockSpec((8, HEAD_DIM), col(0)), pl.BlockSpec((8, HEAD_DIM), col(NSA_KV)),
                      pl.BlockSpec((1, HEAD_DIM), lambda bi, g, pt: (0, 0)),
                      pl.BlockSpec((HEAD_DIM, HEAD_DIM), lambda bi, g, pt: (0, 0)),
                      pl.BlockSpec((1, HEAD_DIM), lambda bi, g, pt: (0, 0)),
                      pl.BlockSpec((1, HEAD_DIM), lambda bi, g, pt: (0, 0))],
            out_specs=pl.BlockSpec((None, ncp, HEAD_DIM), lambda bi, g, pt: (bi, 0, g)),
            scratch_shapes=[pltpu.VMEM((ncp, HEAD_DIM), F32), pltpu.VMEM((ncp + 8, HEAD_DIM), F32)],
        ),
        out_shape=jax.ShapeDtypeStruct((b, ncp, KV_W), F32),
        compiler_params=_params("parallel", "parallel"),
        name="nsa_compress",
    )(page_table, ab, ab, cpe, cpe, b1.reshape(1, -1), w2.astype(BF16), b2.reshape(1, -1),
      (g_norm if g_norm is not None else jnp.ones((HEAD_DIM,), F32)).reshape(1, -1))


def _nsa_cmp_kernel(q_ref, kc_ref, vc_ref, o_ref, sel_ref, *, tq, q_start, n_sel, nsp):
    i = pl.program_id(2)
    kc = kc_ref[...].astype(BF16)
    vc = vc_ref[...].astype(BF16)
    ncp = kc.shape[0]
    qpos = q_start + i * tq + _iota((tq, 1), 0)
    end = _iota((tq, ncp), 1) * CMP_STRIDE + (CMP_LEN - 1)
    msk = end <= qpos
    imp = jnp.zeros((tq, ncp), F32)
    for r in range(NSA_HPG):
        sl = slice(r * HEAD_DIM, (r + 1) * HEAD_DIM)
        s = jnp.where(msk, _dot_nt(q_ref[:, sl], kc) * SCALE, NEG)
        p = jnp.where(msk, jnp.exp(s - jnp.max(s, axis=1, keepdims=True)), 0.0)
        d = jnp.sum(p, axis=1, keepdims=True)
        p = p / jnp.where(d > 0.0, d, 1.0)
        o_ref[:, sl] = _dot(p.astype(BF16), vc)
        imp = imp + p
    cn = _iota((ncp, nsp), 0)
    sj = _iota((ncp, nsp), 1)
    overlap = jnp.where((cn >= 4 * sj - 1) & (cn <= 4 * sj + 3), 1.0, 0.0).astype(BF16)
    score = sum(_dot(part, overlap) for part in _split3(imp))
    j = _iota((tq, nsp), 1)
    qblk = qpos // SEL_BLOCK
    valid = j <= qblk
    forced = (j == 0) | (j == qblk) | (j == qblk - 1)
    score = jnp.where(valid, jnp.where(forced, jnp.inf, score), -jnp.inf)
    sel_ref[...] = jnp.where(valid & (_rank_before(score, n_sel) < min(SEL_TOPN, n_sel)), 1.0, 0.0)


def nsa_compressed(qn, kc, vc, q_start, n_sel):
    b, t, w = qn.shape
    ncp = kc.shape[1]
    gw = NSA_HPG * HEAD_DIM
    tq = min(t, 256)
    nsp = -(-n_sel // LANE) * LANE
    return pl.pallas_call(
        functools.partial(_nsa_cmp_kernel, tq=tq, q_start=q_start, n_sel=n_sel, nsp=nsp),
        grid=(b, NSA_KV, t // tq),
        in_specs=[pl.BlockSpec((None, tq, gw), lambda bi, g, i: (bi, i, g)),
                  pl.BlockSpec((None, ncp, HEAD_DIM), lambda bi, g, i: (bi, 0, g)),
                  pl.BlockSpec((None, ncp, HEAD_DIM), lambda bi, g, i: (bi, 0, g))],
        out_specs=[pl.BlockSpec((None, tq, gw), lambda bi, g, i: (bi, i, g)),
                   pl.BlockSpec((None, None, tq, nsp), lambda bi, g, i: (bi, g, i, 0))],
        out_shape=[jax.ShapeDtypeStruct((b, t, w), F32), jax.ShapeDtypeStruct((b, NSA_KV, t, nsp), F32)],
        compiler_params=_params("parallel", "parallel", "parallel"),
        name="nsa_compressed",
    )(qn, kc, vc)


def _nsa_prompt_kernel(q_ref, sk_ref, sv_ref, wk_ref, wv_ref, sel_ref, ocmp_ref, gl_ref, o_ref,
                       m_ref, part_ref, acc_ref, s_ref, *, tq, tk):
    i = pl.program_id(2)
    q0 = i * tq
    tpos = q0 + _iota((tq, tk), 0)
    sel = sel_ref[...].astype(BF16)
    nsp = sel.shape[1]

    def sel_mask(k0):
        kpos = k0 + _iota((tq, tk), 1)
        expand = jnp.where(_iota((nsp, tk), 0) == (k0 + _iota((nsp, tk), 1)) // SEL_BLOCK, 1.0, 0.0).astype(BF16)
        return (_dot(sel, expand) > 0.5) & (kpos <= tpos)

    def win_mask(k0):
        kpos = k0 + _iota((tq, tk), 1)
        return (kpos <= tpos) & (kpos > tpos - WINDOW)

    n_hi = (q0 + tq - 1) // tk + 1
    branches = ((sk_ref, sv_ref, sel_mask, 0), (wk_ref, wv_ref, win_mask, jnp.maximum(q0 - (WINDOW - 1), 0) // tk))
    fold = lambda a, op: functools.reduce(op, [a[:, c * LANE:(c + 1) * LANE] for c in range(tk // LANE)])
    for br, (k_ref, v_ref, mask_fn, n_lo) in enumerate(branches):
        base = br * NSA_HPG
        for r in range(NSA_HPG):
            part_ref[base + r] = jnp.full((tq, LANE), NEG, F32)
            acc_ref[base + r] = jnp.zeros((tq, HEAD_DIM), F32)

        def max_sweep(n, _, base=base, k_ref=k_ref, mask_fn=mask_fn, n_lo=n_lo):
            k0 = pl.multiple_of(n * tk, tk)
            kb, msk = k_ref[pl.ds(k0, tk), :].astype(BF16), mask_fn(k0)
            for r in range(NSA_HPG):
                s = jnp.where(msk, _dot_nt(q_ref[:, r * HEAD_DIM:(r + 1) * HEAD_DIM], kb), NEG)
                s_ref[n - n_lo, r] = s
                part_ref[base + r] = jnp.maximum(part_ref[base + r], fold(s, jnp.maximum))
            return 0

        lax.fori_loop(n_lo, n_hi, max_sweep, 0)
        for r in range(NSA_HPG):
            m_ref[base + r] = jnp.max(part_ref[base + r], axis=1, keepdims=True)
            part_ref[base + r] = jnp.zeros((tq, LANE), F32)

        def exp_sweep(n, _, base=base, v_ref=v_ref, n_lo=n_lo):
            vb = v_ref[pl.ds(pl.multiple_of(n * tk, tk), tk), :].astype(BF16)
            for r in range(NSA_HPG):
                p = jnp.exp2((s_ref[n - n_lo, r] - m_ref[base + r]) * EXP2_SCALE)
                part_ref[base + r] += fold(p, jnp.add)
                acc_ref[base + r] += _dot(p.astype(BF16), vb)
            return 0

        lax.fori_loop(n_lo, n_hi, exp_sweep, 0)

    gates = [_sigmoid(gl_ref[c]) for c in range(3)]
    for r in range(NSA_HPG):
        sl = slice(r * HEAD_DIM, (r + 1) * HEAD_DIM)
        o_sel = acc_ref[r] / jnp.sum(part_ref[r], axis=1, keepdims=True)
        o_win = acc_ref[NSA_HPG + r] / jnp.sum(part_ref[NSA_HPG + r], axis=1, keepdims=True)
        o = gates[0][:, r:r + 1] * ocmp_ref[:, sl] + gates[1][:, r:r + 1] * o_sel + gates[2][:, r:r + 1] * o_win
        o_ref[:, sl] = o.astype(o_ref.dtype)


def nsa_prompt(qn, skn, wkn, proj, sv_col0, wv_col0, selm, ocmp, gate_logits, tq=256, tk=256):
    b, t, w = qn.shape
    gw = NSA_HPG * HEAD_DIM
    tq, tk = min(t, tq), min(t, tk)
    nsp = selm.shape[-1]
    kv = lambda off: pl.BlockSpec((None, t, HEAD_DIM), lambda bi, g, i: (bi, 0, off + g))
    return pl.pallas_call(
        functools.partial(_nsa_prompt_kernel, tq=tq, tk=tk),
        grid=(b, NSA_KV, t // tq),
        in_specs=[pl.BlockSpec((None, tq, gw), lambda bi, g, i: (bi, i, g)),
                  kv(0), kv(sv_col0 // HEAD_DIM), kv(0), kv(wv_col0 // HEAD_DIM),
                  pl.BlockSpec((None, None, tq, nsp), lambda bi, g, i: (bi, g, i, 0)),
                  pl.BlockSpec((None, tq, gw), lambda bi, g, i: (bi, i, g)),
                  pl.BlockSpec((3, None, None, tq, NSA_HPG), lambda bi, g, i: (0, bi, g, i, 0))],
        out_specs=pl.BlockSpec((None, tq, gw), lambda bi, g, i: (bi, i, g)),
        out_shape=jax.ShapeDtypeStruct((b, t, w), BF16),
        scratch_shapes=[pltpu.VMEM((2 * NSA_HPG, tq, 1), F32), pltpu.VMEM((2 * NSA_HPG, tq, LANE), F32),
                        pltpu.VMEM((2 * NSA_HPG, tq, HEAD_DIM), F32), pltpu.VMEM((t // tk, NSA_HPG, tq, tk), F32)],
        compiler_params=_params("parallel", "parallel", "arbitrary"),
        name="nsa_prompt",
    )(qn, skn, proj, wkn, proj, selm, ocmp, gate_logits)


def _nsa_dec_kernel(pt_ref, q_ref, sel_ref, ocmp_ref, gl_ref, *refs, tq, past_len, pages_per_step):
    kp_refs, vp_refs = refs[:pages_per_step], refs[pages_per_step:2 * pages_per_step]
    skn_ref, svn_ref, wkb_ref, wvb_ref, wkn_ref, wvn_ref, o_ref, m_ref, l_ref, acc_ref = refs[2 * pages_per_step:]
    p = pl.program_id(1)
    n_steps = pl.num_programs(1)
    rows = NSA_KV * NSA_HPG * tq
    per_g = NSA_HPG * tq
    q = q_ref[...]
    sel = sel_ref[...]

    @pl.when(p == 0)
    def _():
        m_ref[...] = jnp.full(m_ref.shape, NEG, F32)
        l_ref[...] = jnp.zeros_like(l_ref)
        acc_ref[...] = jnp.zeros_like(acc_ref)

    def grid_of(n_keys):
        shape = (rows, n_keys * NSA_KV)
        lane, row = _iota(shape, 1), _iota(shape, 0)
        return lane // NSA_KV, lane % NSA_KV == row // per_g, row % tq

    def update(carry, k_rows, v_rows, msk):
        s = _dot_nt(q, k_rows.astype(BF16)) * SCALE
        return _online_update(carry, s, msk, v_rows.astype(BF16))

    key, same_g, _ = grid_of(PAGE)
    blocks_per_page = PAGE // SEL_BLOCK
    carry = (m_ref[...], l_ref[...], acc_ref[...])
    for j in range(pages_per_step):
        blk0 = blocks_per_page * (pages_per_step * p + j)
        picked = jnp.where(key < SEL_BLOCK, _pick_col(sel, blk0), _pick_col(sel, blk0 + 1))
        carry = update(carry, kp_refs[j][...], vp_refs[j][...], same_g & (picked > 0.5))
    m_ref[...], l_ref[...], acc_ref[...] = carry

    @pl.when(p == n_steps - 1)
    def _():
        n_new = skn_ref.shape[0] // NSA_KV
        key, same_g, t_row = grid_of(n_new)
        own = same_g & (key <= t_row) & (key < tq)
        new_blk = _pick_col(sel, past_len // SEL_BLOCK) > 0.5
        o_sel = _flash_out(update(carry, skn_ref[...], svn_ref[...], own & new_blk))
        wkey, wsame, wt = grid_of(WINDOW)
        w_carry = update(_flash_init(rows, HEAD_DIM), wkb_ref[...], wvb_ref[...], wsame & (wkey > wt))
        o_win = _flash_out(update(w_carry, wkn_ref[...], wvn_ref[...], own))
        gates = _sigmoid(gl_ref[...])
        o_ref[...] = gates[:, 0:1] * ocmp_ref[...] + gates[:, 1:2] * o_sel + gates[:, 2:3] * o_win


def nsa_sample(qn, skn_new, sv_new, wkn_new, wv_new, cache_sk, cache_sv, win_k, win_v, layer,
               page_table, selm, ocmp, gate_logits):
    b, tq, w = qn.shape
    rows = NSA_KV * NSA_HPG * tq
    npages = page_table.shape[1]
    past_len = npages * PAGE
    nsp = selm.shape[-1]
    assert win_k.shape[2] == WINDOW * NSA_KV and past_len % SEL_BLOCK == 0 and tq * NSA_KV <= LANE
    selrows = jnp.broadcast_to(selm[:, :, None], (b, NSA_KV, NSA_HPG, tq, nsp)).reshape(b, rows, nsp)
    to_rows = lambda a: a.reshape(b, tq, NSA_KV, NSA_HPG, -1).transpose(0, 2, 3, 1, 4).reshape(b, rows, -1)
    new_rows = lambda a: jnp.pad(a.reshape(b, tq * NSA_KV, HEAD_DIM), ((0, 0), (0, LANE - tq * NSA_KV), (0, 0)))
    per_b = lambda *shape: pl.BlockSpec((None,) + shape, lambda bi, p, pt: (bi,) + (0,) * len(shape))
    pps = 4 if npages % 4 == 0 else 1
    pages = [pl.BlockSpec((None, None, PAGE * NSA_KV, HEAD_DIM), lambda bi, p, pt, j=j: (layer, pt[bi, pps * p + j], 0, 0))
             for j in range(pps)]
    wbuf = pl.BlockSpec((None, None, WINDOW * NSA_KV, HEAD_DIM), lambda bi, p, pt: (layer, bi, 0, 0))
    o = pl.pallas_call(
        functools.partial(_nsa_dec_kernel, tq=tq, past_len=past_len, pages_per_step=pps),
        grid_spec=pltpu.PrefetchScalarGridSpec(
            num_scalar_prefetch=1,
            grid=(b, npages // pps),
            in_specs=[per_b(rows, HEAD_DIM), per_b(rows, nsp), per_b(rows, HEAD_DIM), per_b(rows, 3)]
                     + pages + pages + [per_b(LANE, HEAD_DIM), per_b(LANE, HEAD_DIM), wbuf, wbuf,
                                        per_b(LANE, HEAD_DIM), per_b(LANE, HEAD_DIM)],
            out_specs=per_b(rows, HEAD_DIM),
            scratch_shapes=[pltpu.VMEM((rows, 1), F32), pltpu.VMEM((rows, 1), F32), pltpu.VMEM((rows, HEAD_DIM), F32)],
        ),
        out_shape=jax.ShapeDtypeStruct((b, rows, HEAD_DIM), F32),
        compiler_params=_params("parallel", "arbitrary"),
        name="nsa_sample",
    )(page_table, to_rows(qn), selrows, to_rows(ocmp), to_rows(gate_logits), *([cache_sk] * pps), *([cache_sv] * pps),
      new_rows(skn_new), new_rows(sv_new), win_k, win_v, new_rows(wkn_new), new_rows(wv_new))
    return o.reshape(b, NSA_KV, NSA_HPG, tq, HEAD_DIM).transpose(0, 3, 1, 2, 4).reshape(b, tq, w)


MOBA_W = MOBA_HEADS * HEAD_DIM
CONV_DIM = D_INNER + 2 * SSD_GROUPS * SSD_STATE
EVEN_MAIN = 3 * MOBA_W + D_INNER + CONV_DIM
Z_COL0 = 0
XBC_COL0 = D_INNER
Q_COL0 = XBC_COL0 + CONV_DIM
K_COL0 = Q_COL0 + MOBA_W
V_COL0 = K_COL0 + MOBA_W
NSA_Q = NSA_KV * NSA_HPG * HEAD_DIM
ODD_MAIN = NSA_Q + 6 * KV_W
D_FF = 8192


PROJ_TN = 512


def _ws(a, w, layer, res=None, **kw):
    return a, w, layer, res, kw


def _run_pair(big, small):
    req_big, req_small = next(big), next(small)
    while True:
        a, w, layer, res, kw = req_big
        out_big, out_small = matmul_ws(a, w, layer, res=res, small=req_small[0], small_res=req_small[3], **kw)
        try:
            req_big = big.send(out_big)
        except StopIteration as done_big:
            try:
                small.send(out_small)
            except StopIteration as done_small:
                return done_big.value, done_small.value
            raise AssertionError("the two groups must issue the same matmul sequence")
        req_small = small.send(out_small)


def _proj(h2d, w, layer, main, out_cols=None):
    wt = jnp.swapaxes(w, 1, 2)
    tail = jnp.pad(wt[layer, main:, :], ((0, LANE - (w.shape[2] - main)), (0, 0)))
    main_out = yield _ws(h2d, wt, layer, k=h2d.shape[1], n=main, out_cols=out_cols, w_t=True, tn=PROJ_TN)
    return main_out, matmul(h2d, tail, b_t=True)


def _even_layer(x, sample, q_start, cache_k, cache_v, e, page_table, conv_prev, ssm_prev,
                g_norm, w_in, w_out, qk_g, conv_w, conv_b, dt_bias, a_log, d_skip, g_out):
    b, t, d = x.shape
    x2 = x.reshape(b * t, d)
    h = rmsnorm_rows(x2, g_norm)
    n_tiles = EVEN_MAIN // PROJ_TN
    proj, dt_raw = yield from _proj(h, w_in, e, EVEN_MAIN, out_cols=lambda j: (j + Q_COL0 // PROJ_TN) % n_tiles)
    proj = proj.reshape(b, t, EVEN_MAIN)
    dt_raw = dt_raw.reshape(b, t, LANE)
    qn = head_norm(proj, Q_COL0, MOBA_HEADS, qk_g[0], BF16)
    kn = head_norm(proj, K_COL0, MOBA_HEADS, qk_g[1], F32)
    v = proj[..., V_COL0:]
    if sample:
        o_attn = moba_sample(qn, kn, v, cache_k, cache_v, e, page_table).astype(BF16)
    else:
        o_attn = moba_prompt(qn, kn, proj, V_COL0)
    conv = dwconv(proj, XBC_COL0, CONV_DIM, _history8(conv_prev, b, CONV_DIM), conv_w, conv_b, False, F32)
    xbc = proj[..., XBC_COL0:Q_COL0]
    conv_state = jnp.concatenate([_history8(conv_prev, b, CONV_DIM), xbc], axis=1)[:, -(SSD_CONV - 1):]
    if t % SSD_CHUNK:
        tp = -(-t // SSD_CHUNK) * SSD_CHUNK
        padt = lambda a: jnp.pad(a, ((0, 0), (0, tp - t), (0, 0)))
        mixed, ssm = ssd_mixer(padt(conv), padt(proj[..., Z_COL0:XBC_COL0]), 0, padt(dt_raw), ssm_prev,
                               dt_bias, a_log, d_skip, g_out, t, padt(o_attn))
        mixed = mixed[:, :t]
    else:
        mixed, ssm = ssd_mixer(conv, proj, Z_COL0, dt_raw, ssm_prev, dt_bias, a_log, d_skip, g_out, None, o_attn)
    mixed = mixed.reshape(b * t, MOBA_W + D_INNER)
    x2 = yield _ws(mixed, w_out, e, res=x2, k=MOBA_W + D_INNER, n=d, tk=MOBA_W)
    heads = lambda a: a.reshape(b, t, MOBA_HEADS, HEAD_DIM)
    return x2.reshape(b, t, d), heads(kn), heads(v), conv_state, ssm


def _odd_layer(x, sample, q_start, caches, e, page_table, win_k, win_v,
               g_norm, w_in, w_out, qk_g, pe, w1, b1, w2, b2):
    b, t, d = x.shape
    x2 = x.reshape(b * t, d)
    h = rmsnorm_rows(x2, g_norm)
    proj, gl = yield from _proj(h, w_in, e, ODD_MAIN)
    proj = proj.reshape(b, t, ODD_MAIN)
    gl = gl[:, :3 * NSA_KV * NSA_HPG].reshape(b, t, NSA_KV, NSA_HPG, 3)
    col = lambda i: NSA_Q + i * KV_W
    qn = head_norm(proj, 0, NSA_KV * NSA_HPG, qk_g[0], BF16)
    skn = head_norm(proj, col(2), NSA_KV, qk_g[2], F32)
    wkn = head_norm(proj, col(4), NSA_KV, qk_g[3], F32)
    ck, cv, sv, wv = (proj[..., col(i):col(i + 1)] for i in (0, 1, 3, 5))
    if sample:
        cache_ck, cache_cv, cache_sk, cache_sv = caches
        assert t < CMP_STRIDE
        chunks = lambda c: c.reshape(c.shape[0], -1, CMP_STRIDE * NSA_KV, HEAD_DIM)
        rows_k, rows_v, pt, src = chunks(cache_ck), chunks(cache_cv), page_table, e
        n_rows = page_table.shape[1] * PAGE + t
    else:
        chunks = lambda a: a.reshape(1, b * t // CMP_STRIDE, CMP_STRIDE * NSA_KV, HEAD_DIM)
        rows_k, rows_v, src = chunks(ck), chunks(cv), 0
        npg = t // PAGE
        pt = (jnp.arange(b, dtype=jnp.int32)[:, None] * npg + jnp.arange(npg, dtype=jnp.int32)[None, :])
        n_rows = t
    kc = compress_tokens(rows_k, src, pt, pe[0], w1[0], b1[0], w2[0], b2[0], qk_g[1])
    vc = compress_tokens(rows_v, src, pt, pe[1], w1[1], b1[1], w2[1], b2[1], None)
    n_sel = -(-n_rows // SEL_BLOCK)
    ocmp, selm = nsa_compressed(qn, kc, vc, q_start, n_sel)
    if sample:
        rows_view = lambda c: c.reshape(c.shape[0], c.shape[1], c.shape[2] * NSA_KV, HEAD_DIM)
        o = nsa_sample(qn, skn, sv, wkn, wv, rows_view(cache_sk), rows_view(cache_sv), rows_view(win_k),
                       rows_view(win_v), e, page_table, selm, ocmp, gl).astype(BF16)
        wk_all = jnp.concatenate([win_k[e].reshape(b, -1, KV_W), wkn], axis=1)
        wv_all = jnp.concatenate([win_v[e].reshape(b, -1, KV_W), wv], axis=1)
    else:
        o = nsa_prompt(qn, skn, wkn, proj, col(3), col(5), selm, ocmp, gl.transpose(4, 0, 2, 1, 3))
        wk_all, wv_all = wkn, wv
    x2 = yield _ws(o.reshape(b * t, NSA_Q), w_out, e, res=x2, k=NSA_Q, n=d)
    keep = min(WINDOW, wk_all.shape[1])
    kvh = lambda a: a.reshape(b, a.shape[1], NSA_KV, HEAD_DIM)
    return (x2.reshape(b, t, d), kvh(ck), kvh(cv), kvh(skn), kvh(sv),
            kvh(wk_all[:, wk_all.shape[1] - keep:]), kvh(wv_all[:, wv_all.shape[1] - keep:]))


def _conv_ffn(x, conv_prev, li, g_norm, w_up, conv_w, conv_b, w_down):
    b, t, d = x.shape
    x2 = x.reshape(b * t, d)
    u = (yield _ws(rmsnorm_rows(x2, g_norm), w_up, li, k=d, n=2 * D_FF)).reshape(b, t, 2 * D_FF)
    hist = _history8(conv_prev, b, 2 * D_FF)
    act = dwconv(u, 0, D_FF, hist, conv_w, conv_b, True, BF16)
    state = jnp.concatenate([hist, u], axis=1)[:, -(FFN_CONV - 1):]
    x2 = yield _ws(act.reshape(b * t, D_FF), w_down, li, res=x2, k=D_FF, n=d)
    return x2.reshape(b, t, d), state


def kernel(x_prompt, x_sample, cache_moba_k, cache_moba_v, state_ssd, state_ssd_conv, cache_nsa_cmp_k, cache_nsa_cmp_v, cache_nsa_sel_k, cache_nsa_sel_v, state_nsa_win_k, state_nsa_win_v, state_ffn_conv, page_table, norm_mix, norm_ffn, even_w_in, even_w_out, moba_qk_norm, ssd_conv_w, ssd_conv_b, ssd_dt_bias, ssd_a_log, ssd_d, ssd_norm, odd_w_in, odd_w_out, nsa_qk_norm, cmp_pe, cmp_w1, cmp_b1, cmp_w2, cmp_b2, ffn_w_up, ffn_conv_w, ffn_conv_b, ffn_w_down):
    depth = norm_mix.shape[0]
    past_len = page_table.shape[1] * PAGE
    state_keys = ("moba_k", "moba_v", "ssd", "ssd_conv", "nsa_cmp_k", "nsa_cmp_v",
                  "nsa_sel_k", "nsa_sel_v", "nsa_win_k", "nsa_win_v", "ffn_conv")
    moba_pool = lambda c: c.reshape(c.shape[0], c.shape[1], PAGE * MOBA_HEADS, HEAD_DIM)
    cache_mk, cache_mv = moba_pool(cache_moba_k), moba_pool(cache_moba_v)

    def trunk(x, q_start, sample):
        b = x.shape[0]
        new = {name: [] for name in state_keys}
        for li in range(depth):
            e = li // 2
            if li % 2 == 0:
                cp = state_ssd_conv[e] if sample else None
                sp = state_ssd[e] if sample else jnp.zeros((b, SSD_HEADS, SSD_HEAD_DIM, SSD_STATE), F32)
                x, k, v, cs, ss = yield from _even_layer(x, sample, q_start, cache_mk, cache_mv, e, page_table, cp, sp,
                                                         norm_mix[li], even_w_in, even_w_out, moba_qk_norm[e],
                                                         ssd_conv_w[e], ssd_conv_b[e], ssd_dt_bias[e], ssd_a_log[e],
                                                         ssd_d[e], ssd_norm[e])
                for name, val in zip(("moba_k", "moba_v", "ssd_conv", "ssd"), (k, v, cs, ss)):
                    new[name].append(val)
            else:
                caches = (cache_nsa_cmp_k, cache_nsa_cmp_v, cache_nsa_sel_k, cache_nsa_sel_v)
                outs = yield from _odd_layer(x, sample, q_start, caches, e, page_table, state_nsa_win_k,
                                             state_nsa_win_v, norm_mix[li], odd_w_in, odd_w_out, nsa_qk_norm[e],
                                             cmp_pe[e], cmp_w1[e], cmp_b1[e], cmp_w2[e], cmp_b2[e])
                x = outs[0]
                for name, val in zip(("nsa_cmp_k", "nsa_cmp_v", "nsa_sel_k", "nsa_sel_v", "nsa_win_k", "nsa_win_v"), outs[1:]):
                    new[name].append(val)
            fp = state_ffn_conv[li] if sample else None
            x, fs = yield from _conv_ffn(x, fp, li, norm_ffn[li], ffn_w_up, ffn_conv_w[li], ffn_conv_b[li], ffn_w_down)
            new["ffn_conv"].append(fs)
        return x, {name: jnp.stack(rows) for name, rows in new.items()}

    (y_prompt, sp), (y_sample, ss) = _run_pair(trunk(x_prompt, 0, False), trunk(x_sample, past_len, True))
    out = [y_prompt, y_sample]
    for name in state_keys:
        out += [sp[name], ss[name]]
    return tuple(out)
```

```python
import functools

import numpy as np
import jax
import jax.numpy as jnp
from jax import lax
from jax.experimental import pallas as pl
from jax.experimental.pallas import tpu as pltpu

F32 = jnp.float32
BF16 = jnp.bfloat16

LANE = 128
VMEM_LIMIT_BYTES = 56 * 1024 * 1024

EPS = 1e-6
HEAD_DIM = 128
SCALE = HEAD_DIM ** -0.5
EXP2_SCALE = SCALE * float(np.log2(np.e))
NEG = -1e30

PAGE = 128
MOBA_HEADS = 16
MOBA_BLOCK = 256
MOBA_TOPK = 3
SSD_HEADS = 64
SSD_HEAD_DIM = 64
SSD_GROUPS = 8
SSD_STATE = 128
SSD_CONV = 4
SSD_CHUNK = 128
D_INNER = 4096
GROUP_W = D_INNER // SSD_GROUPS
HEADS_PER_GROUP = SSD_HEADS // SSD_GROUPS
NSA_KV = 4
NSA_HPG = 8
KV_W = NSA_KV * HEAD_DIM
CMP_LEN = 32
CMP_STRIDE = 16
SEL_BLOCK = 64
SEL_TOPN = 16
WINDOW = 512
FFN_CONV = 3


def _params(*sem):
    return pltpu.CompilerParams(dimension_semantics=sem, vmem_limit_bytes=VMEM_LIMIT_BYTES)


def _dot(a, b):
    return jnp.dot(a, b, preferred_element_type=F32)


def _dot_nt(a, b):
    return lax.dot_general(a, b, (((1,), (1,)), ((), ())), preferred_element_type=F32)


def _dot_tn(a, b):
    return lax.dot_general(a, b, (((0,), (0,)), ((), ())), preferred_element_type=F32)


def _split3(x):
    hi = x.astype(BF16)
    r1 = x - hi.astype(F32)
    mid = r1.astype(BF16)
    lo = (r1 - mid.astype(F32)).astype(BF16)
    return hi, mid, lo


def _sigmoid(x):
    return 1.0 / (1.0 + jnp.exp(-x))


def _silu(x):
    return x * _sigmoid(x)


def _softplus(x):
    return jnp.maximum(x, 0.0) + jnp.log1p(jnp.exp(-jnp.abs(x)))


def _gelu_tanh(x):
    return 0.5 * x * (1.0 + jnp.tanh(np.sqrt(2.0 / np.pi).astype(np.float32) * (x + 0.044715 * (x * x * x))))


def _rms(x, g):
    return x * lax.rsqrt(jnp.mean(x * x, axis=-1, keepdims=True) + EPS) * g


def _iota(shape, axis):
    return lax.broadcasted_iota(jnp.int32, shape, axis)


def _rank_before(score, n_cols):
    col = _iota(score.shape, 1)
    rank = jnp.zeros(score.shape, F32)
    for c in range(n_cols):
        sc = score[:, c:c + 1]
        rank = rank + jnp.where((sc > score) | ((sc == score) & (c < col)), 1.0, 0.0)
    return rank


def _pick_col(x, c):
    return jnp.sum(jnp.where(_iota(x.shape, 1) == c, x, 0.0), axis=1, keepdims=True)


def _online_update(carry, s, msk, v_bf16):
    m, l, acc = carry
    s = jnp.where(msk, s, NEG)
    m_new = jnp.maximum(m, jnp.max(s, axis=1, keepdims=True))
    p = jnp.where(msk, jnp.exp(s - m_new), 0.0)
    alpha = jnp.exp(m - m_new)
    l = alpha * l + jnp.sum(p, axis=1, keepdims=True)
    acc = alpha * acc + _dot(p.astype(BF16), v_bf16)
    return m_new, l, acc


def _flash_init(rows, dv):
    return (jnp.full((rows, 1), NEG, F32), jnp.zeros((rows, 1), F32), jnp.zeros((rows, dv), F32))


def _flash_out(carry):
    _, l, acc = carry
    return acc / jnp.where(l > 0.0, l, 1.0)


def _rmsnorm_kernel(x_ref, g_ref, o_ref):
    o_ref[...] = _rms(x_ref[...], g_ref[...]).astype(o_ref.dtype)


def rmsnorm_rows(x2d, g):
    m, d = x2d.shape
    tm = min(m, 256)
    return pl.pallas_call(
        _rmsnorm_kernel,
        grid=(m // tm,),
        in_specs=[pl.BlockSpec((tm, d), lambda i: (i, 0)), pl.BlockSpec((1, d), lambda i: (0, 0))],
        out_specs=pl.BlockSpec((tm, d), lambda i: (i, 0)),
        out_shape=jax.ShapeDtypeStruct((m, d), BF16),
        compiler_params=_params("parallel"),
        name="rmsnorm_rows",
    )(x2d, g.reshape(1, d))


def _matmul_kernel(*refs, nk, has_res, b_t):
    if has_res:
        a_ref, b_ref, r_ref, o_ref = refs
    else:
        a_ref, b_ref, o_ref = refs
    part = (_dot_nt if b_t else _dot)(a_ref[...].astype(BF16), b_ref[...].astype(BF16))
    if nk == 1:
        o_ref[...] = part + r_ref[...] if has_res else part
        return
    k = pl.program_id(2)

    @pl.when(k == 0)
    def _():
        o_ref[...] = part + r_ref[...] if has_res else part

    @pl.when(k > 0)
    def _():
        o_ref[...] += part


def matmul(a, b, res=None, b_t=False, tm=1024, tn=512, tk=4096):
    m, k = a.shape
    n = b.shape[0] if b_t else b.shape[1]
    tm, tn, tk = min(tm, m), min(tn, n), min(tk, k)
    assert m % tm == 0 and n % tn == 0 and k % tk == 0, (a.shape, b.shape)
    nk = k // tk
    b_spec = pl.BlockSpec((tn, tk), lambda i, j, kk: (j, kk)) if b_t else pl.BlockSpec((tk, tn), lambda i, j, kk: (kk, j))
    in_specs = [pl.BlockSpec((tm, tk), lambda i, j, kk: (i, kk)), b_spec]
    args = [a, b]
    if res is not None:
        in_specs.append(pl.BlockSpec((tm, tn), lambda i, j, kk: (i, j)))
        args.append(res)
    return pl.pallas_call(
        functools.partial(_matmul_kernel, nk=nk, has_res=res is not None, b_t=b_t),
        grid=(m // tm, n // tn, nk),
        in_specs=in_specs,
        out_specs=pl.BlockSpec((tm, tn), lambda i, j, kk: (i, j)),
        out_shape=jax.ShapeDtypeStruct((m, n), F32),
        compiler_params=_params("parallel", "parallel", "arbitrary"),
        name="matmul",
    )(*args)


def _matmul_ws_kernel(*refs, nk, has_res, has_small, w_t):
    refs = list(refs)
    a_ref, w_ref = refs.pop(0), refs.pop(0)
    r_ref = refs.pop(0) if has_res else None
    a2_ref = refs.pop(0) if has_small else None
    r2_ref = refs.pop(0) if has_small and has_res else None
    o_ref = refs.pop(0)
    o2_ref = refs.pop(0) if has_small else None
    wbf_ref, = refs
    i = pl.program_id(1)
    kk = pl.program_id(2)
    mm = _dot_nt if w_t else _dot

    def accumulate(out_ref, part, res_ref):
        if nk == 1:
            out_ref[...] = part + res_ref[...] if has_res else part
            return

        @pl.when(kk == 0)
        def _():
            out_ref[...] = part + res_ref[...] if has_res else part

        @pl.when(kk > 0)
        def _():
            out_ref[...] += part

    @pl.when(i == 0)
    def _():
        wbf_ref[kk] = w_ref[...].astype(BF16)
        if has_small:
            accumulate(o2_ref, mm(a2_ref[...], wbf_ref[kk]), r2_ref)

    accumulate(o_ref, mm(a_ref[...], wbf_ref[kk]), r_ref)


def matmul_ws(a, w, layer, *, k, n, res=None, small=None, small_res=None, out_cols=None, w_t=False,
              tm=1024, tn=512, tk=4096):
    m = a.shape[0]
    tm, tn, tk = min(tm, m), min(tn, n), min(tk, k)
    assert m % tm == 0 and n % tn == 0 and k % tk == 0, (a.shape, w.shape, k, n)
    assert (res is None) == (small_res is None) or small is None
    nk = k // tk
    k_idx = (lambda i, kk: 0) if nk == 1 else (lambda i, kk: jnp.where(i == 0, kk, nk - 1))
    if w_t:
        w_spec = pl.BlockSpec((None, tn, tk), lambda j, i, kk: (layer, j, k_idx(i, kk)))
    else:
        w_spec = pl.BlockSpec((None, tk, tn), lambda j, i, kk: (layer, k_idx(i, kk), j))
    col = (lambda j: j) if out_cols is None else out_cols
    o_spec = pl.BlockSpec((tm, tn), lambda j, i, kk: (i, col(j)))
    in_specs = [pl.BlockSpec((tm, tk), lambda j, i, kk: (i, kk)), w_spec]
    args = [a, w]
    out_specs, out_shape = [o_spec], [jax.ShapeDtypeStruct((m, n), F32)]
    if res is not None:
        in_specs.append(o_spec)
        args.append(res)
    if small is not None:
        m2 = small.shape[0]
        o2_spec = pl.BlockSpec((m2, tn), lambda j, i, kk: (0, col(j)))
        in_specs.append(pl.BlockSpec((m2, tk), lambda j, i, kk: (0, k_idx(i, kk))))
        args.append(small)
        if res is not None:
            in_specs.append(o2_spec)
            args.append(small_res)
        out_specs.append(o2_spec)
        out_shape.append(jax.ShapeDtypeStruct((m2, n), F32))
    outs = pl.pallas_call(
        functools.partial(_matmul_ws_kernel, nk=nk, has_res=res is not None, has_small=small is not None, w_t=w_t),
        grid=(n // tn, m // tm, nk),
        in_specs=in_specs,
        out_specs=out_specs,
        out_shape=out_shape,
        scratch_shapes=[pltpu.VMEM((nk, tn, tk) if w_t else (nk, tk, tn), BF16)],
        compiler_params=_params("arbitrary", "arbitrary", "arbitrary"),
        name="matmul_ws",
    )(*args)
    return outs[0] if small is None else tuple(outs)


def _head_norm_kernel(x_ref, g_ref, o_ref, *, heads):
    g = g_ref[...]
    for h in range(heads):
        sl = slice(h * HEAD_DIM, (h + 1) * HEAD_DIM)
        o_ref[:, sl] = _rms(x_ref[:, sl], g).astype(o_ref.dtype)


def head_norm(x, col0, n_heads, g, out_dtype):
    b, t, _ = x.shape
    hb = min(n_heads, 4)
    w = hb * HEAD_DIM
    assert col0 % w == 0 and n_heads % hb == 0
    tq = min(t, 512)
    return pl.pallas_call(
        functools.partial(_head_norm_kernel, heads=hb),
        grid=(b, t // tq, n_heads // hb),
        in_specs=[pl.BlockSpec((None, tq, w), lambda bi, i, j: (bi, i, col0 // w + j)),
                  pl.BlockSpec((1, HEAD_DIM), lambda bi, i, j: (0, 0))],
        out_specs=pl.BlockSpec((None, tq, w), lambda bi, i, j: (bi, i, j)),
        out_shape=jax.ShapeDtypeStruct((b, t, n_heads * HEAD_DIM), out_dtype),
        compiler_params=_params("parallel", "parallel", "parallel"),
        name="head_norm",
    )(x, g.reshape(1, HEAD_DIM))


CONV_ROWS = 256


def _dwconv_kernel(*refs, width, t, glu):
    if glu:
        xa_ref, xg_ref, pa_ref, pg_ref, wa_ref, wg_ref, ba_ref, bg_ref, o_ref, sa_ref, sg_ref = refs
        pairs = ((xa_ref, pa_ref, sa_ref), (xg_ref, pg_ref, sg_ref))
    else:
        xa_ref, pa_ref, wa_ref, ba_ref, o_ref, sa_ref = refs
        pairs = ((xa_ref, pa_ref, sa_ref),)
    for x_ref, p_ref, s_ref in pairs:
        s_ref[0:8, :] = p_ref[...]
        s_ref[8:8 + t, :] = x_ref[...]
    tc = min(t, CONV_ROWS)

    def conv(s_ref, w_ref, b_ref, r0):
        acc = b_ref[...]
        for i in range(width):
            acc = acc + w_ref[i:i + 1, :] * s_ref[pl.ds(8 - (width - 1) + i + r0, tc), :]
        return acc

    for c in range(t // tc):
        r0 = c * tc
        a = conv(sa_ref, wa_ref, ba_ref, r0)
        if glu:
            o_ref[r0:r0 + tc, :] = (a * _silu(conv(sg_ref, wg_ref, bg_ref, r0))).astype(o_ref.dtype)
        else:
            o_ref[r0:r0 + tc, :] = _silu(a).astype(o_ref.dtype)


def dwconv(x, col0, c_out, prev8, w, bias, glu, out_dtype):
    b, t, _ = x.shape
    width = w.shape[0]
    tc = min(256 if t > 64 else 2048, c_out)
    assert col0 % tc == 0 and c_out % tc == 0
    nj = c_out // tc
    x_spec = lambda off: pl.BlockSpec((None, t, tc), lambda bi, j: (bi, 0, off + j))
    p_spec = lambda off: pl.BlockSpec((None, 8, tc), lambda bi, j: (bi, 0, off + j))
    w_spec = lambda off: pl.BlockSpec((width, tc), lambda bi, j: (0, off + j))
    b_spec = lambda off: pl.BlockSpec((1, tc), lambda bi, j: (0, off + j))
    bias2 = bias.reshape(1, -1)
    if glu:
        in_specs = [x_spec(col0 // tc), x_spec(col0 // tc + nj), p_spec(0), p_spec(nj),
                    w_spec(0), w_spec(nj), b_spec(0), b_spec(nj)]
        args = (x, x, prev8, prev8, w, w, bias2, bias2)
        scratch = [pltpu.VMEM((8 + t, tc), F32), pltpu.VMEM((8 + t, tc), F32)]
    else:
        in_specs = [x_spec(col0 // tc), p_spec(0), w_spec(0), b_spec(0)]
        args = (x, prev8, w, bias2)
        scratch = [pltpu.VMEM((8 + t, tc), F32)]
    return pl.pallas_call(
        functools.partial(_dwconv_kernel, width=width, t=t, glu=glu),
        grid=(b, nj),
        in_specs=in_specs,
        out_specs=pl.BlockSpec((None, t, tc), lambda bi, j: (bi, 0, j)),
        out_shape=jax.ShapeDtypeStruct((b, t, c_out), out_dtype),
        scratch_shapes=scratch,
        compiler_params=_params("parallel", "parallel"),
        name="dwconv_glu" if glu else "dwconv_silu",
    )(*args)


def _history8(state, b, c):
    if state is None:
        return jnp.zeros((b, 8, c), F32)
    return jnp.concatenate([jnp.zeros((b, 8 - state.shape[1], c), F32), state], axis=1)


def _moba_tile(i, nb, q_ref, o_ref, kmean_ref, kbf_ref, vbf_ref, spread_ref):
    blk, half = MOBA_BLOCK, MOBA_BLOCK // 2
    fold = lambda a, op: functools.reduce(op, [a[:, c * LANE:(c + 1) * LANE] for c in range(a.shape[1] // LANE)])
    k_blk = lambda n: kbf_ref[n * blk:(n + 1) * blk, :]
    v_blk = lambda n: vbf_ref[n * blk:(n + 1) * blk, :]
    if i > 0:
        gate = _dot_nt(kmean_ref[...].astype(BF16), q_ref[...])[:-(-nb // 8) * 8, :]
        blk_id = _iota(gate.shape, 0)
        valid = blk_id < i
        gate = jnp.where(valid, gate, -jnp.inf)
        rank = jnp.zeros(gate.shape, F32)
        for c in range(i):
            gc = gate[c:c + 1, :]
            rank = rank + jnp.where((gc > gate) | ((gc == gate) & (c < blk_id)), 1.0, 0.0)
        sel_t = jnp.where(valid & (rank < min(MOBA_TOPK, nb)), 1.0, 0.0)
        sel_t = jnp.concatenate([sel_t, jnp.zeros((LANE - sel_t.shape[0], blk), F32)], axis=0).astype(BF16)
        selb_all = _dot_tn(sel_t, spread_ref[:, :i * LANE])
    for hq in range(2):
        q = q_ref[hq * half:(hq + 1) * half, :]
        masks = []
        if i > 0:
            selb = selb_all[hq * half:(hq + 1) * half, :]
            masks = [jnp.concatenate([selb[:, n * LANE:(n + 1) * LANE]] * (blk // LANE), axis=1) > 0.5 for n in range(i)]
        causal = _iota((half, blk), 1) <= _iota((half, blk), 0) + hq * half
        masks.append(causal)
        scores = lambda n: jnp.where(masks[n], _dot_nt(q, k_blk(n)), NEG)
        part = functools.reduce(jnp.maximum, [fold(scores(n), jnp.maximum) for n in range(i + 1)])
        m = jnp.max(part, axis=1, keepdims=True)
        l_part, acc = None, None
        for n in range(i + 1):
            p = jnp.exp2((scores(n) - m) * EXP2_SCALE)
            pv = _dot(p.astype(BF16), v_blk(n))
            l_part = fold(p, jnp.add) if n == 0 else l_part + fold(p, jnp.add)
            acc = pv if n == 0 else acc + pv
        l = jnp.sum(l_part, axis=1, keepdims=True)
        o_ref[hq * half:(hq + 1) * half, :] = (acc / l).astype(o_ref.dtype)


def _moba_tiles_kernel(q_ref, k_ref, v_ref, o_ref, kmean_ref, kbf_ref, vbf_ref, spread_ref, *, nb):
    blk = MOBA_BLOCK
    i = pl.program_id(2)

    @pl.when(i == 0)
    def _():
        kmean_ref[...] = jnp.zeros_like(kmean_ref)
        for n in range(nb):
            kmean_ref[n:n + 1, :] = jnp.mean(k_ref[n * blk:(n + 1) * blk, :], axis=0, keepdims=True)
        kbf_ref[...] = k_ref[...].astype(BF16)
        vbf_ref[...] = v_ref[...].astype(BF16)
        shape = spread_ref.shape
        spread_ref[...] = jnp.where(_iota(shape, 0) == _iota(shape, 1) // LANE, 1.0, 0.0).astype(BF16)

    for c in range(nb):
        @pl.when(i == c)
        def _(c=c):
            _moba_tile(c, nb, q_ref, o_ref, kmean_ref, kbf_ref, vbf_ref, spread_ref)


def moba_prompt(qn, kn, proj, v_col0):
    b, t, w = qn.shape
    nh = w // HEAD_DIM
    nb = t // MOBA_BLOCK
    assert t % MOBA_BLOCK == 0 and nb <= LANE
    return pl.pallas_call(
        functools.partial(_moba_tiles_kernel, nb=nb),
        grid=(b, nh, nb),
        in_specs=[pl.BlockSpec((None, MOBA_BLOCK, HEAD_DIM), lambda bi, h, i: (bi, i, h)),
                  pl.BlockSpec((None, t, HEAD_DIM), lambda bi, h, i: (bi, 0, h)),
                  pl.BlockSpec((None, t, HEAD_DIM), lambda bi, h, i: (bi, 0, v_col0 // HEAD_DIM + h))],
        out_specs=pl.BlockSpec((None, MOBA_BLOCK, HEAD_DIM), lambda bi, h, i: (bi, i, h)),
        out_shape=jax.ShapeDtypeStruct((b, t, w), BF16),
        scratch_shapes=[pltpu.VMEM((LANE, HEAD_DIM), F32), pltpu.VMEM((t, HEAD_DIM), BF16), pltpu.VMEM((t, HEAD_DIM), BF16),
                        pltpu.VMEM((LANE, nb * LANE), BF16)],
        compiler_params=_params("parallel", "parallel", "arbitrary"),
        name="moba_prompt",
    )(qn, kn, proj)


def _moba_dec_block_kernel(pt_ref, q_ref, ka_ref, kb_ref, va_ref, vb_ref,
                           acc_ref, g_ref, m_ref, l_ref, *, nh, tq):
    n = pl.program_id(1)
    q = q_ref[...]
    kb = jnp.concatenate([ka_ref[...], kb_ref[...]], axis=0).astype(BF16)
    vb = jnp.concatenate([va_ref[...], vb_ref[...]], axis=0).astype(BF16)
    s_raw = _dot_nt(q, kb)
    same_head = _iota(s_raw.shape, 1) % nh == _iota(s_raw.shape, 0) // tq
    gate = jnp.sum(jnp.where(same_head, s_raw, 0.0), axis=1, keepdims=True) * (1.0 / MOBA_BLOCK)
    s = jnp.where(same_head, s_raw * SCALE, NEG)
    m = jnp.max(s, axis=1, keepdims=True)
    p = jnp.exp(s - m)
    l = jnp.sum(p, axis=1, keepdims=True)
    acc_ref[...] = _dot(p.astype(BF16), vb)

    @pl.when(n == 0)
    def _():
        g_ref[...] = jnp.zeros_like(g_ref)
        m_ref[...] = jnp.zeros_like(m_ref)
        l_ref[...] = jnp.zeros_like(l_ref)

    lane = _iota(g_ref.shape, 1)
    g_ref[...] = jnp.where(lane == n, gate, g_ref[...])
    m_ref[...] = jnp.where(lane == n, m, m_ref[...])
    l_ref[...] = jnp.where(lane == n, l, l_ref[...])


def _moba_dec_combine_kernel(g_ref, m_ref, l_ref, acc_ref, q_ref, kn_ref, vn_ref, o_ref, *, nbk, nh, tq):
    gate = g_ref[...]
    lane = _iota(gate.shape, 1)
    row = _iota(gate.shape, 0)
    valid = lane < nbk
    gate = jnp.where(valid, gate, -jnp.inf)
    sel = valid & (_rank_before(gate, nbk) < min(MOBA_TOPK, nbk + 1))
    q = q_ref[...]
    s_own = _dot_nt(q, kn_ref[...].astype(BF16)) * SCALE
    own = (lane % nh == row // tq) & (lane // nh <= row % tq)
    m_blk = m_ref[...]
    m_tot = jnp.maximum(jnp.max(jnp.where(sel, m_blk, NEG), axis=1, keepdims=True),
                        jnp.max(jnp.where(own, s_own, NEG), axis=1, keepdims=True))
    wgt = jnp.where(sel, jnp.exp(m_blk - m_tot), 0.0)
    p_own = jnp.where(own, jnp.exp(s_own - m_tot), 0.0)
    l_tot = jnp.sum(wgt * l_ref[...], axis=1, keepdims=True) + jnp.sum(p_own, axis=1, keepdims=True)
    acc = _dot(p_own.astype(BF16), vn_ref[...].astype(BF16))
    for n in range(nbk):
        acc = acc + wgt[:, n:n + 1] * acc_ref[n]
    o_ref[...] = acc / jnp.where(l_tot > 0.0, l_tot, 1.0)


def moba_sample(qn, kn_new, v_new, cache_k, cache_v, layer, page_table):
    b, tq, w = qn.shape
    nh = w // HEAD_DIM
    rows = nh * tq
    npages = page_table.shape[1]
    ppb = MOBA_BLOCK // PAGE
    nbk = npages // ppb
    assert rows == LANE and ppb == 2 and nbk <= LANE
    qrows = qn.reshape(b, tq, nh, HEAD_DIM).transpose(0, 2, 1, 3).reshape(b, rows, HEAD_DIM)
    kn_rows = kn_new.reshape(b, tq * nh, HEAD_DIM)
    vn_rows = v_new.reshape(b, tq * nh, HEAD_DIM)

    page = lambda j: pl.BlockSpec((None, None, PAGE * nh, HEAD_DIM), lambda bi, n, pt: (layer, pt[bi, ppb * n + j], 0, 0))
    stat = pl.BlockSpec((None, rows, LANE), lambda bi, n, pt: (bi, 0, 0))
    acc, g, m, l = pl.pallas_call(
        functools.partial(_moba_dec_block_kernel, nh=nh, tq=tq),
        grid_spec=pltpu.PrefetchScalarGridSpec(
            num_scalar_prefetch=1,
            grid=(b, nbk),
            in_specs=[pl.BlockSpec((None, rows, HEAD_DIM), lambda bi, n, pt: (bi, 0, 0)),
                      page(0), page(1), page(0), page(1)],
            out_specs=[pl.BlockSpec((None, None, rows, HEAD_DIM), lambda bi, n, pt: (bi, n, 0, 0)), stat, stat, stat],
        ),
        out_shape=[jax.ShapeDtypeStruct((b, nbk, rows, HEAD_DIM), F32)] + [jax.ShapeDtypeStruct((b, rows, LANE), F32)] * 3,
        compiler_params=_params("parallel", "arbitrary"),
        name="moba_sample_blocks",
    )(page_table, qrows, cache_k, cache_k, cache_v, cache_v)

    per_b = lambda *shape: pl.BlockSpec((None,) + shape, lambda bi: (bi,) + (0,) * len(shape))
    o = pl.pallas_call(
        functools.partial(_moba_dec_combine_kernel, nbk=nbk, nh=nh, tq=tq),
        grid=(b,),
        in_specs=[per_b(rows, LANE), per_b(rows, LANE), per_b(rows, LANE), per_b(nbk, rows, HEAD_DIM),
                  per_b(rows, HEAD_DIM), per_b(tq * nh, HEAD_DIM), per_b(tq * nh, HEAD_DIM)],
        out_specs=per_b(rows, HEAD_DIM),
        out_shape=jax.ShapeDtypeStruct((b, rows, HEAD_DIM), F32),
        compiler_params=_params("parallel"),
        name="moba_sample_combine",
    )(g, m, l, acc, qrows, kn_rows, vn_rows)
    return o.reshape(b, nh, tq, HEAD_DIM).transpose(0, 2, 1, 3).reshape(b, tq, w)


def _ssd_kernel(x_ref, bm_ref, cm_ref, z_ref, dtc_ref, dtr_ref, bias_c_ref, bias_r_ref, alog_c_ref, alog_r_ref,
                dskip_ref, gout_ref, s0_ref, pre_ref, y_ref, sfin_ref, state_ref, *, q, t_valid):
    c = pl.program_id(1)
    nc = pl.num_programs(1)
    pre_w = pre_ref.shape[1]
    y_ref[:, :pre_w] = pre_ref[...]

    @pl.when(c == 0)
    def _():
        state_ref[...] = s0_ref[...]

    dtc = _softplus(dtc_ref[...] + bias_c_ref[...])
    dtr = _softplus(dtr_ref[...] + bias_r_ref[...])
    if t_valid is not None:
        dtc = jnp.where(c * q + _iota(dtc.shape, 0) < t_valid, dtc, 0.0)
        dtr = jnp.where(c * q + _iota(dtr.shape, 1) < t_valid, dtr, 0.0)
    da_c = dtc * -jnp.exp(alog_c_ref[...])
    da_r = dtr * -jnp.exp(alog_r_ref[...])
    tri = _iota((q, q), 0) >= _iota((q, q), 1)
    tri_bf = jnp.where(tri, 1.0, 0.0).astype(BF16)
    cum_c = sum(_dot(tri_bf, part) for part in _split3(da_c))
    cum_r = sum(_dot_nt(part, tri_bf) for part in _split3(da_r))
    lane = _iota((q, LANE), 1)
    lo_half = lane < SSD_HEAD_DIM
    row_lo = _iota((LANE, 1), 0) < SSD_HEAD_DIM
    pairs = HEADS_PER_GROUP // 2

    for g in range(SSD_GROUPS):
        xg = x_ref[:, g * GROUP_W:(g + 1) * GROUP_W]
        bm = bm_ref[:, g * SSD_STATE:(g + 1) * SSD_STATE].astype(BF16)
        cm = cm_ref[:, g * SSD_STATE:(g + 1) * SSD_STATE].astype(BF16)
        cb = _dot_nt(cm, bm)
        y_parts = []
        for pr in range(pairs):
            x_pair = xg[:, pr * LANE:(pr + 1) * LANE]
            x_bf = x_pair.astype(BF16)
            ys, cols, lasts = [], [], []
            for h in (g * HEADS_PER_GROUP + 2 * pr, g * HEADS_PER_GROUP + 2 * pr + 1):
                col = cum_c[:, h:h + 1]
                seg = col - cum_r[h:h + 1, :]
                decay = jnp.where(tri, jnp.exp(jnp.where(tri, seg, 0.0)), 0.0)
                wgt = cb * decay * dtr[h:h + 1, :]
                ys.append(_dot(wgt.astype(BF16), x_bf))
                cols.append(col)
                lasts.append(cum_c[q - 1:q, h:h + 1])
            sidx = g * pairs + pr
            state = state_ref[sidx]
            y_pair = jnp.where(lo_half, ys[0], ys[1])
            carry_in = _dot_nt(cm, state.astype(BF16))
            y_pair = y_pair + carry_in * jnp.exp(jnp.where(lo_half, cols[0], cols[1]))
            y_parts.append(y_pair)
            tail = jnp.where(lo_half,
                             jnp.exp(lasts[0] - cols[0]) * dtc[:, 2 * sidx:2 * sidx + 1],
                             jnp.exp(lasts[1] - cols[1]) * dtc[:, 2 * sidx + 1:2 * sidx + 2])
            upd = _dot_tn((x_pair * tail).astype(BF16), bm)
            state_ref[sidx] = state * jnp.where(row_lo, jnp.exp(lasts[0]), jnp.exp(lasts[1])) + upd
        sl = slice(g * GROUP_W, (g + 1) * GROUP_W)
        yg = jnp.concatenate(y_parts, axis=1) + xg * dskip_ref[:, sl]
        yg = yg * _silu(z_ref[:, sl])
        y_ref[:, pre_w + g * GROUP_W:pre_w + (g + 1) * GROUP_W] = _rms(yg, gout_ref[:, sl]).astype(y_ref.dtype)

    @pl.when(c == nc - 1)
    def _():
        sfin_ref[...] = state_ref[...]


def ssd_mixer(conv, z_src, z_col0, dt_raw, s0, dt_bias, a_log, d_skip, g_out, t_valid, prefix):
    b, t, _ = conv.shape
    pre_w = prefix.shape[-1]
    q = SSD_CHUNK
    assert t % q == 0
    nc = t // q
    dtr = dt_raw.transpose(0, 2, 1)
    pad_h = LANE - SSD_HEADS
    bias_c = jnp.pad(dt_bias, (0, pad_h)).reshape(1, LANE)
    bias_r = bias_c.reshape(LANE, 1)
    alog_c = jnp.pad(a_log, (0, pad_h)).reshape(1, LANE)
    alog_r = alog_c.reshape(LANE, 1)
    dskip = jnp.repeat(d_skip, SSD_HEAD_DIM).reshape(1, D_INNER)
    npair = SSD_HEADS // 2
    s0p = s0.reshape(b, npair, 2 * SSD_HEAD_DIM, SSD_STATE)
    nbc = D_INNER // (SSD_GROUPS * SSD_STATE)
    const = lambda shape: pl.BlockSpec(shape, lambda bi, c: (0,) * len(shape))
    y, sfin = pl.pallas_call(
        functools.partial(_ssd_kernel, q=q, t_valid=t_valid),
        grid=(b, nc),
        in_specs=[pl.BlockSpec((None, q, D_INNER), lambda bi, c: (bi, c, 0)),
                  pl.BlockSpec((None, q, SSD_GROUPS * SSD_STATE), lambda bi, c: (bi, c, nbc)),
                  pl.BlockSpec((None, q, SSD_GROUPS * SSD_STATE), lambda bi, c: (bi, c, nbc + 1)),
                  pl.BlockSpec((None, q, D_INNER), lambda bi, c: (bi, c, z_col0 // D_INNER)),
                  pl.BlockSpec((None, q, LANE), lambda bi, c: (bi, c, 0)),
                  pl.BlockSpec((None, LANE, q), lambda bi, c: (bi, 0, c)),
                  const((1, LANE)), const((LANE, 1)), const((1, LANE)), const((LANE, 1)),
                  const((1, D_INNER)), const((1, D_INNER)),
                  pl.BlockSpec((None, npair, LANE, SSD_STATE), lambda bi, c: (bi, 0, 0, 0)),
                  pl.BlockSpec((None, q, pre_w), lambda bi, c: (bi, c, 0))],
        out_specs=[pl.BlockSpec((None, q, pre_w + D_INNER), lambda bi, c: (bi, c, 0)),
                   pl.BlockSpec((None, npair, LANE, SSD_STATE), lambda bi, c: (bi, 0, 0, 0))],
        out_shape=[jax.ShapeDtypeStruct((b, t, pre_w + D_INNER), BF16),
                   jax.ShapeDtypeStruct((b, npair, LANE, SSD_STATE), F32)],
        scratch_shapes=[pltpu.VMEM((npair, LANE, SSD_STATE), F32)],
        compiler_params=_params("parallel", "arbitrary"),
        name="ssd_scan",
    )(conv, conv, conv, z_src, dt_raw, dtr, bias_c, bias_r, alog_c, alog_r, dskip, g_out.reshape(1, D_INNER), s0p,
      prefix)
    return y, sfin.reshape(b, SSD_HEADS, SSD_HEAD_DIM, SSD_STATE)


def _compress_kernel(pt_ref, a_ref, b_ref, cpe_a_ref, cpe_b_ref, b1_ref, w2_ref, b2_ref, g_ref, o_ref,
                     ha_ref, hb_ref, *, npg, norm):
    bi = pl.program_id(0)
    ncp = npg * 8

    def gather(p, _):
        src = pl.ds(pl.multiple_of(pt_ref[bi, p] * 8, 8), 8)
        dst = pl.ds(pl.multiple_of(p * 8, 8), 8)
        ha_ref[dst, :] = a_ref[src, :]
        hb_ref[dst, :] = b_ref[src, :]
        return 0

    lax.fori_loop(0, npg, gather, 0)
    hb_ref[ncp:ncp + 8, :] = jnp.zeros((8, HEAD_DIM), F32)
    hid = ha_ref[...] + hb_ref[pl.ds(1, ncp), :] + (cpe_a_ref[0:1, :] + cpe_b_ref[1:2, :] + b1_ref[...])
    out = _dot(_gelu_tanh(hid).astype(BF16), w2_ref[...]) + b2_ref[...]
    if norm:
        out = _rms(out, g_ref[...])
    o_ref[...] = out


ROWS_PER_TILE = 8


def _chunk_products_kernel(x_ref, w_ref, o_ref):
    @pl.when(pl.program_id(1) == 0)
    def _():
        o_ref[...] = jnp.zeros_like(o_ref)

    for r in range(ROWS_PER_TILE):
        g = r % NSA_KV
        prod = _dot(x_ref[:, r, :].astype(BF16), w_ref[r // NSA_KV])
        o_ref[:, g * HEAD_DIM:(g + 1) * HEAD_DIM] += prod[:, :HEAD_DIM]
        o_ref[:, KV_W + g * HEAD_DIM:KV_W + (g + 1) * HEAD_DIM] += prod[:, HEAD_DIM:]


def chunk_products(x, layer, w_pair):
    nck = x.shape[1]
    blk = min(nck, 1024)
    assert nck % blk == 0 and x.shape[2] == CMP_STRIDE * NSA_KV
    per_step = ROWS_PER_TILE // NSA_KV
    return pl.pallas_call(
        _chunk_products_kernel,
        grid=(nck // blk, CMP_STRIDE // per_step),
        in_specs=[pl.BlockSpec((None, blk, ROWS_PER_TILE, HEAD_DIM), lambda i, lp: (layer, i, lp, 0)),
                  pl.BlockSpec((per_step, HEAD_DIM, 2 * HEAD_DIM), lambda i, lp: (lp, 0, 0))],
        out_specs=pl.BlockSpec((blk, 2 * KV_W), lambda i, lp: (i, 0)),
        out_shape=jax.ShapeDtypeStruct((nck, 2 * KV_W), F32),
        compiler_params=_params("parallel", "arbitrary"),
        name="chunk_products",
    )(x, w_pair)


def compress_tokens(rows, layer, page_table, pe, w1, b1, w2, b2, g_norm):
    nck = rows.shape[1]
    b, npg = page_table.shape
    ncp = npg * 8
    w_pair = w1.reshape(2, CMP_STRIDE, HEAD_DIM, HEAD_DIM).transpose(1, 2, 0, 3).reshape(CMP_STRIDE, HEAD_DIM, 2 * HEAD_DIM)
    w_pair = w_pair.astype(BF16)
    ab = chunk_products(rows, layer, w_pair)
    pe_rows = jnp.broadcast_to(pe.reshape(2, CMP_STRIDE, 1, HEAD_DIM), (2, CMP_STRIDE, NSA_KV, HEAD_DIM))
    pe_rows = jnp.pad(pe_rows.reshape(1, 2, CMP_STRIDE * NSA_KV, HEAD_DIM), ((0, 0), (0, 6), (0, 0), (0, 0)))
    cpe = chunk_products(pe_rows, 0, w_pair)
    col = lambda off: (lambda bi, g, pt: (0, off + g))
    return pl.pallas_call(
        functools.partial(_compress_kernel, npg=npg, norm=g_norm is not None),
        grid_spec=pltpu.PrefetchScalarGridSpec(
            num_scalar_prefetch=1,
            grid=(b, NSA_KV),
            in_specs=[pl.BlockSpec((nck, HEAD_DIM), col(0)), pl.BlockSpec((nck, HEAD_DIM), col(NSA_KV)),
                      pl.BlockSpec((8, HEAD_DIM), col(0)), pl.BlockSpec((8, HEAD_DIM), col(NSA_KV)),
                      pl.BlockSpec((1, HEAD_DIM), lambda bi, g, pt: (0, 0)),
                      pl.BlockSpec((HEAD_DIM, HEAD_DIM), lambda bi, g, pt: (0, 0)),
                      pl.BlockSpec((1, HEAD_DIM), lambda bi, g, pt: (0, 0)),
                      pl.BlockSpec((1, HEAD_DIM), lambda bi, g, pt: (0, 0))],
            out_specs=pl.BlockSpec((None, ncp, HEAD_DIM), lambda bi, g, pt: (bi, 0, g)),
            scratch_shapes=[pltpu.VMEM((ncp, HEAD_DIM), F32), pltpu.VMEM((ncp + 8, HEAD_DIM), F32)],
        ),
        out_shape=jax.ShapeDtypeStruct((b, ncp, KV_W), F32),
        compiler_params=_params("parallel", "parallel"),
        name="nsa_compress",
    )(page_table, ab, ab, cpe, cpe, b1.reshape(1, -1), w2.astype(BF16), b2.reshape(1, -1),
      (g_norm if g_norm is not None else jnp.ones((HEAD_DIM,), F32)).reshape(1, -1))


def _nsa_cmp_kernel(q_ref, kc_ref, vc_ref, o_ref, sel_ref, *, tq, q_start, n_sel, nsp, blocks_on_rows):
    i = pl.program_id(2)
    kc = kc_ref[...].astype(BF16)
    vc = vc_ref[...].astype(BF16)
    ncp = kc.shape[0]
    qpos = q_start + i * tq + _iota((tq, 1), 0)
    end = _iota((tq, ncp), 1) * CMP_STRIDE + (CMP_LEN - 1)
    msk = end <= qpos
    imp = jnp.zeros((tq, ncp), F32)
    for r in range(NSA_HPG):
        sl = slice(r * HEAD_DIM, (r + 1) * HEAD_DIM)
        s = jnp.where(msk, _dot_nt(q_ref[:, sl], kc) * SCALE, NEG)
        p = jnp.where(msk, jnp.exp(s - jnp.max(s, axis=1, keepdims=True)), 0.0)
        d = jnp.sum(p, axis=1, keepdims=True)
        p = p / jnp.where(d > 0.0, d, 1.0)
        o_ref[:, sl] = _dot(p.astype(BF16), vc)
        imp = imp + p
    if blocks_on_rows:
        nsr = -(-n_sel // 8) * 8
        sj, cn = _iota((nsr, ncp), 0), _iota((nsr, ncp), 1)
        overlap_t = jnp.where((cn >= 4 * sj - 1) & (cn <= 4 * sj + 3), 1.0, 0.0).astype(BF16)
        score = sum(_dot_nt(overlap_t, part) for part in _split3(imp))
        j = _iota((nsr, tq), 0)
        qblk = (q_start + i * tq + _iota((nsr, tq), 1)) // SEL_BLOCK
        valid = j <= qblk
        forced = (j == 0) | (j == qblk) | (j == qblk - 1)
        score = jnp.where(valid, jnp.where(forced, jnp.inf, score), -jnp.inf)
        rank = jnp.zeros(score.shape, F32)
        for c in range(n_sel):
            sc = score[c:c + 1, :]
            rank = rank + jnp.where((sc > score) | ((sc == score) & (c < j)), 1.0, 0.0)
        sel = jnp.where(valid & (rank < min(SEL_TOPN, n_sel)), 1.0, 0.0)
        sel_ref[...] = jnp.concatenate([sel, jnp.zeros((nsp - nsr, tq), F32)], axis=0)
        return
    cn = _iota((ncp, nsp), 0)
    sj = _iota((ncp, nsp), 1)
    overlap = jnp.where((cn >= 4 * sj - 1) & (cn <= 4 * sj + 3), 1.0, 0.0).astype(BF16)
    score = sum(_dot(part, overlap) for part in _split3(imp))
    j = _iota((tq, nsp), 1)
    qblk = qpos // SEL_BLOCK
    valid = j <= qblk
    forced = (j == 0) | (j == qblk) | (j == qblk - 1)
    score = jnp.where(valid, jnp.where(forced, jnp.inf, score), -jnp.inf)
    sel_ref[...] = jnp.where(valid & (_rank_before(score, n_sel) < min(SEL_TOPN, n_sel)), 1.0, 0.0)


def nsa_compressed(qn, kc, vc, q_start, n_sel):
    b, t, w = qn.shape
    ncp = kc.shape[1]
    gw = NSA_HPG * HEAD_DIM
    tq = min(t, 256)
    nsp = -(-n_sel // LANE) * LANE
    blocks_on_rows = tq % LANE == 0
    if blocks_on_rows:
        sel_spec = pl.BlockSpec((None, None, nsp, tq), lambda bi, g, i: (bi, g, 0, i))
        sel_shape = jax.ShapeDtypeStruct((b, NSA_KV, nsp, t), F32)
    else:
        sel_spec = pl.BlockSpec((None, None, tq, nsp), lambda bi, g, i: (bi, g, i, 0))
        sel_shape = jax.ShapeDtypeStruct((b, NSA_KV, t, nsp), F32)
    return pl.pallas_call(
        functools.partial(_nsa_cmp_kernel, tq=tq, q_start=q_start, n_sel=n_sel, nsp=nsp, blocks_on_rows=blocks_on_rows),
        grid=(b, NSA_KV, t // tq),
        in_specs=[pl.BlockSpec((None, tq, gw), lambda bi, g, i: (bi, i, g)),
                  pl.BlockSpec((None, ncp, HEAD_DIM), lambda bi, g, i: (bi, 0, g)),
                  pl.BlockSpec((None, ncp, HEAD_DIM), lambda bi, g, i: (bi, 0, g))],
        out_specs=[pl.BlockSpec((None, tq, gw), lambda bi, g, i: (bi, i, g)), sel_spec],
        out_shape=[jax.ShapeDtypeStruct((b, t, w), F32), sel_shape],
        compiler_params=_params("parallel", "parallel", "parallel"),
        name="nsa_compressed",
    )(qn, kc, vc)


def _nsa_prompt_kernel(q_ref, sk_ref, sv_ref, wk_ref, wv_ref, sel_ref, ocmp_ref, gl_ref, o_ref,
                       m_ref, part_ref, acc_ref, s_ref, *, tq, tk):
    i = pl.program_id(2)
    q0 = i * tq
    tpos = q0 + _iota((tq, tk), 0)
    sel = sel_ref[...].astype(BF16)
    nsp = sel.shape[0]

    def sel_mask(k0):
        kpos = k0 + _iota((tq, tk), 1)
        expand = jnp.where(_iota((nsp, tk), 0) == (k0 + _iota((nsp, tk), 1)) // SEL_BLOCK, 1.0, 0.0).astype(BF16)
        return (_dot_tn(sel, expand) > 0.5) & (kpos <= tpos)

    def win_mask(k0):
        kpos = k0 + _iota((tq, tk), 1)
        return (kpos <= tpos) & (kpos > tpos - WINDOW)

    n_hi = (q0 + tq - 1) // tk + 1
    branches = ((sk_ref, sv_ref, sel_mask, 0), (wk_ref, wv_ref, win_mask, jnp.maximum(q0 - (WINDOW - 1), 0) // tk))
    fold = lambda a, op: functools.reduce(op, [a[:, c * LANE:(c + 1) * LANE] for c in range(tk // LANE)])
    for br, (k_ref, v_ref, mask_fn, n_lo) in enumerate(branches):
        base = br * NSA_HPG
        for r in range(NSA_HPG):
            part_ref[base + r] = jnp.full((tq, LANE), NEG, F32)
            acc_ref[base + r] = jnp.zeros((tq, HEAD_DIM), F32)

        def max_sweep(n, _, base=base, k_ref=k_ref, mask_fn=mask_fn, n_lo=n_lo):
            k0 = pl.multiple_of(n * tk, tk)
            kb, msk = k_ref[pl.ds(k0, tk), :].astype(BF16), mask_fn(k0)
            for r in range(NSA_HPG):
                s = jnp.where(msk, _dot_nt(q_ref[:, r * HEAD_DIM:(r + 1) * HEAD_DIM], kb), NEG)
                s_ref[n - n_lo, r] = s
                part_ref[base + r] = jnp.maximum(part_ref[base + r], fold(s, jnp.maximum))
            return 0

        lax.fori_loop(n_lo, n_hi, max_sweep, 0)
        for r in range(NSA_HPG):
            m_ref[base + r] = jnp.max(part_ref[base + r], axis=1, keepdims=True)
            part_ref[base + r] = jnp.zeros((tq, LANE), F32)

        def exp_sweep(n, _, base=base, v_ref=v_ref, n_lo=n_lo):
            vb = v_ref[pl.ds(pl.multiple_of(n * tk, tk), tk), :].astype(BF16)
            for r in range(NSA_HPG):
                p = jnp.exp2((s_ref[n - n_lo, r] - m_ref[base + r]) * EXP2_SCALE)
                part_ref[base + r] += fold(p, jnp.add)
                acc_ref[base + r] += _dot(p.astype(BF16), vb)
            return 0

        lax.fori_loop(n_lo, n_hi, exp_sweep, 0)

    gates = [_sigmoid(gl_ref[c]) for c in range(3)]
    for r in range(NSA_HPG):
        sl = slice(r * HEAD_DIM, (r + 1) * HEAD_DIM)
        o_sel = acc_ref[r] / jnp.sum(part_ref[r], axis=1, keepdims=True)
        o_win = acc_ref[NSA_HPG + r] / jnp.sum(part_ref[NSA_HPG + r], axis=1, keepdims=True)
        o = gates[0][:, r:r + 1] * ocmp_ref[:, sl] + gates[1][:, r:r + 1] * o_sel + gates[2][:, r:r + 1] * o_win
        o_ref[:, sl] = o.astype(o_ref.dtype)


def nsa_prompt(qn, skn, wkn, proj, sv_col0, wv_col0, selm, ocmp, gate_logits, tq=256, tk=256):
    b, t, w = qn.shape
    gw = NSA_HPG * HEAD_DIM
    tq, tk = min(t, tq), min(t, tk)
    nsp = selm.shape[2]
    assert selm.shape[3] == t and tq % LANE == 0
    kv = lambda off: pl.BlockSpec((None, t, HEAD_DIM), lambda bi, g, i: (bi, 0, off + g))
    return pl.pallas_call(
        functools.partial(_nsa_prompt_kernel, tq=tq, tk=tk),
        grid=(b, NSA_KV, t // tq),
        in_specs=[pl.BlockSpec((None, tq, gw), lambda bi, g, i: (bi, i, g)),
                  kv(0), kv(sv_col0 // HEAD_DIM), kv(0), kv(wv_col0 // HEAD_DIM),
                  pl.BlockSpec((None, None, nsp, tq), lambda bi, g, i: (bi, g, 0, i)),
                  pl.BlockSpec((None, tq, gw), lambda bi, g, i: (bi, i, g)),
                  pl.BlockSpec((3, None, None, tq, NSA_HPG), lambda bi, g, i: (0, bi, g, i, 0))],
        out_specs=pl.BlockSpec((None, tq, gw), lambda bi, g, i: (bi, i, g)),
        out_shape=jax.ShapeDtypeStruct((b, t, w), BF16),
        scratch_shapes=[pltpu.VMEM((2 * NSA_HPG, tq, 1), F32), pltpu.VMEM((2 * NSA_HPG, tq, LANE), F32),
                        pltpu.VMEM((2 * NSA_HPG, tq, HEAD_DIM), F32), pltpu.VMEM((t // tk, NSA_HPG, tq, tk), F32)],
        compiler_params=_params("parallel", "parallel", "arbitrary"),
        name="nsa_prompt",
    )(qn, skn, proj, wkn, proj, selm, ocmp, gate_logits)


def _nsa_dec_kernel(pt_ref, q_ref, sel_ref, ocmp_ref, gl_ref, *refs, tq, past_len, pages_per_step):
    kp_refs, vp_refs = refs[:pages_per_step], refs[pages_per_step:2 * pages_per_step]
    skn_ref, svn_ref, wkb_ref, wvb_ref, wkn_ref, wvn_ref, o_ref, m_ref, l_ref, acc_ref = refs[2 * pages_per_step:]
    p = pl.program_id(1)
    n_steps = pl.num_programs(1)
    rows = NSA_KV * NSA_HPG * tq
    per_g = NSA_HPG * tq
    q = q_ref[...]
    sel = sel_ref[...]

    @pl.when(p == 0)
    def _():
        m_ref[...] = jnp.full(m_ref.shape, NEG, F32)
        l_ref[...] = jnp.zeros_like(l_ref)
        acc_ref[...] = jnp.zeros_like(acc_ref)

    def grid_of(n_keys):
        shape = (rows, n_keys * NSA_KV)
        lane, row = _iota(shape, 1), _iota(shape, 0)
        return lane // NSA_KV, lane % NSA_KV == row // per_g, row % tq

    def update(carry, k_rows, v_rows, msk):
        s = _dot_nt(q, k_rows.astype(BF16)) * SCALE
        return _online_update(carry, s, msk, v_rows.astype(BF16))

    key, same_g, _ = grid_of(PAGE)
    blocks_per_page = PAGE // SEL_BLOCK
    carry = (m_ref[...], l_ref[...], acc_ref[...])
    for j in range(pages_per_step):
        blk0 = blocks_per_page * (pages_per_step * p + j)
        picked = jnp.where(key < SEL_BLOCK, _pick_col(sel, blk0), _pick_col(sel, blk0 + 1))
        carry = update(carry, kp_refs[j][...], vp_refs[j][...], same_g & (picked > 0.5))
    m_ref[...], l_ref[...], acc_ref[...] = carry

    @pl.when(p == n_steps - 1)
    def _():
        n_new = skn_ref.shape[0] // NSA_KV
        key, same_g, t_row = grid_of(n_new)
        own = same_g & (key <= t_row) & (key < tq)
        new_blk = _pick_col(sel, past_len // SEL_BLOCK) > 0.5
        o_sel = _flash_out(update(carry, skn_ref[...], svn_ref[...], own & new_blk))
        wkey, wsame, wt = grid_of(WINDOW)
        w_carry = update(_flash_init(rows, HEAD_DIM), wkb_ref[...], wvb_ref[...], wsame & (wkey > wt))
        o_win = _flash_out(update(w_carry, wkn_ref[...], wvn_ref[...], own))
        gates = _sigmoid(gl_ref[...])
        o_ref[...] = gates[:, 0:1] * ocmp_ref[...] + gates[:, 1:2] * o_sel + gates[:, 2:3] * o_win


def nsa_sample(qn, skn_new, sv_new, wkn_new, wv_new, cache_sk, cache_sv, win_k, win_v, layer,
               page_table, selm, ocmp, gate_logits):
    b, tq, w = qn.shape
    rows = NSA_KV * NSA_HPG * tq
    npages = page_table.shape[1]
    past_len = npages * PAGE
    nsp = selm.shape[-1]
    assert win_k.shape[2] == WINDOW * NSA_KV and past_len % SEL_BLOCK == 0 and tq * NSA_KV <= LANE
    selrows = jnp.broadcast_to(selm[:, :, None], (b, NSA_KV, NSA_HPG, tq, nsp)).reshape(b, rows, nsp)
    to_rows = lambda a: a.reshape(b, tq, NSA_KV, NSA_HPG, -1).transpose(0, 2, 3, 1, 4).reshape(b, rows, -1)
    new_rows = lambda a: jnp.pad(a.reshape(b, tq * NSA_KV, HEAD_DIM), ((0, 0), (0, LANE - tq * NSA_KV), (0, 0)))
    per_b = lambda *shape: pl.BlockSpec((None,) + shape, lambda bi, p, pt: (bi,) + (0,) * len(shape))
    pps = 4 if npages % 4 == 0 else 1
    pages = [pl.BlockSpec((None, None, PAGE * NSA_KV, HEAD_DIM), lambda bi, p, pt, j=j: (layer, pt[bi, pps * p + j], 0, 0))
             for j in range(pps)]
    wbuf = pl.BlockSpec((None, None, WINDOW * NSA_KV, HEAD_DIM), lambda bi, p, pt: (layer, bi, 0, 0))
    o = pl.pallas_call(
        functools.partial(_nsa_dec_kernel, tq=tq, past_len=past_len, pages_per_step=pps),
        grid_spec=pltpu.PrefetchScalarGridSpec(
            num_scalar_prefetch=1,
            grid=(b, npages // pps),
            in_specs=[per_b(rows, HEAD_DIM), per_b(rows, nsp), per_b(rows, HEAD_DIM), per_b(rows, 3)]
                     + pages + pages + [per_b(LANE, HEAD_DIM), per_b(LANE, HEAD_DIM), wbuf, wbuf,
                                        per_b(LANE, HEAD_DIM), per_b(LANE, HEAD_DIM)],
            out_specs=per_b(rows, HEAD_DIM),
            scratch_shapes=[pltpu.VMEM((rows, 1), F32), pltpu.VMEM((rows, 1), F32), pltpu.VMEM((rows, HEAD_DIM), F32)],
        ),
        out_shape=jax.ShapeDtypeStruct((b, rows, HEAD_DIM), F32),
        compiler_params=_params("parallel", "arbitrary"),
        name="nsa_sample",
    )(page_table, to_rows(qn), selrows, to_rows(ocmp), to_rows(gate_logits), *([cache_sk] * pps), *([cache_sv] * pps),
      new_rows(skn_new), new_rows(sv_new), win_k, win_v, new_rows(wkn_new), new_rows(wv_new))
    return o.reshape(b, NSA_KV, NSA_HPG, tq, HEAD_DIM).transpose(0, 3, 1, 2, 4).reshape(b, tq, w)


MOBA_W = MOBA_HEADS * HEAD_DIM
CONV_DIM = D_INNER + 2 * SSD_GROUPS * SSD_STATE
EVEN_MAIN = 3 * MOBA_W + D_INNER + CONV_DIM
Z_COL0 = 0
XBC_COL0 = D_INNER
Q_COL0 = XBC_COL0 + CONV_DIM
K_COL0 = Q_COL0 + MOBA_W
V_COL0 = K_COL0 + MOBA_W
NSA_Q = NSA_KV * NSA_HPG * HEAD_DIM
ODD_MAIN = NSA_Q + 6 * KV_W
D_FF = 8192


PROJ_TN = 512


def _ws(a, w, layer, res=None, **kw):
    return a, w, layer, res, kw


def _run_pair(big, small):
    req_big, req_small = next(big), next(small)
    while True:
        a, w, layer, res, kw = req_big
        out_big, out_small = matmul_ws(a, w, layer, res=res, small=req_small[0], small_res=req_small[3], **kw)
        try:
            req_big = big.send(out_big)
        except StopIteration as done_big:
            try:
                small.send(out_small)
            except StopIteration as done_small:
                return done_big.value, done_small.value
            raise AssertionError("the two groups must issue the same matmul sequence")
        req_small = small.send(out_small)


def _proj(h2d, w, layer, main, out_cols=None):
    wt = jnp.swapaxes(w, 1, 2)
    tail = jnp.pad(wt[layer, main:, :], ((0, LANE - (w.shape[2] - main)), (0, 0)))
    main_out = yield _ws(h2d, wt, layer, k=h2d.shape[1], n=main, out_cols=out_cols, w_t=True, tn=PROJ_TN)
    return main_out, matmul(h2d, tail, b_t=True)


def _even_layer(x, sample, q_start, cache_k, cache_v, e, page_table, conv_prev, ssm_prev,
                g_norm, w_in, w_out, qk_g, conv_w, conv_b, dt_bias, a_log, d_skip, g_out):
    b, t, d = x.shape
    x2 = x.reshape(b * t, d)
    h = rmsnorm_rows(x2, g_norm)
    n_tiles = EVEN_MAIN // PROJ_TN
    proj, dt_raw = yield from _proj(h, w_in, e, EVEN_MAIN, out_cols=lambda j: (j + Q_COL0 // PROJ_TN) % n_tiles)
    proj = proj.reshape(b, t, EVEN_MAIN)
    dt_raw = dt_raw.reshape(b, t, LANE)
    qn = head_norm(proj, Q_COL0, MOBA_HEADS, qk_g[0], BF16)
    kn = head_norm(proj, K_COL0, MOBA_HEADS, qk_g[1], F32)
    v = proj[..., V_COL0:]
    if sample:
        o_attn = moba_sample(qn, kn, v, cache_k, cache_v, e, page_table).astype(BF16)
    else:
        o_attn = moba_prompt(qn, kn, proj, V_COL0)
    conv = dwconv(proj, XBC_COL0, CONV_DIM, _history8(conv_prev, b, CONV_DIM), conv_w, conv_b, False, F32)
    xbc = proj[..., XBC_COL0:Q_COL0]
    conv_state = jnp.concatenate([_history8(conv_prev, b, CONV_DIM), xbc], axis=1)[:, -(SSD_CONV - 1):]
    if t % SSD_CHUNK:
        tp = -(-t // SSD_CHUNK) * SSD_CHUNK
        padt = lambda a: jnp.pad(a, ((0, 0), (0, tp - t), (0, 0)))
        mixed, ssm = ssd_mixer(padt(conv), padt(proj[..., Z_COL0:XBC_COL0]), 0, padt(dt_raw), ssm_prev,
                               dt_bias, a_log, d_skip, g_out, t, padt(o_attn))
        mixed = mixed[:, :t]
    else:
        mixed, ssm = ssd_mixer(conv, proj, Z_COL0, dt_raw, ssm_prev, dt_bias, a_log, d_skip, g_out, None, o_attn)
    mixed = mixed.reshape(b * t, MOBA_W + D_INNER)
    x2 = yield _ws(mixed, w_out, e, res=x2, k=MOBA_W + D_INNER, n=d, tk=MOBA_W)
    heads = lambda a: a.reshape(b, t, MOBA_HEADS, HEAD_DIM)
    return x2.reshape(b, t, d), heads(kn), heads(v), conv_state, ssm


def _odd_layer(x, sample, q_start, caches, e, page_table, win_k, win_v,
               g_norm, w_in, w_out, qk_g, pe, w1, b1, w2, b2):
    b, t, d = x.shape
    x2 = x.reshape(b * t, d)
    h = rmsnorm_rows(x2, g_norm)
    proj, gl = yield from _proj(h, w_in, e, ODD_MAIN)
    proj = proj.reshape(b, t, ODD_MAIN)
    gl = gl[:, :3 * NSA_KV * NSA_HPG].reshape(b, t, NSA_KV, NSA_HPG, 3)
    col = lambda i: NSA_Q + i * KV_W
    qn = head_norm(proj, 0, NSA_KV * NSA_HPG, qk_g[0], BF16)
    skn = head_norm(proj, col(2), NSA_KV, qk_g[2], F32)
    wkn = head_norm(proj, col(4), NSA_KV, qk_g[3], F32)
    ck, cv, sv, wv = (proj[..., col(i):col(i + 1)] for i in (0, 1, 3, 5))
    if sample:
        cache_ck, cache_cv, cache_sk, cache_sv = caches
        assert t < CMP_STRIDE
        chunks = lambda c: c.reshape(c.shape[0], -1, CMP_STRIDE * NSA_KV, HEAD_DIM)
        rows_k, rows_v, pt, src = chunks(cache_ck), chunks(cache_cv), page_table, e
        n_rows = page_table.shape[1] * PAGE + t
    else:
        chunks = lambda a: a.reshape(1, b * t // CMP_STRIDE, CMP_STRIDE * NSA_KV, HEAD_DIM)
        rows_k, rows_v, src = chunks(ck), chunks(cv), 0
        npg = t // PAGE
        pt = (jnp.arange(b, dtype=jnp.int32)[:, None] * npg + jnp.arange(npg, dtype=jnp.int32)[None, :])
        n_rows = t
    kc = compress_tokens(rows_k, src, pt, pe[0], w1[0], b1[0], w2[0], b2[0], qk_g[1])
    vc = compress_tokens(rows_v, src, pt, pe[1], w1[1], b1[1], w2[1], b2[1], None)
    n_sel = -(-n_rows // SEL_BLOCK)
    ocmp, selm = nsa_compressed(qn, kc, vc, q_start, n_sel)
    if sample:
        rows_view = lambda c: c.reshape(c.shape[0], c.shape[1], c.shape[2] * NSA_KV, HEAD_DIM)
        o = nsa_sample(qn, skn, sv, wkn, wv, rows_view(cache_sk), rows_view(cache_sv), rows_view(win_k),
                       rows_view(win_v), e, page_table, selm, ocmp, gl).astype(BF16)
        wk_all = jnp.concatenate([win_k[e].reshape(b, -1, KV_W), wkn], axis=1)
        wv_all = jnp.concatenate([win_v[e].reshape(b, -1, KV_W), wv], axis=1)
    else:
        o = nsa_prompt(qn, skn, wkn, proj, col(3), col(5), selm, ocmp, gl.transpose(4, 0, 2, 1, 3))
        wk_all, wv_all = wkn, wv
    x2 = yield _ws(o.reshape(b * t, NSA_Q), w_out, e, res=x2, k=NSA_Q, n=d)
    keep = min(WINDOW, wk_all.shape[1])
    kvh = lambda a: a.reshape(b, a.shape[1], NSA_KV, HEAD_DIM)
    return (x2.reshape(b, t, d), kvh(ck), kvh(cv), kvh(skn), kvh(sv),
            kvh(wk_all[:, wk_all.shape[1] - keep:]), kvh(wv_all[:, wv_all.shape[1] - keep:]))


def _conv_ffn(x, conv_prev, li, g_norm, w_up, conv_w, conv_b, w_down):
    b, t, d = x.shape
    x2 = x.reshape(b * t, d)
    u = (yield _ws(rmsnorm_rows(x2, g_norm), w_up, li, k=d, n=2 * D_FF)).reshape(b, t, 2 * D_FF)
    hist = _history8(conv_prev, b, 2 * D_FF)
    act = dwconv(u, 0, D_FF, hist, conv_w, conv_b, True, BF16)
    state = jnp.concatenate([hist, u], axis=1)[:, -(FFN_CONV - 1):]
    x2 = yield _ws(act.reshape(b * t, D_FF), w_down, li, res=x2, k=D_FF, n=d)
    return x2.reshape(b, t, d), state


def kernel(x_prompt, x_sample, cache_moba_k, cache_moba_v, state_ssd, state_ssd_conv, cache_nsa_cmp_k, cache_nsa_cmp_v, cache_nsa_sel_k, cache_nsa_sel_v, state_nsa_win_k, state_nsa_win_v, state_ffn_conv, page_table, norm_mix, norm_ffn, even_w_in, even_w_out, moba_qk_norm, ssd_conv_w, ssd_conv_b, ssd_dt_bias, ssd_a_log, ssd_d, ssd_norm, odd_w_in, odd_w_out, nsa_qk_norm, cmp_pe, cmp_w1, cmp_b1, cmp_w2, cmp_b2, ffn_w_up, ffn_conv_w, ffn_conv_b, ffn_w_down):
    depth = norm_mix.shape[0]
    past_len = page_table.shape[1] * PAGE
    state_keys = ("moba_k", "moba_v", "ssd", "ssd_conv", "nsa_cmp_k", "nsa_cmp_v",
                  "nsa_sel_k", "nsa_sel_v", "nsa_win_k", "nsa_win_v", "ffn_conv")
    moba_pool = lambda c: c.reshape(c.shape[0], c.shape[1], PAGE * MOBA_HEADS, HEAD_DIM)
    cache_mk, cache_mv = moba_pool(cache_moba_k), moba_pool(cache_moba_v)

    def trunk(x, q_start, sample):
        b = x.shape[0]
        new = {name: [] for name in state_keys}
        for li in range(depth):
            e = li // 2
            if li % 2 == 0:
                cp = state_ssd_conv[e] if sample else None
                sp = state_ssd[e] if sample else jnp.zeros((b, SSD_HEADS, SSD_HEAD_DIM, SSD_STATE), F32)
                x, k, v, cs, ss = yield from _even_layer(x, sample, q_start, cache_mk, cache_mv, e, page_table, cp, sp,
                                                         norm_mix[li], even_w_in, even_w_out, moba_qk_norm[e],
                                                         ssd_conv_w[e], ssd_conv_b[e], ssd_dt_bias[e], ssd_a_log[e],
                                                         ssd_d[e], ssd_norm[e])
                for name, val in zip(("moba_k", "moba_v", "ssd_conv", "ssd"), (k, v, cs, ss)):
                    new[name].append(val)
            else:
                caches = (cache_nsa_cmp_k, cache_nsa_cmp_v, cache_nsa_sel_k, cache_nsa_sel_v)
                outs = yield from _odd_layer(x, sample, q_start, caches, e, page_table, state_nsa_win_k,
                                             state_nsa_win_v, norm_mix[li], odd_w_in, odd_w_out, nsa_qk_norm[e],
                                             cmp_pe[e], cmp_w1[e], cmp_b1[e], cmp_w2[e], cmp_b2[e])
                x = outs[0]
                for name, val in zip(("nsa_cmp_k", "nsa_cmp_v", "nsa_sel_k", "nsa_sel_v", "nsa_win_k", "nsa_win_v"), outs[1:]):
                    new[name].append(val)
            fp = state_ffn_conv[li] if sample else None
            x, fs = yield from _conv_ffn(x, fp, li, norm_ffn[li], ffn_w_up, ffn_conv_w[li], ffn_conv_b[li], ffn_w_down)
            new["ffn_conv"].append(fs)
        return x, {name: jnp.stack(rows) for name, rows in new.items()}

    (y_prompt, sp), (y_sample, ss) = _run_pair(trunk(x_prompt, 0, False), trunk(x_sample, past_len, True))
    out = [y_prompt, y_sample]
    for name in state_keys:
        out += [sp[name], ss[name]]
    return tuple(out)
```

```python
import functools

import numpy as np
import jax
import jax.numpy as jnp
from jax import lax
from jax.experimental import pallas as pl
from jax.experimental.pallas import tpu as pltpu

F32 = jnp.float32
BF16 = jnp.bfloat16

LANE = 128
VMEM_LIMIT_BYTES = 56 * 1024 * 1024

EPS = 1e-6
HEAD_DIM = 128
SCALE = HEAD_DIM ** -0.5
EXP2_SCALE = SCALE * float(np.log2(np.e))
NEG = -1e30

PAGE = 128
MOBA_HEADS = 16
MOBA_BLOCK = 256
MOBA_TOPK = 3
SSD_HEADS = 64
SSD_HEAD_DIM = 64
SSD_GROUPS = 8
SSD_STATE = 128
SSD_CONV = 4
SSD_CHUNK = 128
D_INNER = 4096
GROUP_W = D_INNER // SSD_GROUPS
HEADS_PER_GROUP = SSD_HEADS // SSD_GROUPS
NSA_KV = 4
NSA_HPG = 8
KV_W = NSA_KV * HEAD_DIM
CMP_LEN = 32
CMP_STRIDE = 16
SEL_BLOCK = 64
SEL_TOPN = 16
WINDOW = 512
FFN_CONV = 3


def _params(*sem, vmem_limit=VMEM_LIMIT_BYTES):
    return pltpu.CompilerParams(dimension_semantics=sem, vmem_limit_bytes=vmem_limit)


def _dot(a, b):
    return jnp.dot(a, b, preferred_element_type=F32)


def _dot_nt(a, b):
    return lax.dot_general(a, b, (((1,), (1,)), ((), ())), preferred_element_type=F32)


def _dot_tn(a, b):
    return lax.dot_general(a, b, (((0,), (0,)), ((), ())), preferred_element_type=F32)


def _split3(x):
    hi = x.astype(BF16)
    r1 = x - hi.astype(F32)
    mid = r1.astype(BF16)
    lo = (r1 - mid.astype(F32)).astype(BF16)
    return hi, mid, lo


def _sigmoid(x):
    return 1.0 / (1.0 + jnp.exp(-x))


def _silu(x):
    return x * _sigmoid(x)


def _softplus(x):
    return jnp.maximum(x, 0.0) + jnp.log1p(jnp.exp(-jnp.abs(x)))


def _gelu_tanh(x):
    return 0.5 * x * (1.0 + jnp.tanh(np.sqrt(2.0 / np.pi).astype(np.float32) * (x + 0.044715 * (x * x * x))))


def _rms(x, g):
    return x * lax.rsqrt(jnp.mean(x * x, axis=-1, keepdims=True) + EPS) * g


def _iota(shape, axis):
    return lax.broadcasted_iota(jnp.int32, shape, axis)


def _rank_before(score, n_cols):
    col = _iota(score.shape, 1)
    rank = jnp.zeros(score.shape, F32)
    for c in range(n_cols):
        sc = score[:, c:c + 1]
        rank = rank + jnp.where((sc > score) | ((sc == score) & (c < col)), 1.0, 0.0)
    return rank


def _pick_col(x, c):
    return jnp.sum(jnp.where(_iota(x.shape, 1) == c, x, 0.0), axis=1, keepdims=True)


def _online_update(carry, s, msk, v_bf16):
    m, l, acc = carry
    s = jnp.where(msk, s, NEG)
    m_new = jnp.maximum(m, jnp.max(s, axis=1, keepdims=True))
    p = jnp.where(msk, jnp.exp(s - m_new), 0.0)
    alpha = jnp.exp(m - m_new)
    l = alpha * l + jnp.sum(p, axis=1, keepdims=True)
    acc = alpha * acc + _dot(p.astype(BF16), v_bf16)
    return m_new, l, acc


def _flash_init(rows, dv):
    return (jnp.full((rows, 1), NEG, F32), jnp.zeros((rows, 1), F32), jnp.zeros((rows, dv), F32))


def _flash_out(carry):
    _, l, acc = carry
    return acc / jnp.where(l > 0.0, l, 1.0)


def _rmsnorm_kernel(x_ref, g_ref, o_ref):
    o_ref[...] = _rms(x_ref[...], g_ref[...]).astype(o_ref.dtype)


def rmsnorm_rows(x2d, g):
    m, d = x2d.shape
    tm = min(m, 256)
    return pl.pallas_call(
        _rmsnorm_kernel,
        grid=(m // tm,),
        in_specs=[pl.BlockSpec((tm, d), lambda i: (i, 0)), pl.BlockSpec((1, d), lambda i: (0, 0))],
        out_specs=pl.BlockSpec((tm, d), lambda i: (i, 0)),
        out_shape=jax.ShapeDtypeStruct((m, d), BF16),
        compiler_params=_params("parallel"),
        name="rmsnorm_rows",
    )(x2d, g.reshape(1, d))


def _matmul_kernel(*refs, nk, has_res, b_t):
    if has_res:
        a_ref, b_ref, r_ref, o_ref = refs
    else:
        a_ref, b_ref, o_ref = refs
    part = (_dot_nt if b_t else _dot)(a_ref[...].astype(BF16), b_ref[...].astype(BF16))
    if nk == 1:
        o_ref[...] = part + r_ref[...] if has_res else part
        return
    k = pl.program_id(2)

    @pl.when(k == 0)
    def _():
        o_ref[...] = part + r_ref[...] if has_res else part

    @pl.when(k > 0)
    def _():
        o_ref[...] += part


def matmul(a, b, res=None, b_t=False, tm=1024, tn=512, tk=4096):
    m, k = a.shape
    n = b.shape[0] if b_t else b.shape[1]
    tm, tn, tk = min(tm, m), min(tn, n), min(tk, k)
    assert m % tm == 0 and n % tn == 0 and k % tk == 0, (a.shape, b.shape)
    nk = k // tk
    b_spec = pl.BlockSpec((tn, tk), lambda i, j, kk: (j, kk)) if b_t else pl.BlockSpec((tk, tn), lambda i, j, kk: (kk, j))
    in_specs = [pl.BlockSpec((tm, tk), lambda i, j, kk: (i, kk)), b_spec]
    args = [a, b]
    if res is not None:
        in_specs.append(pl.BlockSpec((tm, tn), lambda i, j, kk: (i, j)))
        args.append(res)
    return pl.pallas_call(
        functools.partial(_matmul_kernel, nk=nk, has_res=res is not None, b_t=b_t),
        grid=(m // tm, n // tn, nk),
        in_specs=in_specs,
        out_specs=pl.BlockSpec((tm, tn), lambda i, j, kk: (i, j)),
        out_shape=jax.ShapeDtypeStruct((m, n), F32),
        compiler_params=_params("parallel", "parallel", "arbitrary"),
        name="matmul",
    )(*args)


def _matmul_ws_kernel(*refs, nk, has_res, has_small, w_t):
    refs = list(refs)
    a_ref, w_ref = refs.pop(0), refs.pop(0)
    r_ref = refs.pop(0) if has_res else None
    a2_ref = refs.pop(0) if has_small else None
    r2_ref = refs.pop(0) if has_small and has_res else None
    o_ref = refs.pop(0)
    o2_ref = refs.pop(0) if has_small else None
    wbf_ref, = refs
    i = pl.program_id(1)
    kk = pl.program_id(2)
    mm = _dot_nt if w_t else _dot

    def accumulate(out_ref, part, res_ref):
        if nk == 1:
            out_ref[...] = part + res_ref[...] if has_res else part
            return

        @pl.when(kk == 0)
        def _():
            out_ref[...] = part + res_ref[...] if has_res else part

        @pl.when(kk > 0)
        def _():
            out_ref[...] += part

    @pl.when(i == 0)
    def _():
        wbf_ref[kk] = w_ref[...].astype(BF16)
        if has_small:
            accumulate(o2_ref, mm(a2_ref[...], wbf_ref[kk]), r2_ref)

    accumulate(o_ref, mm(a_ref[...], wbf_ref[kk]), r_ref)


def matmul_ws(a, w, layer, *, k, n, res=None, small=None, small_res=None, out_cols=None, w_t=False,
              tm=1024, tn=512, tk=4096, vmem_limit=VMEM_LIMIT_BYTES):
    m = a.shape[0]
    tm, tn, tk = min(tm, m), min(tn, n), min(tk, k)
    assert m % tm == 0 and n % tn == 0 and k % tk == 0, (a.shape, w.shape, k, n)
    assert (res is None) == (small_res is None) or small is None
    nk = k // tk
    k_idx = (lambda i, kk: 0) if nk == 1 else (lambda i, kk: jnp.where(i == 0, kk, nk - 1))
    if w_t:
        w_spec = pl.BlockSpec((None, tn, tk), lambda j, i, kk: (layer, j, k_idx(i, kk)))
    else:
        w_spec = pl.BlockSpec((None, tk, tn), lambda j, i, kk: (layer, k_idx(i, kk), j))
    col = (lambda j: j) if out_cols is None else out_cols
    o_spec = pl.BlockSpec((tm, tn), lambda j, i, kk: (i, col(j)))
    in_specs = [pl.BlockSpec((tm, tk), lambda j, i, kk: (i, kk)), w_spec]
    args = [a, w]
    out_specs, out_shape = [o_spec], [jax.ShapeDtypeStruct((m, n), F32)]
    if res is not None:
        in_specs.append(o_spec)
        args.append(res)
    if small is not None:
        m2 = small.shape[0]
        o2_spec = pl.BlockSpec((m2, tn), lambda j, i, kk: (0, col(j)))
        in_specs.append(pl.BlockSpec((m2, tk), lambda j, i, kk: (0, k_idx(i, kk))))
        args.append(small)
        if res is not None:
            in_specs.append(o2_spec)
            args.append(small_res)
        out_specs.append(o2_spec)
        out_shape.append(jax.ShapeDtypeStruct((m2, n), F32))
    outs = pl.pallas_call(
        functools.partial(_matmul_ws_kernel, nk=nk, has_res=res is not None, has_small=small is not None, w_t=w_t),
        grid=(n // tn, m // tm, nk),
        in_specs=in_specs,
        out_specs=out_specs,
        out_shape=out_shape,
        scratch_shapes=[pltpu.VMEM((nk, tn, tk) if w_t else (nk, tk, tn), BF16)],
        compiler_params=_params("arbitrary", "arbitrary", "arbitrary", vmem_limit=vmem_limit),
        name="matmul_ws",
    )(*args)
    return outs[0] if small is None else tuple(outs)


def _head_norm_kernel(x_ref, g_ref, o_ref, *, heads):
    g = g_ref[...]
    for h in range(heads):
        sl = slice(h * HEAD_DIM, (h + 1) * HEAD_DIM)
        o_ref[:, sl] = _rms(x_ref[:, sl], g).astype(o_ref.dtype)


def head_norm(x, col0, n_heads, g, out_dtype):
    b, t, _ = x.shape
    hb = min(n_heads, 4)
    w = hb * HEAD_DIM
    assert col0 % w == 0 and n_heads % hb == 0
    tq = min(t, 512)
    return pl.pallas_call(
        functools.partial(_head_norm_kernel, heads=hb),
        grid=(b, t // tq, n_heads // hb),
        in_specs=[pl.BlockSpec((None, tq, w), lambda bi, i, j: (bi, i, col0 // w + j)),
                  pl.BlockSpec((1, HEAD_DIM), lambda bi, i, j: (0, 0))],
        out_specs=pl.BlockSpec((None, tq, w), lambda bi, i, j: (bi, i, j)),
        out_shape=jax.ShapeDtypeStruct((b, t, n_heads * HEAD_DIM), out_dtype),
        compiler_params=_params("parallel", "parallel", "parallel"),
        name="head_norm",
    )(x, g.reshape(1, HEAD_DIM))


CONV_ROWS = 256


def _dwconv_kernel(*refs, width, t, glu):
    if glu:
        xa_ref, xg_ref, pa_ref, pg_ref, wa_ref, wg_ref, ba_ref, bg_ref, o_ref, sa_ref, sg_ref = refs
        pairs = ((xa_ref, pa_ref, sa_ref), (xg_ref, pg_ref, sg_ref))
    else:
        xa_ref, pa_ref, wa_ref, ba_ref, o_ref, sa_ref = refs
        pairs = ((xa_ref, pa_ref, sa_ref),)
    for x_ref, p_ref, s_ref in pairs:
        s_ref[0:8, :] = p_ref[...]
        s_ref[8:8 + t, :] = x_ref[...]
    tc = min(t, CONV_ROWS)

    def conv(s_ref, w_ref, b_ref, r0):
        acc = b_ref[...]
        for i in range(width):
            acc = acc + w_ref[i:i + 1, :] * s_ref[pl.ds(8 - (width - 1) + i + r0, tc), :]
        return acc

    for c in range(t // tc):
        r0 = c * tc
        a = conv(sa_ref, wa_ref, ba_ref, r0)
        if glu:
            o_ref[r0:r0 + tc, :] = (a * _silu(conv(sg_ref, wg_ref, bg_ref, r0))).astype(o_ref.dtype)
        else:
            o_ref[r0:r0 + tc, :] = _silu(a).astype(o_ref.dtype)


def dwconv(x, col0, c_out, prev8, w, bias, glu, out_dtype):
    b, t, _ = x.shape
    width = w.shape[0]
    tc = min(256 if t > 64 else 2048, c_out)
    assert col0 % tc == 0 and c_out % tc == 0
    nj = c_out // tc
    x_spec = lambda off: pl.BlockSpec((None, t, tc), lambda bi, j: (bi, 0, off + j))
    p_spec = lambda off: pl.BlockSpec((None, 8, tc), lambda bi, j: (bi, 0, off + j))
    w_spec = lambda off: pl.BlockSpec((width, tc), lambda bi, j: (0, off + j))
    b_spec = lambda off: pl.BlockSpec((1, tc), lambda bi, j: (0, off + j))
    bias2 = bias.reshape(1, -1)
    if glu:
        in_specs = [x_spec(col0 // tc), x_spec(col0 // tc + nj), p_spec(0), p_spec(nj),
                    w_spec(0), w_spec(nj), b_spec(0), b_spec(nj)]
        args = (x, x, prev8, prev8, w, w, bias2, bias2)
        scratch = [pltpu.VMEM((8 + t, tc), F32), pltpu.VMEM((8 + t, tc), F32)]
    else:
        in_specs = [x_spec(col0 // tc), p_spec(0), w_spec(0), b_spec(0)]
        args = (x, prev8, w, bias2)
        scratch = [pltpu.VMEM((8 + t, tc), F32)]
    return pl.pallas_call(
        functools.partial(_dwconv_kernel, width=width, t=t, glu=glu),
        grid=(b, nj),
        in_specs=in_specs,
        out_specs=pl.BlockSpec((None, t, tc), lambda bi, j: (bi, 0, j)),
        out_shape=jax.ShapeDtypeStruct((b, t, c_out), out_dtype),
        scratch_shapes=scratch,
        compiler_params=_params("parallel", "parallel"),
        name="dwconv_glu" if glu else "dwconv_silu",
    )(*args)


def _history8(state, b, c):
    if state is None:
        return jnp.zeros((b, 8, c), F32)
    return jnp.concatenate([jnp.zeros((b, 8 - state.shape[1], c), F32), state], axis=1)


def _moba_tile(i, nb, q_ref, o_ref, kmean_ref, kbf_ref, vbf_ref, spread_ref):
    blk, half = MOBA_BLOCK, MOBA_BLOCK // 2
    fold = lambda a, op: functools.reduce(op, [a[:, c * LANE:(c + 1) * LANE] for c in range(a.shape[1] // LANE)])
    k_blk = lambda n: kbf_ref[n * blk:(n + 1) * blk, :]
    v_blk = lambda n: vbf_ref[n * blk:(n + 1) * blk, :]
    if i > 0:
        gate = _dot_nt(kmean_ref[...].astype(BF16), q_ref[...])[:-(-nb // 8) * 8, :]
        blk_id = _iota(gate.shape, 0)
        valid = blk_id < i
        gate = jnp.where(valid, gate, -jnp.inf)
        rank = jnp.zeros(gate.shape, F32)
        for c in range(i):
            gc = gate[c:c + 1, :]
            rank = rank + jnp.where((gc > gate) | ((gc == gate) & (c < blk_id)), 1.0, 0.0)
        sel_t = jnp.where(valid & (rank < min(MOBA_TOPK, nb)), 1.0, 0.0)
        sel_t = jnp.concatenate([sel_t, jnp.zeros((LANE - sel_t.shape[0], blk), F32)], axis=0).astype(BF16)
        selb_all = _dot_tn(sel_t, spread_ref[:, :i * LANE])
    for hq in range(2):
        q = q_ref[hq * half:(hq + 1) * half, :]
        masks = []
        if i > 0:
            selb = selb_all[hq * half:(hq + 1) * half, :]
            masks = [jnp.concatenate([selb[:, n * LANE:(n + 1) * LANE]] * (blk // LANE), axis=1) > 0.5 for n in range(i)]
        causal = _iota((half, blk), 1) <= _iota((half, blk), 0) + hq * half
        masks.append(causal)
        scores = lambda n: jnp.where(masks[n], _dot_nt(q, k_blk(n)), NEG)
        part = functools.reduce(jnp.maximum, [fold(scores(n), jnp.maximum) for n in range(i + 1)])
        m = jnp.max(part, axis=1, keepdims=True)
        l_part, acc = None, None
        for n in range(i + 1):
            p = jnp.exp2((scores(n) - m) * EXP2_SCALE)
            pv = _dot(p.astype(BF16), v_blk(n))
            l_part = fold(p, jnp.add) if n == 0 else l_part + fold(p, jnp.add)
            acc = pv if n == 0 else acc + pv
        l = jnp.sum(l_part, axis=1, keepdims=True)
        o_ref[hq * half:(hq + 1) * half, :] = (acc / l).astype(o_ref.dtype)


def _moba_tiles_kernel(q_ref, k_ref, v_ref, o_ref, kmean_ref, kbf_ref, vbf_ref, spread_ref, *, nb):
    blk = MOBA_BLOCK
    i = pl.program_id(2)

    @pl.when(i == 0)
    def _():
        kmean_ref[...] = jnp.zeros_like(kmean_ref)
        for n in range(nb):
            kmean_ref[n:n + 1, :] = jnp.mean(k_ref[n * blk:(n + 1) * blk, :], axis=0, keepdims=True)
        kbf_ref[...] = k_ref[...].astype(BF16)
        vbf_ref[...] = v_ref[...].astype(BF16)
        shape = spread_ref.shape
        spread_ref[...] = jnp.where(_iota(shape, 0) == _iota(shape, 1) // LANE, 1.0, 0.0).astype(BF16)

    for c in range(nb):
        @pl.when(i == c)
        def _(c=c):
            _moba_tile(c, nb, q_ref, o_ref, kmean_ref, kbf_ref, vbf_ref, spread_ref)


def moba_prompt(qn, kn, proj, v_col0):
    b, t, w = qn.shape
    nh = w // HEAD_DIM
    nb = t // MOBA_BLOCK
    assert t % MOBA_BLOCK == 0 and nb <= LANE
    return pl.pallas_call(
        functools.partial(_moba_tiles_kernel, nb=nb),
        grid=(b, nh, nb),
        in_specs=[pl.BlockSpec((None, MOBA_BLOCK, HEAD_DIM), lambda bi, h, i: (bi, i, h)),
                  pl.BlockSpec((None, t, HEAD_DIM), lambda bi, h, i: (bi, 0, h)),
                  pl.BlockSpec((None, t, HEAD_DIM), lambda bi, h, i: (bi, 0, v_col0 // HEAD_DIM + h))],
        out_specs=pl.BlockSpec((None, MOBA_BLOCK, HEAD_DIM), lambda bi, h, i: (bi, i, h)),
        out_shape=jax.ShapeDtypeStruct((b, t, w), BF16),
        scratch_shapes=[pltpu.VMEM((LANE, HEAD_DIM), F32), pltpu.VMEM((t, HEAD_DIM), BF16), pltpu.VMEM((t, HEAD_DIM), BF16),
                        pltpu.VMEM((LANE, nb * LANE), BF16)],
        compiler_params=_params("parallel", "parallel", "arbitrary"),
        name="moba_prompt",
    )(qn, kn, proj)


def _moba_dec_block_kernel(pt_ref, q_ref, ka_ref, kb_ref, va_ref, vb_ref,
                           acc_ref, g_ref, m_ref, l_ref, *, nh, tq):
    n = pl.program_id(1)
    q = q_ref[...]
    kb = jnp.concatenate([ka_ref[...], kb_ref[...]], axis=0).astype(BF16)
    vb = jnp.concatenate([va_ref[...], vb_ref[...]], axis=0).astype(BF16)
    s_raw = _dot_nt(q, kb)
    same_head = _iota(s_raw.shape, 1) % nh == _iota(s_raw.shape, 0) // tq
    gate = jnp.sum(jnp.where(same_head, s_raw, 0.0), axis=1, keepdims=True) * (1.0 / MOBA_BLOCK)
    s = jnp.where(same_head, s_raw * SCALE, NEG)
    m = jnp.max(s, axis=1, keepdims=True)
    p = jnp.exp(s - m)
    l = jnp.sum(p, axis=1, keepdims=True)
    acc_ref[...] = _dot(p.astype(BF16), vb)

    @pl.when(n == 0)
    def _():
        g_ref[...] = jnp.zeros_like(g_ref)
        m_ref[...] = jnp.zeros_like(m_ref)
        l_ref[...] = jnp.zeros_like(l_ref)

    lane = _iota(g_ref.shape, 1)
    g_ref[...] = jnp.where(lane == n, gate, g_ref[...])
    m_ref[...] = jnp.where(lane == n, m, m_ref[...])
    l_ref[...] = jnp.where(lane == n, l, l_ref[...])


def _moba_dec_combine_kernel(g_ref, m_ref, l_ref, acc_ref, q_ref, kn_ref, vn_ref, o_ref, *, nbk, nh, tq):
    gate = g_ref[...]
    lane = _iota(gate.shape, 1)
    row = _iota(gate.shape, 0)
    valid = lane < nbk
    gate = jnp.where(valid, gate, -jnp.inf)
    sel = valid & (_rank_before(gate, nbk) < min(MOBA_TOPK, nbk + 1))
    q = q_ref[...]
    s_own = _dot_nt(q, kn_ref[...].astype(BF16)) * SCALE
    own = (lane % nh == row // tq) & (lane // nh <= row % tq)
    m_blk = m_ref[...]
    m_tot = jnp.maximum(jnp.max(jnp.where(sel, m_blk, NEG), axis=1, keepdims=True),
                        jnp.max(jnp.where(own, s_own, NEG), axis=1, keepdims=True))
    wgt = jnp.where(sel, jnp.exp(m_blk - m_tot), 0.0)
    p_own = jnp.where(own, jnp.exp(s_own - m_tot), 0.0)
    l_tot = jnp.sum(wgt * l_ref[...], axis=1, keepdims=True) + jnp.sum(p_own, axis=1, keepdims=True)
    acc = _dot(p_own.astype(BF16), vn_ref[...].astype(BF16))
    for n in range(nbk):
        acc = acc + wgt[:, n:n + 1] * acc_ref[n]
    o_ref[...] = acc / jnp.where(l_tot > 0.0, l_tot, 1.0)


def moba_sample(qn, kn_new, v_new, cache_k, cache_v, layer, page_table):
    b, tq, w = qn.shape
    nh = w // HEAD_DIM
    rows = nh * tq
    npages = page_table.shape[1]
    ppb = MOBA_BLOCK // PAGE
    nbk = npages // ppb
    assert rows == LANE and ppb == 2 and nbk <= LANE
    qrows = qn.reshape(b, tq, nh, HEAD_DIM).transpose(0, 2, 1, 3).reshape(b, rows, HEAD_DIM)
    kn_rows = kn_new.reshape(b, tq * nh, HEAD_DIM)
    vn_rows = v_new.reshape(b, tq * nh, HEAD_DIM)

    page = lambda j: pl.BlockSpec((None, None, PAGE * nh, HEAD_DIM), lambda bi, n, pt: (layer, pt[bi, ppb * n + j], 0, 0))
    stat = pl.BlockSpec((None, rows, LANE), lambda bi, n, pt: (bi, 0, 0))
    acc, g, m, l = pl.pallas_call(
        functools.partial(_moba_dec_block_kernel, nh=nh, tq=tq),
        grid_spec=pltpu.PrefetchScalarGridSpec(
            num_scalar_prefetch=1,
            grid=(b, nbk),
            in_specs=[pl.BlockSpec((None, rows, HEAD_DIM), lambda bi, n, pt: (bi, 0, 0)),
                      page(0), page(1), page(0), page(1)],
            out_specs=[pl.BlockSpec((None, None, rows, HEAD_DIM), lambda bi, n, pt: (bi, n, 0, 0)), stat, stat, stat],
        ),
        out_shape=[jax.ShapeDtypeStruct((b, nbk, rows, HEAD_DIM), F32)] + [jax.ShapeDtypeStruct((b, rows, LANE), F32)] * 3,
        compiler_params=_params("parallel", "arbitrary"),
        name="moba_sample_blocks",
    )(page_table, qrows, cache_k, cache_k, cache_v, cache_v)

    per_b = lambda *shape: pl.BlockSpec((None,) + shape, lambda bi: (bi,) + (0,) * len(shape))
    o = pl.pallas_call(
        functools.partial(_moba_dec_combine_kernel, nbk=nbk, nh=nh, tq=tq),
        grid=(b,),
        in_specs=[per_b(rows, LANE), per_b(rows, LANE), per_b(rows, LANE), per_b(nbk, rows, HEAD_DIM),
                  per_b(rows, HEAD_DIM), per_b(tq * nh, HEAD_DIM), per_b(tq * nh, HEAD_DIM)],
        out_specs=per_b(rows, HEAD_DIM),
        out_shape=jax.ShapeDtypeStruct((b, rows, HEAD_DIM), F32),
        compiler_params=_params("parallel"),
        name="moba_sample_combine",
    )(g, m, l, acc, qrows, kn_rows, vn_rows)
    return o.reshape(b, nh, tq, HEAD_DIM).transpose(0, 2, 1, 3).reshape(b, tq, w)


def _ssd_kernel(x_ref, bm_ref, cm_ref, z_ref, dtc_ref, dtr_ref, bias_c_ref, bias_r_ref, alog_c_ref, alog_r_ref,
                dskip_ref, gout_ref, s0_ref, pre_ref, y_ref, sfin_ref, state_ref, *, q, t_valid):
    c = pl.program_id(1)
    nc = pl.num_programs(1)
    pre_w = pre_ref.shape[1]
    y_ref[:, :pre_w] = pre_ref[...]

    @pl.when(c == 0)
    def _():
        state_ref[...] = s0_ref[...]

    dtc = _softplus(dtc_ref[...] + bias_c_ref[...])
    dtr = _softplus(dtr_ref[...] + bias_r_ref[...])
    if t_valid is not None:
        dtc = jnp.where(c * q + _iota(dtc.shape, 0) < t_valid, dtc, 0.0)
        dtr = jnp.where(c * q + _iota(dtr.shape, 1) < t_valid, dtr, 0.0)
    da_c = dtc * -jnp.exp(alog_c_ref[...])
    da_r = dtr * -jnp.exp(alog_r_ref[...])
    tri = _iota((q, q), 0) >= _iota((q, q), 1)
    tri_bf = jnp.where(tri, 1.0, 0.0).astype(BF16)
    cum_c = sum(_dot(tri_bf, part) for part in _split3(da_c))
    cum_r = sum(_dot_nt(part, tri_bf) for part in _split3(da_r))
    lane = _iota((q, LANE), 1)
    lo_half = lane < SSD_HEAD_DIM
    row_lo = _iota((LANE, 1), 0) < SSD_HEAD_DIM
    pairs = HEADS_PER_GROUP // 2

    for g in range(SSD_GROUPS):
        xg = x_ref[:, g * GROUP_W:(g + 1) * GROUP_W]
        bm = bm_ref[:, g * SSD_STATE:(g + 1) * SSD_STATE].astype(BF16)
        cm = cm_ref[:, g * SSD_STATE:(g + 1) * SSD_STATE].astype(BF16)
        cb = _dot_nt(cm, bm)
        y_parts = []
        for pr in range(pairs):
            x_pair = xg[:, pr * LANE:(pr + 1) * LANE]
            x_bf = x_pair.astype(BF16)
            ys, cols, lasts = [], [], []
            for h in (g * HEADS_PER_GROUP + 2 * pr, g * HEADS_PER_GROUP + 2 * pr + 1):
                col = cum_c[:, h:h + 1]
                seg = col - cum_r[h:h + 1, :]
                decay = jnp.where(tri, jnp.exp(jnp.where(tri, seg, 0.0)), 0.0)
                wgt = cb * decay * dtr[h:h + 1, :]
                ys.append(_dot(wgt.astype(BF16), x_bf))
                cols.append(col)
                lasts.append(cum_c[q - 1:q, h:h + 1])
            sidx = g * pairs + pr
            state = state_ref[sidx]
            y_pair = jnp.where(lo_half, ys[0], ys[1])
            carry_in = _dot_nt(cm, state.astype(BF16))
            y_pair = y_pair + carry_in * jnp.exp(jnp.where(lo_half, cols[0], cols[1]))
            y_parts.append(y_pair)
            tail = jnp.where(lo_half,
                             jnp.exp(lasts[0] - cols[0]) * dtc[:, 2 * sidx:2 * sidx + 1],
                             jnp.exp(lasts[1] - cols[1]) * dtc[:, 2 * sidx + 1:2 * sidx + 2])
            upd = _dot_tn((x_pair * tail).astype(BF16), bm)
            state_ref[sidx] = state * jnp.where(row_lo, jnp.exp(lasts[0]), jnp.exp(lasts[1])) + upd
        sl = slice(g * GROUP_W, (g + 1) * GROUP_W)
        yg = jnp.concatenate(y_parts, axis=1) + xg * dskip_ref[:, sl]
        yg = yg * _silu(z_ref[:, sl])
        y_ref[:, pre_w + g * GROUP_W:pre_w + (g + 1) * GROUP_W] = _rms(yg, gout_ref[:, sl]).astype(y_ref.dtype)

    @pl.when(c == nc - 1)
    def _():
        sfin_ref[...] = state_ref[...]


def ssd_mixer(conv, z_src, z_col0, dt_raw, s0, dt_bias, a_log, d_skip, g_out, t_valid, prefix):
    b, t, _ = conv.shape
    pre_w = prefix.shape[-1]
    q = SSD_CHUNK
    assert t % q == 0
    nc = t // q
    dtr = dt_raw.transpose(0, 2, 1)
    pad_h = LANE - SSD_HEADS
    bias_c = jnp.pad(dt_bias, (0, pad_h)).reshape(1, LANE)
    bias_r = bias_c.reshape(LANE, 1)
    alog_c = jnp.pad(a_log, (0, pad_h)).reshape(1, LANE)
    alog_r = alog_c.reshape(LANE, 1)
    dskip = jnp.repeat(d_skip, SSD_HEAD_DIM).reshape(1, D_INNER)
    npair = SSD_HEADS // 2
    s0p = s0.reshape(b, npair, 2 * SSD_HEAD_DIM, SSD_STATE)
    nbc = D_INNER // (SSD_GROUPS * SSD_STATE)
    const = lambda shape: pl.BlockSpec(shape, lambda bi, c: (0,) * len(shape))
    y, sfin = pl.pallas_call(
        functools.partial(_ssd_kernel, q=q, t_valid=t_valid),
        grid=(b, nc),
        in_specs=[pl.BlockSpec((None, q, D_INNER), lambda bi, c: (bi, c, 0)),
                  pl.BlockSpec((None, q, SSD_GROUPS * SSD_STATE), lambda bi, c: (bi, c, nbc)),
                  pl.BlockSpec((None, q, SSD_GROUPS * SSD_STATE), lambda bi, c: (bi, c, nbc + 1)),
                  pl.BlockSpec((None, q, D_INNER), lambda bi, c: (bi, c, z_col0 // D_INNER)),
                  pl.BlockSpec((None, q, LANE), lambda bi, c: (bi, c, 0)),
                  pl.BlockSpec((None, LANE, q), lambda bi, c: (bi, 0, c)),
                  const((1, LANE)), const((LANE, 1)), const((1, LANE)), const((LANE, 1)),
                  const((1, D_INNER)), const((1, D_INNER)),
                  pl.BlockSpec((None, npair, LANE, SSD_STATE), lambda bi, c: (bi, 0, 0, 0)),
                  pl.BlockSpec((None, q, pre_w), lambda bi, c: (bi, c, 0))],
        out_specs=[pl.BlockSpec((None, q, pre_w + D_INNER), lambda bi, c: (bi, c, 0)),
                   pl.BlockSpec((None, npair, LANE, SSD_STATE), lambda bi, c: (bi, 0, 0, 0))],
        out_shape=[jax.ShapeDtypeStruct((b, t, pre_w + D_INNER), BF16),
                   jax.ShapeDtypeStruct((b, npair, LANE, SSD_STATE), F32)],
        scratch_shapes=[pltpu.VMEM((npair, LANE, SSD_STATE), F32)],
        compiler_params=_params("parallel", "arbitrary"),
        name="ssd_scan",
    )(conv, conv, conv, z_src, dt_raw, dtr, bias_c, bias_r, alog_c, alog_r, dskip, g_out.reshape(1, D_INNER), s0p,
      prefix)
    return y, sfin.reshape(b, SSD_HEADS, SSD_HEAD_DIM, SSD_STATE)


def _compress_kernel(pt_ref, a_ref, b_ref, cpe_a_ref, cpe_b_ref, b1_ref, w2_ref, b2_ref, g_ref, o_ref,
                     ha_ref, hb_ref, *, npg, norm):
    bi = pl.program_id(0)
    ncp = npg * 8

    def gather(p, _):
        src = pl.ds(pl.multiple_of(pt_ref[bi, p] * 8, 8), 8)
        dst = pl.ds(pl.multiple_of(p * 8, 8), 8)
        ha_ref[dst, :] = a_ref[src, :]
        hb_ref[dst, :] = b_ref[src, :]
        return 0

    lax.fori_loop(0, npg, gather, 0)
    hb_ref[ncp:ncp + 8, :] = jnp.zeros((8, HEAD_DIM), F32)
    hid = ha_ref[...] + hb_ref[pl.ds(1, ncp), :] + (cpe_a_ref[0:1, :] + cpe_b_ref[1:2, :] + b1_ref[...])
    out = _dot(_gelu_tanh(hid).astype(BF16), w2_ref[...]) + b2_ref[...]
    if norm:
        out = _rms(out, g_ref[...])
    o_ref[...] = out


ROWS_PER_TILE = 8


def _chunk_products_kernel(x_ref, w_ref, o_ref):
    @pl.when(pl.program_id(1) == 0)
    def _():
        o_ref[...] = jnp.zeros_like(o_ref)

    for r in range(ROWS_PER_TILE):
        g = r % NSA_KV
        prod = _dot(x_ref[:, r, :].astype(BF16), w_ref[r // NSA_KV])
        o_ref[:, g * HEAD_DIM:(g + 1) * HEAD_DIM] += prod[:, :HEAD_DIM]
        o_ref[:, KV_W + g * HEAD_DIM:KV_W + (g + 1) * HEAD_DIM] += prod[:, HEAD_DIM:]


def chunk_products(x, layer, w_pair):
    nck = x.shape[1]
    blk = min(nck, 1024)
    assert nck % blk == 0 and x.shape[2] == CMP_STRIDE * NSA_KV
    per_step = ROWS_PER_TILE // NSA_KV
    return pl.pallas_call(
        _chunk_products_kernel,
        grid=(nck // blk, CMP_STRIDE // per_step),
        in_specs=[pl.BlockSpec((None, blk, ROWS_PER_TILE, HEAD_DIM), lambda i, lp: (layer, i, lp, 0)),
                  pl.BlockSpec((per_step, HEAD_DIM, 2 * HEAD_DIM), lambda i, lp: (lp, 0, 0))],
        out_specs=pl.BlockSpec((blk, 2 * KV_W), lambda i, lp: (i, 0)),
        out_shape=jax.ShapeDtypeStruct((nck, 2 * KV_W), F32),
        compiler_params=_params("parallel", "arbitrary"),
        name="chunk_products",
    )(x, w_pair)


def compress_tokens(rows, layer, page_table, pe, w1, b1, w2, b2, g_norm):
    nck = rows.shape[1]
    b, npg = page_table.shape
    ncp = npg * 8
    w_pair = w1.reshape(2, CMP_STRIDE, HEAD_DIM, HEAD_DIM).transpose(1, 2, 0, 3).reshape(CMP_STRIDE, HEAD_DIM, 2 * HEAD_DIM)
    w_pair = w_pair.astype(BF16)
    ab = chunk_products(rows, layer, w_pair)
    pe_rows = jnp.broadcast_to(pe.reshape(2, CMP_STRIDE, 1, HEAD_DIM), (2, CMP_STRIDE, NSA_KV, HEAD_DIM))
    pe_rows = jnp.pad(pe_rows.reshape(1, 2, CMP_STRIDE * NSA_KV, HEAD_DIM), ((0, 0), (0, 6), (0, 0), (0, 0)))
    cpe = chunk_products(pe_rows, 0, w_pair)
    col = lambda off: (lambda bi, g, pt: (0, off + g))
    return pl.pallas_call(
        functools.partial(_compress_kernel, npg=npg, norm=g_norm is not None),
        grid_spec=pltpu.PrefetchScalarGridSpec(
            num_scalar_prefetch=1,
            grid=(b, NSA_KV),
            in_specs=[pl.BlockSpec((nck, HEAD_DIM), col(0)), pl.BlockSpec((nck, HEAD_DIM), col(NSA_KV)),
                      pl.BlockSpec((8, HEAD_DIM), col(0)), pl.BlockSpec((8, HEAD_DIM), col(NSA_KV)),
                      pl.BlockSpec((1, HEAD_DIM), lambda bi, g, pt: (0, 0)),
                      pl.BlockSpec((HEAD_DIM, HEAD_DIM), lambda bi, g, pt: (0, 0)),
                      pl.BlockSpec((1, HEAD_DIM), lambda bi, g, pt: (0, 0)),
                      pl.BlockSpec((1, HEAD_DIM), lambda bi, g, pt: (0, 0))],
            out_specs=pl.BlockSpec((None, ncp, HEAD_DIM), lambda bi, g, pt: (bi, 0, g)),
            scratch_shapes=[pltpu.VMEM((ncp, HEAD_DIM), F32), pltpu.VMEM((ncp + 8, HEAD_DIM), F32)],
        ),
        out_shape=jax.ShapeDtypeStruct((b, ncp, KV_W), F32),
        compiler_params=_params("parallel", "parallel"),
        name="nsa_compress",
    )(page_table, ab, ab, cpe, cpe, b1.reshape(1, -1), w2.astype(BF16), b2.reshape(1, -1),
      (g_norm if g_norm is not None else jnp.ones((HEAD_DIM,), F32)).reshape(1, -1))


def _nsa_cmp_kernel(q_ref, kc_ref, vc_ref, o_ref, sel_ref, *, tq, q_start, n_sel, nsp, blocks_on_rows):
    i = pl.program_id(2)
    kc = kc_ref[...].astype(BF16)
    vc = vc_ref[...].astype(BF16)
    ncp = kc.shape[0]
    qpos = q_start + i * tq + _iota((tq, 1), 0)
    end = _iota((tq, ncp), 1) * CMP_STRIDE + (CMP_LEN - 1)
    msk = end <= qpos
    imp = jnp.zeros((tq, ncp), F32)
    for r in range(NSA_HPG):
        sl = slice(r * HEAD_DIM, (r + 1) * HEAD_DIM)
        s = jnp.where(msk, _dot_nt(q_ref[:, sl], kc) * SCALE, NEG)
        p = jnp.where(msk, jnp.exp(s - jnp.max(s, axis=1, keepdims=True)), 0.0)
        d = jnp.sum(p, axis=1, keepdims=True)
        p = p / jnp.where(d > 0.0, d, 1.0)
        o_ref[:, sl] = _dot(p.astype(BF16), vc)
        imp = imp + p
    if blocks_on_rows:
        nsr = -(-n_sel // 8) * 8
        sj, cn = _iota((nsr, ncp), 0), _iota((nsr, ncp), 1)
        overlap_t = jnp.where((cn >= 4 * sj - 1) & (cn <= 4 * sj + 3), 1.0, 0.0).astype(BF16)
        score = sum(_dot_nt(overlap_t, part) for part in _split3(imp))
        j = _iota((nsr, tq), 0)
        qblk = (q_start + i * tq + _iota((nsr, tq), 1)) // SEL_BLOCK
        valid = j <= qblk
        forced = (j == 0) | (j == qblk) | (j == qblk - 1)
        score = jnp.where(valid, jnp.where(forced, jnp.inf, score), -jnp.inf)
        rank = jnp.zeros(score.shape, F32)
        for c in range(n_sel):
            sc = score[c:c + 1, :]
            rank = rank + jnp.where((sc > score) | ((sc == score) & (c < j)), 1.0, 0.0)
        sel = jnp.where(valid & (rank < min(SEL_TOPN, n_sel)), 1.0, 0.0)
        sel_ref[...] = jnp.concatenate([sel, jnp.zeros((nsp - nsr, tq), F32)], axis=0)
        return
    cn = _iota((ncp, nsp), 0)
    sj = _iota((ncp, nsp), 1)
    overlap = jnp.where((cn >= 4 * sj - 1) & (cn <= 4 * sj + 3), 1.0, 0.0).astype(BF16)
    score = sum(_dot(part, overlap) for part in _split3(imp))
    j = _iota((tq, nsp), 1)
    qblk = qpos // SEL_BLOCK
    valid = j <= qblk
    forced = (j == 0) | (j == qblk) | (j == qblk - 1)
    score = jnp.where(valid, jnp.where(forced, jnp.inf, score), -jnp.inf)
    sel_ref[...] = jnp.where(valid & (_rank_before(score, n_sel) < min(SEL_TOPN, n_sel)), 1.0, 0.0)


def nsa_compressed(qn, kc, vc, q_start, n_sel):
    b, t, w = qn.shape
    ncp = kc.shape[1]
    gw = NSA_HPG * HEAD_DIM
    tq = min(t, 256)
    nsp = -(-n_sel // LANE) * LANE
    blocks_on_rows = tq % LANE == 0
    if blocks_on_rows:
        sel_spec = pl.BlockSpec((None, None, nsp, tq), lambda bi, g, i: (bi, g, 0, i))
        sel_shape = jax.ShapeDtypeStruct((b, NSA_KV, nsp, t), F32)
    else:
        sel_spec = pl.BlockSpec((None, None, tq, nsp), lambda bi, g, i: (bi, g, i, 0))
        sel_shape = jax.ShapeDtypeStruct((b, NSA_KV, t, nsp), F32)
    return pl.pallas_call(
        functools.partial(_nsa_cmp_kernel, tq=tq, q_start=q_start, n_sel=n_sel, nsp=nsp, blocks_on_rows=blocks_on_rows),
        grid=(b, NSA_KV, t // tq),
        in_specs=[pl.BlockSpec((None, tq, gw), lambda bi, g, i: (bi, i, g)),
                  pl.BlockSpec((None, ncp, HEAD_DIM), lambda bi, g, i: (bi, 0, g)),
                  pl.BlockSpec((None, ncp, HEAD_DIM), lambda bi, g, i: (bi, 0, g))],
        out_specs=[pl.BlockSpec((None, tq, gw), lambda bi, g, i: (bi, i, g)), sel_spec],
        out_shape=[jax.ShapeDtypeStruct((b, t, w), F32), sel_shape],
        compiler_params=_params("parallel", "parallel", "parallel"),
        name="nsa_compressed",
    )(qn, kc, vc)


def _nsa_prompt_kernel(q_ref, sk_ref, sv_ref, wk_ref, wv_ref, sel_ref, ocmp_ref, gl_ref, o_ref,
                       m_ref, part_ref, acc_ref, s_ref, *, tq, tk):
    i = pl.program_id(2)
    q0 = i * tq
    tpos = q0 + _iota((tq, tk), 0)
    sel = sel_ref[...].astype(BF16)
    nsp = sel.shape[0]

    def sel_mask(k0):
        kpos = k0 + _iota((tq, tk), 1)
        expand = jnp.where(_iota((nsp, tk), 0) == (k0 + _iota((nsp, tk), 1)) // SEL_BLOCK, 1.0, 0.0).astype(BF16)
        return (_dot_tn(sel, expand) > 0.5) & (kpos <= tpos)

    def win_mask(k0):
        kpos = k0 + _iota((tq, tk), 1)
        return (kpos <= tpos) & (kpos > tpos - WINDOW)

    n_hi = (q0 + tq - 1) // tk + 1
    branches = ((sk_ref, sv_ref, sel_mask, 0), (wk_ref, wv_ref, win_mask, jnp.maximum(q0 - (WINDOW - 1), 0) // tk))
    fold = lambda a, op: functools.reduce(op, [a[:, c * LANE:(c + 1) * LANE] for c in range(tk // LANE)])
    for br, (k_ref, v_ref, mask_fn, n_lo) in enumerate(branches):
        base = br * NSA_HPG
        for r in range(NSA_HPG):
            part_ref[base + r] = jnp.full((tq, LANE), NEG, F32)
            acc_ref[base + r] = jnp.zeros((tq, HEAD_DIM), F32)

        def max_sweep(n, _, base=base, k_ref=k_ref, mask_fn=mask_fn, n_lo=n_lo):
            k0 = pl.multiple_of(n * tk, tk)
            kb, msk = k_ref[pl.ds(k0, tk), :].astype(BF16), mask_fn(k0)
            for r in range(NSA_HPG):
                s = jnp.where(msk, _dot_nt(q_ref[:, r * HEAD_DIM:(r + 1) * HEAD_DIM], kb), NEG)
                s_ref[n - n_lo, r] = s
                part_ref[base + r] = jnp.maximum(part_ref[base + r], fold(s, jnp.maximum))
            return 0

        lax.fori_loop(n_lo, n_hi, max_sweep, 0)
        for r in range(NSA_HPG):
            m_ref[base + r] = jnp.max(part_ref[base + r], axis=1, keepdims=True)
            part_ref[base + r] = jnp.zeros((tq, LANE), F32)

        def exp_sweep(n, _, base=base, v_ref=v_ref, n_lo=n_lo):
            vb = v_ref[pl.ds(pl.multiple_of(n * tk, tk), tk), :].astype(BF16)
            for r in range(NSA_HPG):
                p = jnp.exp2((s_ref[n - n_lo, r] - m_ref[base + r]) * EXP2_SCALE)
                part_ref[base + r] += fold(p, jnp.add)
                acc_ref[base + r] += _dot(p.astype(BF16), vb)
            return 0

        lax.fori_loop(n_lo, n_hi, exp_sweep, 0)

    gates = [_sigmoid(gl_ref[c]) for c in range(3)]
    for r in range(NSA_HPG):
        sl = slice(r * HEAD_DIM, (r + 1) * HEAD_DIM)
        o_sel = acc_ref[r] / jnp.sum(part_ref[r], axis=1, keepdims=True)
        o_win = acc_ref[NSA_HPG + r] / jnp.sum(part_ref[NSA_HPG + r], axis=1, keepdims=True)
        o = gates[0][:, r:r + 1] * ocmp_ref[:, sl] + gates[1][:, r:r + 1] * o_sel + gates[2][:, r:r + 1] * o_win
        o_ref[:, sl] = o.astype(o_ref.dtype)


def nsa_prompt(qn, skn, wkn, proj, sv_col0, wv_col0, selm, ocmp, gate_logits, tq=256, tk=256):
    b, t, w = qn.shape
    gw = NSA_HPG * HEAD_DIM
    tq, tk = min(t, tq), min(t, tk)
    nsp = selm.shape[2]
    assert selm.shape[3] == t and tq % LANE == 0
    kv = lambda off: pl.BlockSpec((None, t, HEAD_DIM), lambda bi, g, i: (bi, 0, off + g))
    return pl.pallas_call(
        functools.partial(_nsa_prompt_kernel, tq=tq, tk=tk),
        grid=(b, NSA_KV, t // tq),
        in_specs=[pl.BlockSpec((None, tq, gw), lambda bi, g, i: (bi, i, g)),
                  kv(0), kv(sv_col0 // HEAD_DIM), kv(0), kv(wv_col0 // HEAD_DIM),
                  pl.BlockSpec((None, None, nsp, tq), lambda bi, g, i: (bi, g, 0, i)),
                  pl.BlockSpec((None, tq, gw), lambda bi, g, i: (bi, i, g)),
                  pl.BlockSpec((3, None, None, tq, NSA_HPG), lambda bi, g, i: (0, bi, g, i, 0))],
        out_specs=pl.BlockSpec((None, tq, gw), lambda bi, g, i: (bi, i, g)),
        out_shape=jax.ShapeDtypeStruct((b, t, w), BF16),
        scratch_shapes=[pltpu.VMEM((2 * NSA_HPG, tq, 1), F32), pltpu.VMEM((2 * NSA_HPG, tq, LANE), F32),
                        pltpu.VMEM((2 * NSA_HPG, tq, HEAD_DIM), F32), pltpu.VMEM((t // tk, NSA_HPG, tq, tk), F32)],
        compiler_params=_params("parallel", "parallel", "arbitrary"),
        name="nsa_prompt",
    )(qn, skn, proj, wkn, proj, selm, ocmp, gate_logits)


def _nsa_dec_kernel(pt_ref, q_ref, sel_ref, ocmp_ref, gl_ref, *refs, tq, past_len, pages_per_step):
    kp_refs, vp_refs = refs[:pages_per_step], refs[pages_per_step:2 * pages_per_step]
    skn_ref, svn_ref, wkb_ref, wvb_ref, wkn_ref, wvn_ref, o_ref, m_ref, l_ref, acc_ref = refs[2 * pages_per_step:]
    p = pl.program_id(1)
    n_steps = pl.num_programs(1)
    rows = NSA_KV * NSA_HPG * tq
    per_g = NSA_HPG * tq
    q = q_ref[...]
    sel = sel_ref[...]

    @pl.when(p == 0)
    def _():
        m_ref[...] = jnp.full(m_ref.shape, NEG, F32)
        l_ref[...] = jnp.zeros_like(l_ref)
        acc_ref[...] = jnp.zeros_like(acc_ref)

    def grid_of(n_keys):
        shape = (rows, n_keys * NSA_KV)
        lane, row = _iota(shape, 1), _iota(shape, 0)
        return lane // NSA_KV, lane % NSA_KV == row // per_g, row % tq

    def update(carry, k_rows, v_rows, msk):
        s = _dot_nt(q, k_rows.astype(BF16)) * SCALE
        return _online_update(carry, s, msk, v_rows.astype(BF16))

    key, same_g, _ = grid_of(PAGE)
    blocks_per_page = PAGE // SEL_BLOCK
    carry = (m_ref[...], l_ref[...], acc_ref[...])
    for j in range(pages_per_step):
        blk0 = blocks_per_page * (pages_per_step * p + j)
        picked = jnp.where(key < SEL_BLOCK, _pick_col(sel, blk0), _pick_col(sel, blk0 + 1))
        carry = update(carry, kp_refs[j][...], vp_refs[j][...], same_g & (picked > 0.5))
    m_ref[...], l_ref[...], acc_ref[...] = carry

    @pl.when(p == n_steps - 1)
    def _():
        n_new = skn_ref.shape[0] // NSA_KV
        key, same_g, t_row = grid_of(n_new)
        own = same_g & (key <= t_row) & (key < tq)
        new_blk = _pick_col(sel, past_len // SEL_BLOCK) > 0.5
        o_sel = _flash_out(update(carry, skn_ref[...], svn_ref[...], own & new_blk))
        wkey, wsame, wt = grid_of(WINDOW)
        w_carry = update(_flash_init(rows, HEAD_DIM), wkb_ref[...], wvb_ref[...], wsame & (wkey > wt))
        o_win = _flash_out(update(w_carry, wkn_ref[...], wvn_ref[...], own))
        gates = _sigmoid(gl_ref[...])
        o_ref[...] = gates[:, 0:1] * ocmp_ref[...] + gates[:, 1:2] * o_sel + gates[:, 2:3] * o_win


def nsa_sample(qn, skn_new, sv_new, wkn_new, wv_new, cache_sk, cache_sv, win_k, win_v, layer,
               page_table, selm, ocmp, gate_logits):
    b, tq, w = qn.shape
    rows = NSA_KV * NSA_HPG * tq
    npages = page_table.shape[1]
    past_len = npages * PAGE
    nsp = selm.shape[-1]
    assert win_k.shape[2] == WINDOW * NSA_KV and past_len % SEL_BLOCK == 0 and tq * NSA_KV <= LANE
    selrows = jnp.broadcast_to(selm[:, :, None], (b, NSA_KV, NSA_HPG, tq, nsp)).reshape(b, rows, nsp)
    to_rows = lambda a: a.reshape(b, tq, NSA_KV, NSA_HPG, -1).transpose(0, 2, 3, 1, 4).reshape(b, rows, -1)
    new_rows = lambda a: jnp.pad(a.reshape(b, tq * NSA_KV, HEAD_DIM), ((0, 0), (0, LANE - tq * NSA_KV), (0, 0)))
    per_b = lambda *shape: pl.BlockSpec((None,) + shape, lambda bi, p, pt: (bi,) + (0,) * len(shape))
    pps = 4 if npages % 4 == 0 else 1
    pages = [pl.BlockSpec((None, None, PAGE * NSA_KV, HEAD_DIM), lambda bi, p, pt, j=j: (layer, pt[bi, pps * p + j], 0, 0))
             for j in range(pps)]
    wbuf = pl.BlockSpec((None, None, WINDOW * NSA_KV, HEAD_DIM), lambda bi, p, pt: (layer, bi, 0, 0))
    o = pl.pallas_call(
        functools.partial(_nsa_dec_kernel, tq=tq, past_len=past_len, pages_per_step=pps),
        grid_spec=pltpu.PrefetchScalarGridSpec(
            num_scalar_prefetch=1,
            grid=(b, npages // pps),
            in_specs=[per_b(rows, HEAD_DIM), per_b(rows, nsp), per_b(rows, HEAD_DIM), per_b(rows, 3)]
                     + pages + pages + [per_b(LANE, HEAD_DIM), per_b(LANE, HEAD_DIM), wbuf, wbuf,
                                        per_b(LANE, HEAD_DIM), per_b(LANE, HEAD_DIM)],
            out_specs=per_b(rows, HEAD_DIM),
            scratch_shapes=[pltpu.VMEM((rows, 1), F32), pltpu.VMEM((rows, 1), F32), pltpu.VMEM((rows, HEAD_DIM), F32)],
        ),
        out_shape=jax.ShapeDtypeStruct((b, rows, HEAD_DIM), F32),
        compiler_params=_params("parallel", "arbitrary"),
        name="nsa_sample",
    )(page_table, to_rows(qn), selrows, to_rows(ocmp), to_rows(gate_logits), *([cache_sk] * pps), *([cache_sv] * pps),
      new_rows(skn_new), new_rows(sv_new), win_k, win_v, new_rows(wkn_new), new_rows(wv_new))
    return o.reshape(b, NSA_KV, NSA_HPG, tq, HEAD_DIM).transpose(0, 3, 1, 2, 4).reshape(b, tq, w)


MOBA_W = MOBA_HEADS * HEAD_DIM
CONV_DIM = D_INNER + 2 * SSD_GROUPS * SSD_STATE
EVEN_MAIN = 3 * MOBA_W + D_INNER + CONV_DIM
Z_COL0 = 0
XBC_COL0 = D_INNER
Q_COL0 = XBC_COL0 + CONV_DIM
K_COL0 = Q_COL0 + MOBA_W
V_COL0 = K_COL0 + MOBA_W
NSA_Q = NSA_KV * NSA_HPG * HEAD_DIM
ODD_MAIN = NSA_Q + 6 * KV_W
D_FF = 8192


PROJ_TN = 1024
PROJ_TM = 512
PROJ_VMEM_LIMIT_BYTES = 60 * 1024 * 1024
WIDE = dict(tm=PROJ_TM, tn=PROJ_TN, vmem_limit=PROJ_VMEM_LIMIT_BYTES)


def _ws(a, w, layer, res=None, **kw):
    return a, w, layer, res, kw


def _run_pair(big, small):
    req_big, req_small = next(big), next(small)
    while True:
        a, w, layer, res, kw = req_big
        out_big, out_small = matmul_ws(a, w, layer, res=res, small=req_small[0], small_res=req_small[3], **kw)
        try:
            req_big = big.send(out_big)
        except StopIteration as done_big:
            try:
                small.send(out_small)
            except StopIteration as done_small:
                return done_big.value, done_small.value
            raise AssertionError("the two groups must issue the same matmul sequence")
        req_small = small.send(out_small)


def _proj(h2d, w, layer, main, out_cols=None):
    wt = jnp.swapaxes(w, 1, 2)
    tail = jnp.pad(wt[layer, main:, :], ((0, LANE - (w.shape[2] - main)), (0, 0)))
    main_out = yield _ws(h2d, wt, layer, k=h2d.shape[1], n=main, out_cols=out_cols, w_t=True, **WIDE)
    return main_out, matmul(h2d, tail, b_t=True)


def _even_layer(x, sample, q_start, cache_k, cache_v, e, page_table, conv_prev, ssm_prev,
                g_norm, w_in, w_out, qk_g, conv_w, conv_b, dt_bias, a_log, d_skip, g_out):
    b, t, d = x.shape
    x2 = x.reshape(b * t, d)
    h = rmsnorm_rows(x2, g_norm)
    n_tiles = EVEN_MAIN // PROJ_TN
    proj, dt_raw = yield from _proj(h, w_in, e, EVEN_MAIN, out_cols=lambda j: (j + Q_COL0 // PROJ_TN) % n_tiles)
    proj = proj.reshape(b, t, EVEN_MAIN)
    dt_raw = dt_raw.reshape(b, t, LANE)
    qn = head_norm(proj, Q_COL0, MOBA_HEADS, qk_g[0], BF16)
    kn = head_norm(proj, K_COL0, MOBA_HEADS, qk_g[1], F32)
    v = proj[..., V_COL0:]
    if sample:
        o_attn = moba_sample(qn, kn, v, cache_k, cache_v, e, page_table).astype(BF16)
    else:
        o_attn = moba_prompt(qn, kn, proj, V_COL0)
    conv = dwconv(proj, XBC_COL0, CONV_DIM, _history8(conv_prev, b, CONV_DIM), conv_w, conv_b, False, F32)
    xbc = proj[..., XBC_COL0:Q_COL0]
    conv_state = jnp.concatenate([_history8(conv_prev, b, CONV_DIM), xbc], axis=1)[:, -(SSD_CONV - 1):]
    if t % SSD_CHUNK:
        tp = -(-t // SSD_CHUNK) * SSD_CHUNK
        padt = lambda a: jnp.pad(a, ((0, 0), (0, tp - t), (0, 0)))
        mixed, ssm = ssd_mixer(padt(conv), padt(proj[..., Z_COL0:XBC_COL0]), 0, padt(dt_raw), ssm_prev,
                               dt_bias, a_log, d_skip, g_out, t, padt(o_attn))
        mixed = mixed[:, :t]
    else:
        mixed, ssm = ssd_mixer(conv, proj, Z_COL0, dt_raw, ssm_prev, dt_bias, a_log, d_skip, g_out, None, o_attn)
    mixed = mixed.reshape(b * t, MOBA_W + D_INNER)
    x2 = yield _ws(mixed, w_out, e, res=x2, k=MOBA_W + D_INNER, n=d, tk=MOBA_W)
    heads = lambda a: a.reshape(b, t, MOBA_HEADS, HEAD_DIM)
    return x2.reshape(b, t, d), heads(kn), heads(v), conv_state, ssm


def _odd_layer(x, sample, q_start, caches, e, page_table, win_k, win_v,
               g_norm, w_in, w_out, qk_g, pe, w1, b1, w2, b2):
    b, t, d = x.shape
    x2 = x.reshape(b * t, d)
    h = rmsnorm_rows(x2, g_norm)
    proj, gl = yield from _proj(h, w_in, e, ODD_MAIN)
    proj = proj.reshape(b, t, ODD_MAIN)
    gl = gl[:, :3 * NSA_KV * NSA_HPG].reshape(b, t, NSA_KV, NSA_HPG, 3)
    col = lambda i: NSA_Q + i * KV_W
    qn = head_norm(proj, 0, NSA_KV * NSA_HPG, qk_g[0], BF16)
    skn = head_norm(proj, col(2), NSA_KV, qk_g[2], F32)
    wkn = head_norm(proj, col(4), NSA_KV, qk_g[3], F32)
    ck, cv, sv, wv = (proj[..., col(i):col(i + 1)] for i in (0, 1, 3, 5))
    if sample:
        cache_ck, cache_cv, cache_sk, cache_sv = caches
        assert t < CMP_STRIDE
        chunks = lambda c: c.reshape(c.shape[0], -1, CMP_STRIDE * NSA_KV, HEAD_DIM)
        rows_k, rows_v, pt, src = chunks(cache_ck), chunks(cache_cv), page_table, e
        n_rows = page_table.shape[1] * PAGE + t
    else:
        chunks = lambda a: a.reshape(1, b * t // CMP_STRIDE, CMP_STRIDE * NSA_KV, HEAD_DIM)
        rows_k, rows_v, src = chunks(ck), chunks(cv), 0
        npg = t // PAGE
        pt = (jnp.arange(b, dtype=jnp.int32)[:, None] * npg + jnp.arange(npg, dtype=jnp.int32)[None, :])
        n_rows = t
    kc = compress_tokens(rows_k, src, pt, pe[0], w1[0], b1[0], w2[0], b2[0], qk_g[1])
    vc = compress_tokens(rows_v, src, pt, pe[1], w1[1], b1[1], w2[1], b2[1], None)
    n_sel = -(-n_rows // SEL_BLOCK)
    ocmp, selm = nsa_compressed(qn, kc, vc, q_start, n_sel)
    if sample:
        rows_view = lambda c: c.reshape(c.shape[0], c.shape[1], c.shape[2] * NSA_KV, HEAD_DIM)
        o = nsa_sample(qn, skn, sv, wkn, wv, rows_view(cache_sk), rows_view(cache_sv), rows_view(win_k),
                       rows_view(win_v), e, page_table, selm, ocmp, gl).astype(BF16)
        wk_all = jnp.concatenate([win_k[e].reshape(b, -1, KV_W), wkn], axis=1)
        wv_all = jnp.concatenate([win_v[e].reshape(b, -1, KV_W), wv], axis=1)
    else:
        o = nsa_prompt(qn, skn, wkn, proj, col(3), col(5), selm, ocmp, gl.transpose(4, 0, 2, 1, 3))
        wk_all, wv_all = wkn, wv
    x2 = yield _ws(o.reshape(b * t, NSA_Q), w_out, e, res=x2, k=NSA_Q, n=d)
    keep = min(WINDOW, wk_all.shape[1])
    kvh = lambda a: a.reshape(b, a.shape[1], NSA_KV, HEAD_DIM)
    return (x2.reshape(b, t, d), kvh(ck), kvh(cv), kvh(skn), kvh(sv),
            kvh(wk_all[:, wk_all.shape[1] - keep:]), kvh(wv_all[:, wv_all.shape[1] - keep:]))


def _conv_ffn(x, conv_prev, li, g_norm, w_up, conv_w, conv_b, w_down):
    b, t, d = x.shape
    x2 = x.reshape(b * t, d)
    u = (yield _ws(rmsnorm_rows(x2, g_norm), w_up, li, k=d, n=2 * D_FF, **WIDE)).reshape(b, t, 2 * D_FF)
    hist = _history8(conv_prev, b, 2 * D_FF)
    act = dwconv(u, 0, D_FF, hist, conv_w, conv_b, True, BF16)
    state = jnp.concatenate([hist, u], axis=1)[:, -(FFN_CONV - 1):]
    x2 = yield _ws(act.reshape(b * t, D_FF), w_down, li, res=x2, k=D_FF, n=d)
    return x2.reshape(b, t, d), state


def kernel(x_prompt, x_sample, cache_moba_k, cache_moba_v, state_ssd, state_ssd_conv, cache_nsa_cmp_k, cache_nsa_cmp_v, cache_nsa_sel_k, cache_nsa_sel_v, state_nsa_win_k, state_nsa_win_v, state_ffn_conv, page_table, norm_mix, norm_ffn, even_w_in, even_w_out, moba_qk_norm, ssd_conv_w, ssd_conv_b, ssd_dt_bias, ssd_a_log, ssd_d, ssd_norm, odd_w_in, odd_w_out, nsa_qk_norm, cmp_pe, cmp_w1, cmp_b1, cmp_w2, cmp_b2, ffn_w_up, ffn_conv_w, ffn_conv_b, ffn_w_down):
    depth = norm_mix.shape[0]
    past_len = page_table.shape[1] * PAGE
    state_keys = ("moba_k", "moba_v", "ssd", "ssd_conv", "nsa_cmp_k", "nsa_cmp_v",
                  "nsa_sel_k", "nsa_sel_v", "nsa_win_k", "nsa_win_v", "ffn_conv")
    moba_pool = lambda c: c.reshape(c.shape[0], c.shape[1], PAGE * MOBA_HEADS, HEAD_DIM)
    cache_mk, cache_mv = moba_pool(cache_moba_k), moba_pool(cache_moba_v)

    def trunk(x, q_start, sample):
        b = x.shape[0]
        new = {name: [] for name in state_keys}
        for li in range(depth):
            e = li // 2
            if li % 2 == 0:
                cp = state_ssd_conv[e] if sample else None
                sp = state_ssd[e] if sample else jnp.zeros((b, SSD_HEADS, SSD_HEAD_DIM, SSD_STATE), F32)
                x, k, v, cs, ss = yield from _even_layer(x, sample, q_start, cache_mk, cache_mv, e, page_table, cp, sp,
                                                         norm_mix[li], even_w_in, even_w_out, moba_qk_norm[e],
                                                         ssd_conv_w[e], ssd_conv_b[e], ssd_dt_bias[e], ssd_a_log[e],
                                                         ssd_d[e], ssd_norm[e])
                for name, val in zip(("moba_k", "moba_v", "ssd_conv", "ssd"), (k, v, cs, ss)):
                    new[name].append(val)
            else:
                caches = (cache_nsa_cmp_k, cache_nsa_cmp_v, cache_nsa_sel_k, cache_nsa_sel_v)
                outs = yield from _odd_layer(x, sample, q_start, caches, e, page_table, state_nsa_win_k,
                                             state_nsa_win_v, norm_mix[li], odd_w_in, odd_w_out, nsa_qk_norm[e],
                                             cmp_pe[e], cmp_w1[e], cmp_b1[e], cmp_w2[e], cmp_b2[e])
                x = outs[0]
                for name, val in zip(("nsa_cmp_k", "nsa_cmp_v", "nsa_sel_k", "nsa_sel_v", "nsa_win_k", "nsa_win_v"), outs[1:]):
                    new[name].append(val)
            fp = state_ffn_conv[li] if sample else None
            x, fs = yield from _conv_ffn(x, fp, li, norm_ffn[li], ffn_w_up, ffn_conv_w[li], ffn_conv_b[li], ffn_w_down)
            new["ffn_conv"].append(fs)
        return x, {name: jnp.stack(rows) for name, rows in new.items()}

    (y_prompt, sp), (y_sample, ss) = _run_pair(trunk(x_prompt, 0, False), trunk(x_sample, past_len, True))
    out = [y_prompt, y_sample]
    for name in state_keys:
        out += [sp[name], ss[name]]
    return tuple(out)
```
